```python
import jax, jax.numpy as jnp
from jax import lax
import numpy as np

D_MODEL = 2048
BATCH = 2
SEQ = 16384
DEPTH = 2

CTX_LEN = 256
GRID_W = 64
N_MOD = 9
D_FF = 5632
FNET_GROUPS = 8
FNET_GROUP_DIM = D_MODEL // FNET_GROUPS
RWKV_HEAD = 64
RWKV_HEADS = D_MODEL // RWKV_HEAD
DECAY_LORA = 96
AAA_LORA = 96
GATE_LORA = 256
N_DIR = 2
NORM_EPS = 1e-6
GN_EPS = 64e-5

kernel_name = 'hybrid_fnet_rwkv7_dit_trunk'


def rms_norm(x, g):
    x32 = x.astype(jnp.float32)
    y = x32 * lax.rsqrt(jnp.mean(x32 * x32, axis=-1, keepdims=True) + NORM_EPS)
    return (y * g.astype(jnp.float32)).astype(x.dtype)


def modulate(x, g, shift, scale):
    return rms_norm(x, g) * (1 + scale) + shift


def swiglu(h, w13, w2):
    gate, up = jnp.split(h @ w13, 2, axis=-1)
    return (jax.nn.silu(gate) * up) @ w2


def fourier_mix(h, w_o, b_o):
    B_, L, D = h.shape
    hg = h.astype(jnp.float32).reshape(B_, L, FNET_GROUPS, FNET_GROUP_DIM)
    f = jnp.fft.fftn(hg, axes=(1, 3), norm='ortho').real
    return f.reshape(B_, L, D).astype(h.dtype) @ w_o + b_o


def grid_shift(h):
    B_, L, D = h.shape
    rows = L // GRID_W
    q = D // 4
    g = h.reshape(B_, rows, GRID_W, D)
    left = jnp.pad(g[:, :, :-1, :q], ((0, 0), (0, 0), (1, 0), (0, 0)))
    right = jnp.pad(g[:, :, 1:, q:2 * q], ((0, 0), (0, 0), (0, 1), (0, 0)))
    up = jnp.pad(g[:, :-1, :, 2 * q:3 * q], ((0, 0), (1, 0), (0, 0), (0, 0)))
    down = jnp.pad(g[:, 1:, :, 3 * q:], ((0, 0), (0, 1), (0, 0), (0, 0)))
    return jnp.concatenate([left, right, up, down], axis=-1).reshape(B_, L, D)


def seq_shift(h):
    q = h.shape[-1] // 4
    prev = jnp.pad(h[:, :-1], ((0, 0), (1, 0), (0, 0)))
    nxt = jnp.pad(h[:, 1:], ((0, 0), (0, 1), (0, 0)))
    return jnp.concatenate([prev[..., :q], nxt[..., q:2 * q], prev[..., 2 * q:3 * q], nxt[..., 3 * q:]], axis=-1)


def rwkv_project(h, hs, mu, w_rkv, w0, w1, w2, a0, a1, a2, g1, g2, k_k, k_a):
    B_, L, D = h.shape
    hm = h[None] + (hs - h)[None] * mu[:, None, None, :]
    r, k, v = jnp.einsum('pbld,pde->pble', hm[:3], w_rkv)
    lora_w = jnp.einsum('zblr,zrd->zbld', jnp.tanh(jnp.einsum('bld,zdr->zblr', hm[3], w1)), w2)
    log_w = -jax.nn.softplus(-(w0[:, None, None, :] + lora_w)) - 0.5
    decay = jnp.exp(-jnp.exp(log_w))
    a = jax.nn.sigmoid(a0[:, None, None, :] + jnp.einsum('zblr,zrd->zbld', jnp.einsum('bld,zdr->zblr', hm[4], a1), a2))
    g = jax.nn.sigmoid(hm[5] @ g1) @ g2
    kk32 = (k * k_k).astype(jnp.float32).reshape(B_, L, RWKV_HEADS, RWKV_HEAD)
    kk32 = kk32 * lax.rsqrt(jnp.maximum(jnp.sum(kk32 * kk32, axis=-1, keepdims=True), 1e-24))
    kk = kk32.reshape(B_, L, D).astype(h.dtype)
    k_dir = k[None] * (1 + (a - 1) * k_a)
    b_dir = kk[None] * a
    return r, v, g, kk, decay, k_dir, b_dir


def wkv_step(S, inp):
    r, w, k, v, a, b = inp
    sa = jnp.einsum('bhij,bhj->bhi', S, a)
    S = S * w[:, :, None, :] + sa[..., None] * b[:, :, None, :] + v[..., None] * k[:, :, None, :]
    return S, jnp.einsum('bhij,bhj->bhi', S, r)


def wkv_scan(S0, reverse, r, w, k, v, a, b):
    B_, L, _ = r.shape
    seq = tuple(jnp.moveaxis(t.astype(jnp.float32).reshape(B_, L, RWKV_HEADS, RWKV_HEAD), 1, 0)
                for t in (r, w, k, v, a, b))
    S, ys = lax.scan(wkv_step, S0, seq, reverse=reverse)
    return S, jnp.moveaxis(ys, 0, 1)


def rwkv_output(y, r, k_bonus, v, g, r_k, ln_w, ln_b, w_o):
    B_, L = y.shape[:2]
    mean = jnp.mean(y, axis=-1, keepdims=True)
    var = jnp.mean(jnp.square(y - mean), axis=-1, keepdims=True)
    yn = (y - mean) * lax.rsqrt(var + GN_EPS)
    yn = yn * ln_w.reshape(RWKV_HEADS, RWKV_HEAD).astype(jnp.float32) + ln_b.reshape(RWKV_HEADS, RWKV_HEAD).astype(jnp.float32)
    rh = r.reshape(B_, L, RWKV_HEADS, RWKV_HEAD)
    kh = k_bonus.reshape(B_, L, RWKV_HEADS, RWKV_HEAD)
    vh = v.reshape(B_, L, RWKV_HEADS, RWKV_HEAD)
    bonus = jnp.sum(rh * kh * r_k, axis=-1, keepdims=True) * vh
    o = (yn.astype(r.dtype) + bonus).reshape(B_, L, D_MODEL) * g
    return o @ w_o


def rwkv_mix(h_ctx, h_lat, need_ctx_out, mu, w_rkv, w0, w1, w2, a0, a1, a2, g1, g2, k_k, k_a, r_k, ln_w, ln_b, w_o):
    p = (mu, w_rkv, w0, w1, w2, a0, a1, a2, g1, g2, k_k, k_a)
    r_c, v_c, g_c, kk_c, w_c, k_c, b_c = rwkv_project(h_ctx, seq_shift(h_ctx), *p)
    r_l, v_l, g_l, kk_l, w_l, k_l, b_l = rwkv_project(h_lat, grid_shift(h_lat), *p)
    B_ = h_lat.shape[0]
    S0 = jnp.zeros((B_, RWKV_HEADS, RWKV_HEAD, RWKV_HEAD), jnp.float32)
    S_f, yf_c = wkv_scan(S0, False, r_c, w_c[0], k_c[0], v_c, -kk_c, b_c[0])
    S_b, yb_c = wkv_scan(S0, True, r_c, w_c[1], k_c[1], v_c, -kk_c, b_c[1])
    _, yf_l = wkv_scan(S_f, False, r_l, w_l[0], k_l[0], v_l, -kk_l, b_l[0])
    _, yb_l = wkv_scan(S_b, True, r_l, w_l[1], k_l[1], v_l, -kk_l, b_l[1])
    o_lat = rwkv_output(yf_l + yb_l, r_l, 0.5 * (k_l[0] + k_l[1]), v_l, g_l, r_k, ln_w, ln_b, w_o)
    o_ctx = None
    if need_ctx_out:
        o_ctx = rwkv_output(yf_c + yb_c, r_c, 0.5 * (k_c[0] + k_c[1]), v_c, g_c, r_k, ln_w, ln_b, w_o)
    return o_ctx, o_lat


def setup_inputs(seed: int = 0) -> dict:
    key = jax.random.key(seed)
    ks = jax.random.split(key, 32)
    n_a = (DEPTH + 1) // 2
    n_b = DEPTH // 2
    D = D_MODEL

    def nrm(k, shape, s):
        return jax.random.normal(k, shape, jnp.float32) * s

    return {
        'x': nrm(ks[0], (BATCH, SEQ, D), 1.0),
        'c': nrm(ks[1], (BATCH, D), 1.0),
        'ctx': nrm(ks[2], (BATCH, CTX_LEN, D), 1.0),
        'c_ctx': nrm(ks[3], (D,), 1.0),
        'mod_w': nrm(ks[4], (DEPTH, D, N_MOD * D), 0.5 * D ** -0.5),
        'mod_b': nrm(ks[5], (DEPTH, N_MOD * D), 0.01),
        'norm_w': 1.0 + nrm(ks[6], (DEPTH, 3, D), 0.02),
        'ffn_w13': nrm(ks[7], (DEPTH, 2, D, 2 * D_FF), D ** -0.5),
        'ffn_w2': nrm(ks[8], (DEPTH, 2, D_FF, D), D_FF ** -0.5),
        'fnet_w_o': nrm(ks[9], (n_a, D, D), D ** -0.5),
        'fnet_b_o': nrm(ks[10], (n_a, D), 0.01),
        'rwkv_mu': jax.random.uniform(ks[11], (n_b, 6, D), jnp.float32),
        'rwkv_w_rkv': nrm(ks[12], (n_b, 3, D, D), D ** -0.5),
        'rwkv_w0': jax.random.uniform(ks[13], (n_b, N_DIR, D), jnp.float32, minval=-6.5, maxval=-1.5),
        'rwkv_w1': nrm(ks[14], (n_b, N_DIR, D, DECAY_LORA), D ** -0.5),
        'rwkv_w2': nrm(ks[15], (n_b, N_DIR, DECAY_LORA, D), 0.1 * DECAY_LORA ** -0.5),
        'rwkv_a0': nrm(ks[16], (n_b, N_DIR, D), 0.1),
        'rwkv_a1': nrm(ks[17], (n_b, N_DIR, D, AAA_LORA), D ** -0.5),
        'rwkv_a2': nrm(ks[18], (n_b, N_DIR, AAA_LORA, D), 0.1 * AAA_LORA ** -0.5),
        'rwkv_g1': nrm(ks[19], (n_b, D, GATE_LORA), D ** -0.5),
        'rwkv_g2': nrm(ks[20], (n_b, GATE_LORA, D), GATE_LORA ** -0.5),
        'rwkv_k_k': 0.85 + nrm(ks[21], (n_b, D), 0.05),
        'rwkv_k_a': 1.0 + nrm(ks[22], (n_b, D), 0.05),
        'rwkv_r_k': nrm(ks[23], (n_b, RWKV_HEADS, RWKV_HEAD), 0.1),
        'rwkv_ln_w': 1.0 + nrm(ks[24], (n_b, D), 0.02),
        'rwkv_ln_b': nrm(ks[25], (n_b, D), 0.01),
        'rwkv_w_o': nrm(ks[26], (n_b, D, D), D ** -0.5),
        'final_norm_w': 1.0 + nrm(ks[27], (D,), 0.02),
    }


def reference(x, c, ctx, c_ctx, mod_w, mod_b, norm_w, ffn_w13, ffn_w2, fnet_w_o, fnet_b_o,
              rwkv_mu, rwkv_w_rkv, rwkv_w0, rwkv_w1, rwkv_w2, rwkv_a0, rwkv_a1, rwkv_a2,
              rwkv_g1, rwkv_g2, rwkv_k_k, rwkv_k_a, rwkv_r_k, rwkv_ln_w, rwkv_ln_b, rwkv_w_o,
              final_norm_w):
    B_ = x.shape[0]
    sc = jax.nn.silu(c)
    sc_ctx = jax.nn.silu(c_ctx)
    xl, xc = x, ctx
    for i in range(DEPTH):
        last = i == DEPTH - 1
        kind = i % 2
        j = i // 2
        ml = (sc @ mod_w[i] + mod_b[i]).reshape(B_, N_MOD, D_MODEL)[:, :, None, :]
        mc = (sc_ctx @ mod_w[i] + mod_b[i]).reshape(N_MOD, D_MODEL)
        ctx_live = (not last) or kind == 1

        xl = xl + 0.5 * ml[:, 2] * swiglu(modulate(xl, norm_w[i, 0], ml[:, 0], ml[:, 1]), ffn_w13[i, 0], ffn_w2[i, 0])
        if ctx_live:
            xc = xc + 0.5 * mc[2] * swiglu(modulate(xc, norm_w[i, 0], mc[0], mc[1]), ffn_w13[i, 0], ffn_w2[i, 0])

        hl = modulate(xl, norm_w[i, 1], ml[:, 3], ml[:, 4])
        if kind == 0:
            o_l = fourier_mix(hl, fnet_w_o[j], fnet_b_o[j])
            if not last:
                hc = modulate(xc, norm_w[i, 1], mc[3], mc[4])
                xc = xc + mc[5] * fourier_mix(hc, fnet_w_o[j], fnet_b_o[j])
        else:
            hc = modulate(xc, norm_w[i, 1], mc[3], mc[4])
            o_c, o_l = rwkv_mix(hc, hl, not last, rwkv_mu[j], rwkv_w_rkv[j], rwkv_w0[j], rwkv_w1[j], rwkv_w2[j],
                                rwkv_a0[j], rwkv_a1[j], rwkv_a2[j], rwkv_g1[j], rwkv_g2[j], rwkv_k_k[j],
                                rwkv_k_a[j], rwkv_r_k[j], rwkv_ln_w[j], rwkv_ln_b[j], rwkv_w_o[j])
            if not last:
                xc = xc + mc[5] * o_c
        xl = xl + ml[:, 5] * o_l

        xl = xl + 0.5 * ml[:, 8] * swiglu(modulate(xl, norm_w[i, 2], ml[:, 6], ml[:, 7]), ffn_w13[i, 1], ffn_w2[i, 1])
        if not last:
            xc = xc + 0.5 * mc[8] * swiglu(modulate(xc, norm_w[i, 2], mc[6], mc[7]), ffn_w13[i, 1], ffn_w2[i, 1])
    return rms_norm(xl, final_norm_w)
```

```python
import functools
import math

import numpy as np
import jax
import jax.numpy as jnp
from jax import lax
from jax.experimental import pallas as pl
from jax.experimental.pallas import tpu as pltpu

F32 = jnp.float32
BF16 = jnp.bfloat16

NORM_EPS = 1e-6
GN_EPS = 64e-5
KK_EPS = 1e-24
GRID_W = 64
FNET_GROUPS = 8
RWKV_HEAD = 64
N_MOD = 9
DECAY_SCALE = math.exp(-0.5)

LANE = 128
WKV_CHUNK = 64
WKV_HEADS_PER_BLOCK = 4
VMEM_LIMIT = 56 * 1024 * 1024


def _cparams(sem):
    return pltpu.CompilerParams(dimension_semantics=sem, vmem_limit_bytes=VMEM_LIMIT)


def _sigmoid(x):
    return 1.0 / (1.0 + jnp.exp(-x))


def _modulate(x, nw, shift, scale):
    y = x * lax.rsqrt(jnp.mean(x * x, axis=-1, keepdims=True) + NORM_EPS)
    return (y * nw) * (1.0 + scale) + shift


def _dot(a, b):
    return jnp.dot(a, b, preferred_element_type=F32)


def _dot_nt(a, b):
    return lax.dot_general(a, b, (((1,), (1,)), ((), ())), preferred_element_type=F32)


def _split2(x):
    hi = x.astype(BF16)
    lo = (x - hi.astype(F32)).astype(BF16)
    return hi, lo


def _split3(x):
    hi = x.astype(BF16)
    r1 = x - hi.astype(F32)
    mid = r1.astype(BF16)
    lo = (r1 - mid.astype(F32)).astype(BF16)
    return hi, mid, lo


def _dot_x2(x, w_bf16):
    hi, lo = _split2(x)
    return _dot(hi, w_bf16) + _dot(lo, w_bf16)


def _adaln_kernel(c_ref, w_ref, b_ref, o_ref):
    c = c_ref[...]
    s = c * _sigmoid(c)
    o_ref[...] = _dot(s.astype(BF16), w_ref[...].astype(BF16)) + b_ref[...]


def _adaln(c8, mod_w, mod_b):
    depth, d, nd = mod_w.shape
    tn = 1024 if nd % 1024 == 0 else 512
    return pl.pallas_call(
        _adaln_kernel,
        out_shape=jax.ShapeDtypeStruct((depth, 8, nd), F32),
        grid=(depth, nd // tn),
        in_specs=[
            pl.BlockSpec((8, d), lambda l, j: (0, 0)),
            pl.BlockSpec((None, d, tn), lambda l, j: (l, 0, j)),
            pl.BlockSpec((None, 1, tn), lambda l, j: (l, 0, j)),
        ],
        out_specs=pl.BlockSpec((None, 8, tn), lambda l, j: (l, 0, j)),
        compiler_params=_cparams(("arbitrary", "arbitrary")),
        name="adaln_linear",
    )(c8, mod_w, mod_b.reshape(depth, 1, nd))


def _ffn_kernel(*refs, final_norm, emit_h):
    x_ref, sh_ref, sc_ref, gt_ref, nw_ref, w1g_ref, w1u_ref, w2_ref, fw_ref = refs[:9]
    if emit_h:
        nw2_ref, sh2_ref, sc2_ref, o_ref, h2_ref, h_ref, acc_ref = refs[9:]
    else:
        o_ref, h_ref, acc_ref = refs[9:]
    j = pl.program_id(1)

    @pl.when(j == 0)
    def _():
        h_ref[...] = _modulate(x_ref[...], nw_ref[...], sh_ref[...], sc_ref[...]).astype(BF16)
        acc_ref[...] = jnp.zeros_like(acc_ref)

    h = h_ref[...]
    g = _dot(h, w1g_ref[...])
    u = _dot(h, w1u_ref[...])
    a = (g * _sigmoid(g)) * u
    acc_ref[...] += _dot(a.astype(BF16), w2_ref[...])

    @pl.when(j == pl.num_programs(1) - 1)
    def _():
        y = x_ref[...] + (0.5 * gt_ref[...]) * acc_ref[...]
        if final_norm:
            y = y * lax.rsqrt(jnp.mean(y * y, axis=-1, keepdims=True) + NORM_EPS) * fw_ref[...]
        o_ref[...] = y
        if emit_h:
            h2_ref[...] = _modulate(y, nw2_ref[...], sh2_ref[...], sc2_ref[...])


def _ffn(x, mods, row_of_tile, sub, nw, w13, w2, fw, *, tm, final_norm=False, next_nw=None):
    n, d = x.shape
    f = w2.shape[0]
    tf = 512
    nf = f // tf
    emit_h = next_nw is not None

    def mod_spec(k):
        return pl.BlockSpec((None, 1, d), lambda i, j: (row_of_tile(i) * N_MOD + 3 * sub + k, 0, 0))

    vec_spec = pl.BlockSpec((1, d), lambda i, j: (0, 0))
    tile_spec = pl.BlockSpec((tm, d), lambda i, j: (i, 0))
    out = jax.ShapeDtypeStruct((n, d), F32)
    args = [x, mods, mods, mods, nw, w13, w13, w2, fw]
    specs = [tile_spec, mod_spec(0), mod_spec(1), mod_spec(2), vec_spec,
             pl.BlockSpec((d, tf), lambda i, j: (0, j)),
             pl.BlockSpec((d, tf), lambda i, j: (0, nf + j)),
             pl.BlockSpec((tf, d), lambda i, j: (j, 0)),
             vec_spec]
    if emit_h:
        args += [next_nw, mods, mods]
        specs += [vec_spec, mod_spec(3), mod_spec(4)]
    return pl.pallas_call(
        functools.partial(_ffn_kernel, final_norm=final_norm, emit_h=emit_h),
        out_shape=[out, out] if emit_h else out,
        grid=(n // tm, nf),
        in_specs=specs,
        out_specs=[tile_spec, tile_spec] if emit_h else tile_spec,
        scratch_shapes=[pltpu.VMEM((tm, d), BF16), pltpu.VMEM((tm, d), F32)],
        compiler_params=_cparams(("arbitrary", "arbitrary")),
        name="ffn_swiglu",
    )(*args)


def _dft_tables(n):
    k = np.arange(n, dtype=np.int64)
    ang = 2.0 * np.pi * ((k[:, None] * k[None, :]) % n).astype(np.float64) / n
    return np.cos(ang), np.sin(ang)


def _hi_lo(m):
    m = jnp.asarray(m, F32)
    hi = m.astype(BF16)
    return hi, (m - hi.astype(F32)).astype(BF16)


DFT_SUB = 8
DFT_COLS = 512


def _dft_kernel(*refs, n_planes, mode, twiddle):
    it = iter(refs)
    x_ref = next(it)
    mats = [(next(it)[...], next(it)[...]) for _ in range(n_planes)]
    if twiddle:
        tc_ref, ts_ref = next(it), next(it)
    o_ref = next(it)

    def transform(planes):
        y = None
        for x, (mh, ml) in zip(planes, mats):
            xh, xl = _split2(x)
            part = _dot(mh, xh) + (_dot(mh, xl) + _dot(ml, xh))
            y = part if y is None else y + part
        half = y.shape[0] // 2
        return y[:half], y[half:]

    if mode == "flat":
        yr, yi = transform([x_ref[...]])
        o_ref[0] = yr
        o_ref[1] = yi
    elif mode == "a":
        for j in range(DFT_SUB):
            yr, yi = transform([x_ref[:, j, :]])
            if twiddle:
                reps = yr.shape[1] // LANE
                c = jnp.concatenate([tc_ref[j]] * reps, axis=1)
                s = jnp.concatenate([ts_ref[j]] * reps, axis=1)
                yr, yi = yr * c + yi * s, yi * c - yr * s
            o_ref[0, j] = yr
            o_ref[1, j] = yi
    else:
        for j in range(DFT_SUB):
            yr, yi = transform([x_ref[0, :, j, :], x_ref[1, :, j, :]])
            o_ref[0, :, j, :] = yr
            o_ref[1, :, j, :] = yi


def _dft_call(mode, x, mats, out_shape, grid, x_spec, o_spec, tw=None, tw_spec=None):
    args, specs = [x], [x_spec]
    for m in mats:
        mh, ml = _hi_lo(m)
        args += [mh, ml]
        specs += [pl.BlockSpec(m.shape, lambda *_: (0, 0))] * 2
    if tw is not None:
        args += list(tw)
        specs += [tw_spec] * 2
    return pl.pallas_call(
        functools.partial(_dft_kernel, n_planes=len(mats), mode=mode, twiddle=tw is not None),
        out_shape=jax.ShapeDtypeStruct(out_shape, F32),
        grid=grid,
        in_specs=specs,
        out_specs=o_spec,
        compiler_params=_cparams(("arbitrary",) * len(grid)),
        name="dft_" + mode,
    )(*args)


def _fmix_kernel(pr_ref, pi_ref, x_ref, gt_ref, ch_ref, cl_ref, wo_ref, bo_ref, o_ref, *, groups):
    pr, pi = pr_ref[...], pi_ref[...]
    gd = pr.shape[1] // groups
    ch, cl = ch_ref[...], cl_ref[...]
    outs = []
    for g in range(groups):
        z = jnp.concatenate([pr[:, g * gd:(g + 1) * gd], pi[:, g * gd:(g + 1) * gd]], axis=1)
        zh, zl = _split2(z)
        outs.append(_dot(zh, ch) + (_dot(zl, ch) + _dot(zh, cl)))
    f = jnp.concatenate(outs, axis=1)
    o = _dot(f.astype(BF16), wo_ref[...]) + bo_ref[...]
    o_ref[...] = x_ref[...] + gt_ref[...] * o


def _fmix(p, x, mods, row_of_tile, wo, bo, *, tm):
    n, d = x.shape
    b, _, l, _ = p.shape
    tpb = l // tm
    gd = d // FNET_GROUPS
    c, s = _dft_tables(gd)
    ch, cl = _hi_lo(np.concatenate([c, s], axis=0) / math.sqrt(gd))
    return pl.pallas_call(
        functools.partial(_fmix_kernel, groups=FNET_GROUPS),
        out_shape=jax.ShapeDtypeStruct((n, d), F32),
        grid=(n // tm,),
        in_specs=[
            pl.BlockSpec((None, None, tm, d), lambda i: (i // tpb, 0, i % tpb, 0)),
            pl.BlockSpec((None, None, tm, d), lambda i: (i // tpb, 1, i % tpb, 0)),
            pl.BlockSpec((tm, d), lambda i: (i, 0)),
            pl.BlockSpec((None, 1, d), lambda i: (row_of_tile(i) * N_MOD + 5, 0, 0)),
            pl.BlockSpec((2 * gd, gd), lambda i: (0, 0)),
            pl.BlockSpec((2 * gd, gd), lambda i: (0, 0)),
            pl.BlockSpec((d, d), lambda i: (0, 0)),
            pl.BlockSpec((1, d), lambda i: (0, 0)),
        ],
        out_specs=pl.BlockSpec((tm, d), lambda i: (i, 0)),
        compiler_params=_cparams(("arbitrary",)),
        name="fourier_out",
    )(p, p, x, mods, ch, cl, wo, bo)


def _fourier_latent(xl, hl, bsz, seq, mods, wo, bo):
    n, d = xl.shape
    la = lb = int(round(math.sqrt(seq)))
    assert la * lb == seq and lb % DFT_SUB == 0 and la % DFT_SUB == 0
    td = min(DFT_COLS, d)
    s8 = DFT_SUB
    ca, sa = _dft_tables(la)
    m_a = np.concatenate([ca, -sa], axis=0) / math.sqrt(la)
    cb, sb = _dft_tables(lb)
    m_br = np.concatenate([cb, -sb], axis=0) / math.sqrt(lb)
    m_bi = np.concatenate([sb, cb], axis=0) / math.sqrt(lb)
    n2 = np.arange(lb, dtype=np.int64)[:, None]
    k1 = np.arange(la, dtype=np.int64)[None, :]
    ang = 2.0 * np.pi * ((n2 * k1) % seq).astype(np.float64) / seq
    twc = jnp.asarray(np.broadcast_to(np.cos(ang)[:, :, None], (lb, la, LANE)), F32)
    tws = jnp.asarray(np.broadcast_to(np.sin(ang)[:, :, None], (lb, la, LANE)), F32)
    grid = (bsz, lb // s8, d // td)
    a = _dft_call(
        "a", hl.reshape(bsz, la, lb, d), [m_a], (bsz, 2, lb, la, d), grid,
        pl.BlockSpec((None, la, s8, td), lambda b, g, c: (b, 0, g, c)),
        pl.BlockSpec((None, 2, s8, la, td), lambda b, g, c: (b, 0, g, 0, c)),
        tw=(twc, tws), tw_spec=pl.BlockSpec((s8, la, LANE), lambda b, g, c: (g, 0, 0)))
    p = _dft_call(
        "b", a, [m_br, m_bi], (bsz, 2, lb, la, d), (bsz, la // s8, d // td),
        pl.BlockSpec((None, 2, lb, s8, td), lambda b, g, c: (b, 0, 0, g, c)),
        pl.BlockSpec((None, 2, lb, s8, td), lambda b, g, c: (b, 0, 0, g, c)))
    p = p.reshape(bsz, 2, seq, d)
    return _fmix(p, xl, mods, lambda i: i // (seq // 256), wo, bo, tm=256)


def _fourier_ctx(xc, hc, bsz, lc, mods, wo, bo):
    n, d = xc.shape
    td = min(DFT_COLS, d)
    c, s = _dft_tables(lc)
    m = np.concatenate([c, -s], axis=0) / math.sqrt(lc)
    p = _dft_call(
        "flat", hc.reshape(bsz, lc, d), [m], (bsz, 2, lc, d), (bsz, d // td),
        pl.BlockSpec((None, lc, td), lambda b, c: (b, 0, c)),
        pl.BlockSpec((None, 2, lc, td), lambda b, c: (b, 0, 0, c)))
    return _fmix(p, xc, mods, lambda i: 2, wo, bo, tm=lc)


def _rwkv_proj_kernel(*refs, grid_mode, tpb):
    it = iter(refs)
    x_ref = next(it)
    if grid_mode:
        xp_ref, xn_ref = next(it), next(it)
    nw_ref, sh_ref, sc_ref, mu_ref = next(it), next(it), next(it), next(it)
    w1_ref, a1_ref, g1_ref = next(it), next(it), next(it)
    wr_ref, wk_ref, wv_ref = next(it), next(it), next(it)
    w2f_ref, w2b_ref, a2f_ref, a2b_ref, g2_ref = next(it), next(it), next(it), next(it), next(it)
    vec_ref, ones_ref = next(it), next(it)
    r_ref, k_ref, v_ref, g_ref, lwf_ref, lwb_ref, alf_ref, alb_ref, bon_ref = (next(it) for _ in range(9))
    hr_ref, hk_ref, hv_ref, tw_ref, ta_ref, tg_ref = (next(it) for _ in range(6))

    i = pl.program_id(0)
    j = pl.program_id(1)

    @pl.when(j == 0)
    def _():
        nw, sh, sc = nw_ref[...], sh_ref[...], sc_ref[...]
        h = _modulate(x_ref[...], nw, sh, sc)
        tm, d = h.shape
        q = d // 4
        row = lax.broadcasted_iota(jnp.int32, (tm, 1), 0)
        if grid_mode:
            tib = i % tpb
            up_ok = (tib != 0).astype(F32)
            dn_ok = (tib != tpb - 1).astype(F32)
            hp = _modulate(xp_ref[...], nw, sh, sc)[:, 2 * q:3 * q] * up_ok
            hn = _modulate(xn_ref[...], nw, sh, sc)[:, 3 * q:] * dn_ok
            col = row % GRID_W
            left = jnp.where(col != 0, pltpu.roll(h[:, :q], 1, 0), 0.0)
            right = jnp.where(col != GRID_W - 1, pltpu.roll(h[:, q:2 * q], tm - 1, 0), 0.0)
            up = jnp.concatenate([hp, h[:tm - GRID_W, 2 * q:3 * q]], axis=0)
            down = jnp.concatenate([h[GRID_W:, 3 * q:], hn], axis=0)
            hs = jnp.concatenate([left, right, up, down], axis=1)
        else:
            prev = jnp.where(row != 0, pltpu.roll(h, 1, 0), 0.0)
            nxt = jnp.where(row != tm - 1, pltpu.roll(h, tm - 1, 0), 0.0)
            hs = jnp.concatenate([prev[:, :q], nxt[:, q:2 * q], prev[:, 2 * q:3 * q], nxt[:, 3 * q:]], axis=1)
        dlt = hs - h
        mu = mu_ref[...]
        hr_ref[...] = (h + dlt * mu[0:1]).astype(BF16)
        hk_ref[...] = (h + dlt * mu[1:2]).astype(BF16)
        hv_ref[...] = (h + dlt * mu[2:3]).astype(BF16)
        tw_ref[...] = jnp.tanh(_dot((h + dlt * mu[3:4]).astype(BF16), w1_ref[...])).astype(BF16)
        ta_ref[...] = _dot((h + dlt * mu[4:5]).astype(BF16), a1_ref[...]).astype(BF16)
        tg_ref[...] = _sigmoid(_dot((h + dlt * mu[5:6]).astype(BF16), g1_ref[...])).astype(BF16)

    vec = vec_ref[...]
    w0f, w0b, a0f, a0b, ka, rk = (vec[n:n + 1] for n in range(6))
    r = _dot(hr_ref[...], wr_ref[...])
    k = _dot(hk_ref[...], wk_ref[...])
    v = _dot(hv_ref[...], wv_ref[...])
    tw = tw_ref[...]
    ta = ta_ref[...]
    lwf = -DECAY_SCALE * _sigmoid(w0f + _dot(tw, w2f_ref[...]))
    lwb = -DECAY_SCALE * _sigmoid(w0b + _dot(tw, w2b_ref[...]))
    alf = _sigmoid(a0f + _dot(ta, a2f_ref[...]))
    alb = _sigmoid(a0b + _dot(ta, a2b_ref[...]))
    kb = k * (1.0 + (0.5 * (alf + alb) - 1.0) * ka)
    r_ref[...] = r
    k_ref[...] = k
    v_ref[...] = v
    g_ref[...] = _dot(tg_ref[...], g2_ref[...])
    lwf_ref[...] = lwf
    lwb_ref[...] = lwb
    alf_ref[...] = alf
    alb_ref[...] = alb
    bon_ref[...] = _dot_x2(r * kb * rk, ones_ref[...]) * v


def _rwkv_proj(x, mods, row_of_tile, nw, prm, *, tm, grid_mode, tpb):
    n, d = x.shape
    tn = 256
    nj = d // tn
    hw = GRID_W
    lr = prm["w1c"].shape[1]

    def mod_spec(k):
        return pl.BlockSpec((None, 1, d), lambda i, j: (row_of_tile(i) * N_MOD + 3 + k, 0, 0))

    args, specs = [x], [pl.BlockSpec((tm, d), lambda i, j: (i, 0))]
    if grid_mode:
        nb = n // hw
        r = tm // hw
        args += [x, x]
        specs += [pl.BlockSpec((hw, d), lambda i, j: (jnp.maximum(i * r - 1, 0), 0)),
                  pl.BlockSpec((hw, d), lambda i, j: (jnp.minimum((i + 1) * r, nb - 1), 0))]
    args += [nw, mods, mods, prm["mu"], prm["w1c"], prm["a1c"], prm["g1"], prm["wr"], prm["wk"], prm["wv"],
             prm["w2f"], prm["w2b"], prm["a2f"], prm["a2b"], prm["g2"], prm["vec"], prm["ones_head"]]
    full = lambda shape: pl.BlockSpec(shape, lambda i, j: (0,) * len(shape))
    coltile = lambda rows: pl.BlockSpec((rows, tn), lambda i, j: (0, j))
    specs += [full((1, d)), mod_spec(0), mod_spec(1), full((8, d)), full((d, lr)), full((d, lr)), full((d, lr)),
              coltile(d), coltile(d), coltile(d), coltile(lr), coltile(lr), coltile(lr), coltile(lr), coltile(lr),
              coltile(8), full((tn, tn))]
    out = jax.ShapeDtypeStruct((n, d), F32)
    return pl.pallas_call(
        functools.partial(_rwkv_proj_kernel, grid_mode=grid_mode, tpb=tpb),
        out_shape=[out] * 9,
        grid=(n // tm, nj),
        in_specs=specs,
        out_specs=[pl.BlockSpec((tm, tn), lambda i, j: (i, j))] * 9,
        scratch_shapes=[pltpu.VMEM((tm, d), BF16)] * 3 + [pltpu.VMEM((tm, lr), BF16)] * 3,
        compiler_params=_cparams(("arbitrary", "arbitrary")),
        name="rwkv_proj",
    )(*args)


def _wkv_kernel(r_ref, k_ref, v_ref, lw_ref, al_ref, kk_ref, ka_ref, s0_ref, bm_ref, ms_ref, mi_ref, tri_ref,
                ones_ref, eye_ref, y_ref, sf_ref, s_scr, *, rev, chunk, heads, nq, need_y):
    cc = pl.program_id(1)
    c = chunk
    wd = heads * RWKV_HEAD
    rn = heads * c

    @pl.when(cc == 0)
    def _():
        s_scr[...] = s0_ref[...]

    bm = bm_ref[...]
    strict = ms_ref[...] > 0.0
    incl = mi_ref[...] > 0.0
    tri = tri_ref[...]
    ones = ones_ref[...]
    eye = eye_ref[...]

    def stack(x):
        return jnp.concatenate([x] * heads, axis=0) * bm

    if not need_y:
        y_ref[...] = jnp.zeros_like(y_ref)

    for q in range(nq):
        sl = slice(q * wd, (q + 1) * wd)
        r, k, v, lw, al = r_ref[:, sl], k_ref[:, sl], v_ref[:, sl], lw_ref[:, sl], al_ref[:, sl]
        kk0 = k * kk_ref[:, sl]
        kk = kk0 * lax.rsqrt(jnp.maximum(_dot_x2(kk0 * kk0, ones), KK_EPS))
        b = kk * al
        kd = k * (1.0 + (al - 1.0) * ka_ref[:, sl])
        l3 = _split3(lw)
        lg = _dot(tri, l3[0]) + (_dot(tri, l3[1]) + _dot(tri, l3[2]))
        gc = lg[0:1] if rev else lg[c - 1:c]
        gin = jnp.exp(-lg)
        gout = jnp.exp(gc - lg)
        a_s = stack(-kk * jnp.exp(lg - lw)).astype(BF16)
        r_s = stack(r * jnp.exp(lg))
        b_s = stack(b * gin).astype(BF16)
        k_s = stack(kd * gin).astype(BF16)
        v_s = stack(v)
        v_sb = v_s.astype(BF16)
        bo_s = stack(b * gout).astype(BF16)
        ko_s = stack(kd * gout).astype(BF16)

        bk = jnp.concatenate([b_s, k_s], axis=0)
        ga = _dot_nt(a_s, bk)
        aab = jnp.where(strict, ga[:, :rn], 0.0)
        aak = jnp.where(strict, ga[:, rn:], 0.0)
        w0 = _dot(aak.astype(BF16), v_sb)

        xb = aab.astype(BF16)
        pm = eye + aab
        x = _dot(xb, xb)
        steps = int(math.log2(c))
        for _ in range(steps - 2):
            xb = x.astype(BF16)
            res = _dot(xb, jnp.concatenate([xb, pm.astype(BF16)], axis=1))
            x = res[:, :rn]
            pm = pm + res[:, rn:]
        t = pm + _dot(x.astype(BF16), pm.astype(BF16))

        av = _dot(t.astype(BF16), jnp.concatenate([a_s, w0.astype(BF16)], axis=1))
        s = s_scr[q]
        sb = s.astype(BF16)
        if need_y:
            gr = _dot_nt(r_s.astype(BF16), bk)
            arb = jnp.where(incl, gr[:, :rn], 0.0).astype(BF16)
            ark = jnp.where(incl, gr[:, rn:], 0.0).astype(BF16)
            ry = _dot(arb, av.astype(BF16))
            rh = r_s + ry[:, :wd]
            ys = _dot_nt(rh.astype(BF16), sb) + ry[:, wd:] + _dot(ark, v_sb)
            y = ys[0:c]
            for h in range(1, heads):
                y = y + ys[h * c:(h + 1) * c]
            y_ref[:, sl] = y
        avt = av.T.astype(BF16)
        pd = _dot(avt, bo_s)
        dt = pd[wd:] + _dot(v_s.T.astype(BF16), ko_s)
        s_scr[q] = s * jnp.exp(gc) + _dot(sb, pd[:wd].astype(BF16)) + dt

    @pl.when(cc == pl.num_programs(1) - 1)
    def _():
        sf_ref[...] = s_scr[...]


def _wkv_consts(rev):
    c, g = WKV_CHUNK, WKV_HEADS_PER_BLOCK
    wd, rn = g * RWKV_HEAD, g * c
    hrow = np.arange(rn) // c
    trow = np.arange(rn) % c
    hlane = np.arange(wd) // RWKV_HEAD
    bm = (hrow[:, None] == hlane[None, :]).astype(np.float32)
    same = hrow[:, None] == hrow[None, :]
    if rev:
        strict = same & (trow[None, :] > trow[:, None])
        incl = same & (trow[None, :] >= trow[:, None])
        tri = np.triu(np.ones((c, c), np.float32))
    else:
        strict = same & (trow[None, :] < trow[:, None])
        incl = same & (trow[None, :] <= trow[:, None])
        tri = np.tril(np.ones((c, c), np.float32))
    ones = (hlane[:, None] == hlane[None, :]).astype(np.float32)
    return (jnp.asarray(bm), jnp.asarray(strict, F32), jnp.asarray(incl, F32), jnp.asarray(tri, BF16),
            jnp.asarray(ones, BF16), jnp.asarray(np.eye(rn, dtype=np.float32)))


def _wkv(r, k, v, lw, al, kkp, kap, s0, bsz, *, rev, need_y):
    n, d = r.shape
    c, g = WKV_CHUNK, WKV_HEADS_PER_BLOCK
    wd, rn = g * RWKV_HEAD, g * c
    nq = d // wd
    nc = n // bsz // c
    if rev:
        tok = lambda b, cc: (b * nc + (nc - 1 - cc), 0)
    else:
        tok = lambda b, cc: (b * nc + cc, 0)
    tspec = pl.BlockSpec((c, d), tok)
    full = lambda shape: pl.BlockSpec(shape, lambda b, cc: (0,) * len(shape))
    sspec = pl.BlockSpec((None, nq, wd, wd), lambda b, cc: (b, 0, 0, 0))
    consts = _wkv_consts(rev)
    y, sf = pl.pallas_call(
        functools.partial(_wkv_kernel, rev=rev, chunk=c, heads=g, nq=nq, need_y=need_y),
        out_shape=[jax.ShapeDtypeStruct((n, d), F32), jax.ShapeDtypeStruct((bsz, nq, wd, wd), F32)],
        grid=(bsz, nc),
        in_specs=[tspec] * 5 + [full((1, d)), full((1, d)), sspec,
                                full((rn, wd)), full((rn, rn)), full((rn, rn)), full((c, c)), full((wd, wd)),
                                full((rn, rn))],
        out_specs=[tspec, sspec],
        scratch_shapes=[pltpu.VMEM((nq, wd, wd), F32)],
        compiler_params=_cparams(("arbitrary", "arbitrary")),
        name="wkv_scan",
    )(r, k, v, lw, al, kkp, kap, s0, *consts)
    return y, sf


def _rwkv_out_kernel(yf_ref, yb_ref, bon_ref, g_ref, x_ref, gt_ref, lnw_ref, lnb_ref, ones_ref, wo_ref, o_ref,
                     *, wd):
    y = yf_ref[...] + yb_ref[...]
    ones = ones_ref[...]
    d = y.shape[1]
    inv = 1.0 / RWKV_HEAD

    def headsum(z):
        return jnp.concatenate([_dot_x2(z[:, q * wd:(q + 1) * wd], ones) for q in range(d // wd)], axis=1)

    dev = y - headsum(y) * inv
    var = headsum(dev * dev) * inv
    yn = dev * lax.rsqrt(var + GN_EPS) * lnw_ref[...] + lnb_ref[...]
    o = (yn + bon_ref[...]) * g_ref[...]
    o_ref[...] = x_ref[...] + gt_ref[...] * _dot(o.astype(BF16), wo_ref[...])


def _rwkv_out(yf, yb, bon, g, x, mods, row_of_tile, lnw, lnb, wo, *, tm):
    n, d = x.shape
    wd = WKV_HEADS_PER_BLOCK * RWKV_HEAD
    hl = np.arange(wd) // RWKV_HEAD
    ones = jnp.asarray(hl[:, None] == hl[None, :], BF16)
    tspec = pl.BlockSpec((tm, d), lambda i: (i, 0))
    full = lambda shape: pl.BlockSpec(shape, lambda i: (0,) * len(shape))
    return pl.pallas_call(
        functools.partial(_rwkv_out_kernel, wd=wd),
        out_shape=jax.ShapeDtypeStruct((n, d), F32),
        grid=(n // tm,),
        in_specs=[tspec] * 5 + [pl.BlockSpec((None, 1, d), lambda i: (row_of_tile(i) * N_MOD + 5, 0, 0)),
                                full((1, d)), full((1, d)), full((wd, wd)), full((d, d))],
        out_specs=tspec,
        compiler_params=_cparams(("arbitrary",)),
        name="rwkv_out",
    )(yf, yb, bon, g, x, mods, lnw, lnb, ones, wo)


def _rwkv_params(j, d, rwkv_mu, rwkv_w_rkv, rwkv_w0, rwkv_w1, rwkv_w2, rwkv_a0, rwkv_a1, rwkv_a2, rwkv_g1,
                 rwkv_g2, rwkv_k_a, rwkv_r_k):
    lr = rwkv_g1.shape[2]
    rank = rwkv_w1.shape[3]
    assert 2 * rank <= lr

    def first(w):
        return jnp.pad(jnp.concatenate([w[0], w[1]], axis=1), ((0, 0), (0, lr - 2 * rank))).astype(BF16)

    def second(w, z):
        return jnp.pad(w, ((z * rank, lr - (z + 1) * rank), (0, 0))).astype(BF16)

    vec = jnp.stack([rwkv_w0[j, 0], rwkv_w0[j, 1], rwkv_a0[j, 0], rwkv_a0[j, 1], rwkv_k_a[j],
                     rwkv_r_k[j].reshape(d), jnp.zeros((d,), F32), jnp.zeros((d,), F32)])
    hl = np.arange(256) // RWKV_HEAD
    return dict(
        mu=jnp.pad(rwkv_mu[j], ((0, 2), (0, 0))),
        w1c=first(rwkv_w1[j]), a1c=first(rwkv_a1[j]), g1=rwkv_g1[j].astype(BF16),
        wr=rwkv_w_rkv[j, 0].astype(BF16), wk=rwkv_w_rkv[j, 1].astype(BF16), wv=rwkv_w_rkv[j, 2].astype(BF16),
        w2f=second(rwkv_w2[j, 0], 0), w2b=second(rwkv_w2[j, 1], 1),
        a2f=second(rwkv_a2[j, 0], 0), a2b=second(rwkv_a2[j, 1], 1),
        g2=rwkv_g2[j].astype(BF16), vec=vec,
        ones_head=jnp.asarray(hl[:, None] == hl[None, :], BF16),
    )


def kernel(x, c, ctx, c_ctx, mod_w, mod_b, norm_w, ffn_w13, ffn_w2, fnet_w_o, fnet_b_o, rwkv_mu, rwkv_w_rkv,
           rwkv_w0, rwkv_w1, rwkv_w2, rwkv_a0, rwkv_a1, rwkv_a2, rwkv_g1, rwkv_g2, rwkv_k_k, rwkv_k_a, rwkv_r_k,
           rwkv_ln_w, rwkv_ln_b, rwkv_w_o, final_norm_w):
    bsz, seq, d = x.shape
    lc = ctx.shape[1]
    depth = mod_w.shape[0]
    assert depth == 2 and bsz == 2, "layer schedule below is written for the two-layer, batch-2 trunk"
    tm = 512
    assert seq % tm == 0 and seq % GRID_W == 0

    c8 = jnp.concatenate([c, c_ctx[None], jnp.zeros((8 - bsz - 1, d), F32)], axis=0)
    mods_all = _adaln(c8, mod_w, mod_b)
    w13 = ffn_w13.astype(BF16)
    w2 = ffn_w2.astype(BF16)
    fw = final_norm_w.reshape(1, d)

    xl = x.reshape(bsz * seq, d)
    xc = ctx.reshape(bsz * lc, d)
    lat_row = lambda i: i // (seq // tm)
    ctx_row = lambda i: 2

    mods = mods_all[0, :3].reshape(3 * N_MOD, 1, d)
    nw = norm_w[0].reshape(3, 1, d)
    xl, hl = _ffn(xl, mods, lat_row, 0, nw[0], w13[0, 0], w2[0, 0], fw, tm=tm, next_nw=nw[1])
    xc, hc = _ffn(xc, mods, ctx_row, 0, nw[0], w13[0, 0], w2[0, 0], fw, tm=lc, next_nw=nw[1])
    wo = fnet_w_o[0].astype(BF16)
    bo = fnet_b_o[0].reshape(1, d)
    xl = _fourier_latent(xl, hl, bsz, seq, mods, wo, bo)
    xc = _fourier_ctx(xc, hc, bsz, lc, mods, wo, bo)
    xl = _ffn(xl, mods, lat_row, 2, nw[2], w13[0, 1], w2[0, 1], fw, tm=tm)
    xc = _ffn(xc, mods, ctx_row, 2, nw[2], w13[0, 1], w2[0, 1], fw, tm=lc)

    mods = mods_all[1, :3].reshape(3 * N_MOD, 1, d)
    nw = norm_w[1].reshape(3, 1, d)
    xl = _ffn(xl, mods, lat_row, 0, nw[0], w13[1, 0], w2[1, 0], fw, tm=tm)
    xc = _ffn(xc, mods, ctx_row, 0, nw[0], w13[1, 0], w2[1, 0], fw, tm=lc)
    prm = _rwkv_params(0, d, rwkv_mu, rwkv_w_rkv, rwkv_w0, rwkv_w1, rwkv_w2, rwkv_a0, rwkv_a1, rwkv_a2, rwkv_g1,
                       rwkv_g2, rwkv_k_a, rwkv_r_k)
    r_c, k_c, v_c, _, lwf_c, lwb_c, alf_c, alb_c, _ = _rwkv_proj(
        xc, mods, ctx_row, nw[1], prm, tm=lc, grid_mode=False, tpb=1)
    r_l, k_l, v_l, g_l, lwf_l, lwb_l, alf_l, alb_l, bon_l = _rwkv_proj(
        xl, mods, lat_row, nw[1], prm, tm=tm, grid_mode=True, tpb=seq // tm)
    kkp = rwkv_k_k[0].reshape(1, d)
    kap = rwkv_k_a[0].reshape(1, d)
    wd = WKV_HEADS_PER_BLOCK * RWKV_HEAD
    s0 = jnp.zeros((bsz, d // wd, wd, wd), F32)
    _, s_f = _wkv(r_c, k_c, v_c, lwf_c, alf_c, kkp, kap, s0, bsz, rev=False, need_y=False)
    _, s_b = _wkv(r_c, k_c, v_c, lwb_c, alb_c, kkp, kap, s0, bsz, rev=True, need_y=False)
    yf, _ = _wkv(r_l, k_l, v_l, lwf_l, alf_l, kkp, kap, s_f, bsz, rev=False, need_y=True)
    yb, _ = _wkv(r_l, k_l, v_l, lwb_l, alb_l, kkp, kap, s_b, bsz, rev=True, need_y=True)
    xl = _rwkv_out(yf, yb, bon_l, g_l, xl, mods, lambda i: i // (seq // 256), rwkv_ln_w[0].reshape(1, d),
                   rwkv_ln_b[0].reshape(1, d), rwkv_w_o[0].astype(BF16), tm=256)
    xl = _ffn(xl, mods, lat_row, 2, nw[2], w13[1, 1], w2[1, 1], fw, tm=tm, final_norm=True)
    return xl.reshape(bsz, seq, d)
```

```python
import functools
import math

import numpy as np
import jax
import jax.numpy as jnp
from jax import lax
from jax.experimental import pallas as pl
from jax.experimental.pallas import tpu as pltpu

F32 = jnp.float32
BF16 = jnp.bfloat16

NORM_EPS = 1e-6
GN_EPS = 64e-5
KK_EPS = 1e-24
GRID_W = 64
FNET_GROUPS = 8
RWKV_HEAD = 64
N_MOD = 9
DECAY_SCALE = math.exp(-0.5)

LANE = 128
WKV_CHUNK = 64
WKV_HEADS_PER_BLOCK = 4
WKV_GROUP = 4
VMEM_LIMIT = 56 * 1024 * 1024


def _cparams(sem):
    return pltpu.CompilerParams(dimension_semantics=sem, vmem_limit_bytes=VMEM_LIMIT)


def _sigmoid(x):
    return 1.0 / (1.0 + jnp.exp(-x))


def _modulate(x, nw, shift, scale):
    y = x * lax.rsqrt(jnp.mean(x * x, axis=-1, keepdims=True) + NORM_EPS)
    return (y * nw) * (1.0 + scale) + shift


def _dot(a, b):
    return jnp.dot(a, b, preferred_element_type=F32)


def _dot_nt(a, b):
    return lax.dot_general(a, b, (((1,), (1,)), ((), ())), preferred_element_type=F32)


def _split2(x):
    hi = x.astype(BF16)
    lo = (x - hi.astype(F32)).astype(BF16)
    return hi, lo


def _split3(x):
    hi = x.astype(BF16)
    r1 = x - hi.astype(F32)
    mid = r1.astype(BF16)
    lo = (r1 - mid.astype(F32)).astype(BF16)
    return hi, mid, lo


def _dot_x2(x, w_bf16):
    hi, lo = _split2(x)
    return _dot(hi, w_bf16) + _dot(lo, w_bf16)


def _adaln_kernel(c_ref, w_ref, b_ref, o_ref):
    c = c_ref[...]
    s = c * _sigmoid(c)
    o_ref[...] = _dot(s.astype(BF16), w_ref[...].astype(BF16)) + b_ref[...]


def _adaln(c8, mod_w, mod_b):
    depth, d, nd = mod_w.shape
    tn = 1024 if nd % 1024 == 0 else 512
    return pl.pallas_call(
        _adaln_kernel,
        out_shape=jax.ShapeDtypeStruct((depth, 8, nd), F32),
        grid=(depth, nd // tn),
        in_specs=[
            pl.BlockSpec((8, d), lambda l, j: (0, 0)),
            pl.BlockSpec((None, d, tn), lambda l, j: (l, 0, j)),
            pl.BlockSpec((None, 1, tn), lambda l, j: (l, 0, j)),
        ],
        out_specs=pl.BlockSpec((None, 8, tn), lambda l, j: (l, 0, j)),
        compiler_params=_cparams(("arbitrary", "arbitrary")),
        name="adaln_linear",
    )(c8, mod_w, mod_b.reshape(depth, 1, nd))


def _ffn_kernel(*refs, final_norm, emit_h):
    x_ref, sh_ref, sc_ref, gt_ref, nw_ref, w1g_ref, w1u_ref, w2_ref, fw_ref = refs[:9]
    if emit_h:
        nw2_ref, sh2_ref, sc2_ref, o_ref, h2_ref, h_ref, acc_ref = refs[9:]
    else:
        o_ref, h_ref, acc_ref = refs[9:]
    j = pl.program_id(1)

    @pl.when(j == 0)
    def _():
        h_ref[...] = _modulate(x_ref[...], nw_ref[...], sh_ref[...], sc_ref[...]).astype(BF16)
        acc_ref[...] = jnp.zeros_like(acc_ref)

    h = h_ref[...]
    g = _dot(h, w1g_ref[...])
    u = _dot(h, w1u_ref[...])
    a = (g * _sigmoid(g)) * u
    acc_ref[...] += _dot(a.astype(BF16), w2_ref[...])

    @pl.when(j == pl.num_programs(1) - 1)
    def _():
        y = x_ref[...] + (0.5 * gt_ref[...]) * acc_ref[...]
        if final_norm:
            y = y * lax.rsqrt(jnp.mean(y * y, axis=-1, keepdims=True) + NORM_EPS) * fw_ref[...]
        o_ref[...] = y
        if emit_h:
            h2_ref[...] = _modulate(y, nw2_ref[...], sh2_ref[...], sc2_ref[...])


def _ffn(x, mods, row_of_tile, sub, nw, w13, w2, fw, *, tm, final_norm=False, next_nw=None):
    n, d = x.shape
    f = w2.shape[0]
    tf = 512
    nf = f // tf
    emit_h = next_nw is not None

    def mod_spec(k):
        return pl.BlockSpec((None, 1, d), lambda i, j: (row_of_tile(i) * N_MOD + 3 * sub + k, 0, 0))

    vec_spec = pl.BlockSpec((1, d), lambda i, j: (0, 0))
    tile_spec = pl.BlockSpec((tm, d), lambda i, j: (i, 0))
    out = jax.ShapeDtypeStruct((n, d), F32)
    args = [x, mods, mods, mods, nw, w13, w13, w2, fw]
    specs = [tile_spec, mod_spec(0), mod_spec(1), mod_spec(2), vec_spec,
             pl.BlockSpec((d, tf), lambda i, j: (0, j)),
             pl.BlockSpec((d, tf), lambda i, j: (0, nf + j)),
             pl.BlockSpec((tf, d), lambda i, j: (j, 0)),
             vec_spec]
    if emit_h:
        args += [next_nw, mods, mods]
        specs += [vec_spec, mod_spec(3), mod_spec(4)]
    return pl.pallas_call(
        functools.partial(_ffn_kernel, final_norm=final_norm, emit_h=emit_h),
        out_shape=[out, out] if emit_h else out,
        grid=(n // tm, nf),
        in_specs=specs,
        out_specs=[tile_spec, tile_spec] if emit_h else tile_spec,
        scratch_shapes=[pltpu.VMEM((tm, d), BF16), pltpu.VMEM((tm, d), F32)],
        compiler_params=_cparams(("arbitrary", "arbitrary")),
        name="ffn_swiglu",
    )(*args)


def _dft_tables(n):
    k = np.arange(n, dtype=np.int64)
    ang = 2.0 * np.pi * ((k[:, None] * k[None, :]) % n).astype(np.float64) / n
    return np.cos(ang), np.sin(ang)


def _hi_lo(m):
    m = jnp.asarray(m, F32)
    hi = m.astype(BF16)
    return hi, (m - hi.astype(F32)).astype(BF16)


DFT_SUB = 8
DFT_COLS = 512


def _dft_kernel(*refs, n_planes, mode, twiddle):
    it = iter(refs)
    x_ref = next(it)
    mats = [(next(it)[...], next(it)[...]) for _ in range(n_planes)]
    if twiddle:
        tc_ref, ts_ref = next(it), next(it)
    o_ref = next(it)

    def transform(planes):
        y = None
        for x, (mh, ml) in zip(planes, mats):
            xh, xl = _split2(x)
            part = _dot(mh, xh) + (_dot(mh, xl) + _dot(ml, xh))
            y = part if y is None else y + part
        half = y.shape[0] // 2
        return y[:half], y[half:]

    if mode == "flat":
        yr, yi = transform([x_ref[...]])
        o_ref[0] = yr
        o_ref[1] = yi
    elif mode == "a":
        for j in range(DFT_SUB):
            yr, yi = transform([x_ref[:, j, :]])
            if twiddle:
                reps = yr.shape[1] // LANE
                c = jnp.concatenate([tc_ref[j]] * reps, axis=1)
                s = jnp.concatenate([ts_ref[j]] * reps, axis=1)
                yr, yi = yr * c + yi * s, yi * c - yr * s
            o_ref[0, j] = yr
            o_ref[1, j] = yi
    else:
        for j in range(DFT_SUB):
            yr, yi = transform([x_ref[0, :, j, :], x_ref[1, :, j, :]])
            o_ref[0, :, j, :] = yr
            o_ref[1, :, j, :] = yi


def _dft_call(mode, x, mats, out_shape, grid, x_spec, o_spec, tw=None, tw_spec=None):
    args, specs = [x], [x_spec]
    for m in mats:
        mh, ml = _hi_lo(m)
        args += [mh, ml]
        specs += [pl.BlockSpec(m.shape, lambda *_: (0, 0))] * 2
    if tw is not None:
        args += list(tw)
        specs += [tw_spec] * 2
    return pl.pallas_call(
        functools.partial(_dft_kernel, n_planes=len(mats), mode=mode, twiddle=tw is not None),
        out_shape=jax.ShapeDtypeStruct(out_shape, F32),
        grid=grid,
        in_specs=specs,
        out_specs=o_spec,
        compiler_params=_cparams(("arbitrary",) * len(grid)),
        name="dft_" + mode,
    )(*args)


def _fmix_kernel(pr_ref, pi_ref, x_ref, gt_ref, ch_ref, cl_ref, wo_ref, bo_ref, o_ref, *, groups):
    pr, pi = pr_ref[...], pi_ref[...]
    gd = pr.shape[1] // groups
    ch, cl = ch_ref[...], cl_ref[...]
    outs = []
    for g in range(groups):
        z = jnp.concatenate([pr[:, g * gd:(g + 1) * gd], pi[:, g * gd:(g + 1) * gd]], axis=1)
        zh, zl = _split2(z)
        outs.append(_dot(zh, ch) + (_dot(zl, ch) + _dot(zh, cl)))
    f = jnp.concatenate(outs, axis=1)
    o = _dot(f.astype(BF16), wo_ref[...]) + bo_ref[...]
    o_ref[...] = x_ref[...] + gt_ref[...] * o


def _fmix(p, x, mods, row_of_tile, wo, bo, *, tm):
    n, d = x.shape
    b, _, l, _ = p.shape
    tpb = l // tm
    gd = d // FNET_GROUPS
    c, s = _dft_tables(gd)
    ch, cl = _hi_lo(np.concatenate([c, s], axis=0) / math.sqrt(gd))
    return pl.pallas_call(
        functools.partial(_fmix_kernel, groups=FNET_GROUPS),
        out_shape=jax.ShapeDtypeStruct((n, d), F32),
        grid=(n // tm,),
        in_specs=[
            pl.BlockSpec((None, None, tm, d), lambda i: (i // tpb, 0, i % tpb, 0)),
            pl.BlockSpec((None, None, tm, d), lambda i: (i // tpb, 1, i % tpb, 0)),
            pl.BlockSpec((tm, d), lambda i: (i, 0)),
            pl.BlockSpec((None, 1, d), lambda i: (row_of_tile(i) * N_MOD + 5, 0, 0)),
            pl.BlockSpec((2 * gd, gd), lambda i: (0, 0)),
            pl.BlockSpec((2 * gd, gd), lambda i: (0, 0)),
            pl.BlockSpec((d, d), lambda i: (0, 0)),
            pl.BlockSpec((1, d), lambda i: (0, 0)),
        ],
        out_specs=pl.BlockSpec((tm, d), lambda i: (i, 0)),
        compiler_params=_cparams(("arbitrary",)),
        name="fourier_out",
    )(p, p, x, mods, ch, cl, wo, bo)


def _fourier_latent(xl, hl, bsz, seq, mods, wo, bo):
    n, d = xl.shape
    la = lb = int(round(math.sqrt(seq)))
    assert la * lb == seq and lb % DFT_SUB == 0 and la % DFT_SUB == 0
    td = min(DFT_COLS, d)
    s8 = DFT_SUB
    ca, sa = _dft_tables(la)
    m_a = np.concatenate([ca, -sa], axis=0) / math.sqrt(la)
    cb, sb = _dft_tables(lb)
    m_br = np.concatenate([cb, -sb], axis=0) / math.sqrt(lb)
    m_bi = np.concatenate([sb, cb], axis=0) / math.sqrt(lb)
    n2 = np.arange(lb, dtype=np.int64)[:, None]
    k1 = np.arange(la, dtype=np.int64)[None, :]
    ang = 2.0 * np.pi * ((n2 * k1) % seq).astype(np.float64) / seq
    twc = jnp.asarray(np.broadcast_to(np.cos(ang)[:, :, None], (lb, la, LANE)), F32)
    tws = jnp.asarray(np.broadcast_to(np.sin(ang)[:, :, None], (lb, la, LANE)), F32)
    grid = (bsz, lb // s8, d // td)
    a = _dft_call(
        "a", hl.reshape(bsz, la, lb, d), [m_a], (bsz, 2, lb, la, d), grid,
        pl.BlockSpec((None, la, s8, td), lambda b, g, c: (b, 0, g, c)),
        pl.BlockSpec((None, 2, s8, la, td), lambda b, g, c: (b, 0, g, 0, c)),
        tw=(twc, tws), tw_spec=pl.BlockSpec((s8, la, LANE), lambda b, g, c: (g, 0, 0)))
    p = _dft_call(
        "b", a, [m_br, m_bi], (bsz, 2, lb, la, d), (bsz, la // s8, d // td),
        pl.BlockSpec((None, 2, lb, s8, td), lambda b, g, c: (b, 0, 0, g, c)),
        pl.BlockSpec((None, 2, lb, s8, td), lambda b, g, c: (b, 0, 0, g, c)))
    p = p.reshape(bsz, 2, seq, d)
    return _fmix(p, xl, mods, lambda i: i // (seq // 256), wo, bo, tm=256)


def _fourier_ctx(xc, hc, bsz, lc, mods, wo, bo):
    n, d = xc.shape
    td = min(DFT_COLS, d)
    c, s = _dft_tables(lc)
    m = np.concatenate([c, -s], axis=0) / math.sqrt(lc)
    p = _dft_call(
        "flat", hc.reshape(bsz, lc, d), [m], (bsz, 2, lc, d), (bsz, d // td),
        pl.BlockSpec((None, lc, td), lambda b, c: (b, 0, c)),
        pl.BlockSpec((None, 2, lc, td), lambda b, c: (b, 0, 0, c)))
    return _fmix(p, xc, mods, lambda i: 2, wo, bo, tm=lc)


def _rwkv_proj_kernel(*refs, grid_mode, tpb):
    it = iter(refs)
    x_ref = next(it)
    if grid_mode:
        xp_ref, xn_ref = next(it), next(it)
    nw_ref, sh_ref, sc_ref, mu_ref = next(it), next(it), next(it), next(it)
    w1_ref, a1_ref, g1_ref = next(it), next(it), next(it)
    wr_ref, wk_ref, wv_ref = next(it), next(it), next(it)
    w2f_ref, w2b_ref, a2f_ref, a2b_ref, g2_ref = next(it), next(it), next(it), next(it), next(it)
    vec_ref, ones_ref = next(it), next(it)
    r_ref, k_ref, v_ref, g_ref, lwf_ref, lwb_ref, alf_ref, alb_ref, bon_ref = (next(it) for _ in range(9))
    hr_ref, hk_ref, hv_ref, tw_ref, ta_ref, tg_ref = (next(it) for _ in range(6))

    i = pl.program_id(0)
    j = pl.program_id(1)

    @pl.when(j == 0)
    def _():
        nw, sh, sc = nw_ref[...], sh_ref[...], sc_ref[...]
        h = _modulate(x_ref[...], nw, sh, sc)
        tm, d = h.shape
        q = d // 4
        row = lax.broadcasted_iota(jnp.int32, (tm, 1), 0)
        if grid_mode:
            tib = i % tpb
            up_ok = (tib != 0).astype(F32)
            dn_ok = (tib != tpb - 1).astype(F32)
            hp = _modulate(xp_ref[...], nw, sh, sc)[:, 2 * q:3 * q] * up_ok
            hn = _modulate(xn_ref[...], nw, sh, sc)[:, 3 * q:] * dn_ok
            col = row % GRID_W
            left = jnp.where(col != 0, pltpu.roll(h[:, :q], 1, 0), 0.0)
            right = jnp.where(col != GRID_W - 1, pltpu.roll(h[:, q:2 * q], tm - 1, 0), 0.0)
            up = jnp.concatenate([hp, h[:tm - GRID_W, 2 * q:3 * q]], axis=0)
            down = jnp.concatenate([h[GRID_W:, 3 * q:], hn], axis=0)
            hs = jnp.concatenate([left, right, up, down], axis=1)
        else:
            prev = jnp.where(row != 0, pltpu.roll(h, 1, 0), 0.0)
            nxt = jnp.where(row != tm - 1, pltpu.roll(h, tm - 1, 0), 0.0)
            hs = jnp.concatenate([prev[:, :q], nxt[:, q:2 * q], prev[:, 2 * q:3 * q], nxt[:, 3 * q:]], axis=1)
        dlt = hs - h
        mu = mu_ref[...]
        hr_ref[...] = (h + dlt * mu[0:1]).astype(BF16)
        hk_ref[...] = (h + dlt * mu[1:2]).astype(BF16)
        hv_ref[...] = (h + dlt * mu[2:3]).astype(BF16)
        tw_ref[...] = jnp.tanh(_dot((h + dlt * mu[3:4]).astype(BF16), w1_ref[...])).astype(BF16)
        ta_ref[...] = _dot((h + dlt * mu[4:5]).astype(BF16), a1_ref[...]).astype(BF16)
        tg_ref[...] = _sigmoid(_dot((h + dlt * mu[5:6]).astype(BF16), g1_ref[...])).astype(BF16)

    vec = vec_ref[...]
    w0f, w0b, a0f, a0b, ka, rk = (vec[n:n + 1] for n in range(6))
    r = _dot(hr_ref[...], wr_ref[...])
    k = _dot(hk_ref[...], wk_ref[...])
    v = _dot(hv_ref[...], wv_ref[...])
    tw = tw_ref[...]
    ta = ta_ref[...]
    lwf = -DECAY_SCALE * _sigmoid(w0f + _dot(tw, w2f_ref[...]))
    lwb = -DECAY_SCALE * _sigmoid(w0b + _dot(tw, w2b_ref[...]))
    alf = _sigmoid(a0f + _dot(ta, a2f_ref[...]))
    alb = _sigmoid(a0b + _dot(ta, a2b_ref[...]))
    kb = k * (1.0 + (0.5 * (alf + alb) - 1.0) * ka)
    r_ref[...] = r
    k_ref[...] = k
    v_ref[...] = v
    g_ref[...] = _dot(tg_ref[...], g2_ref[...])
    lwf_ref[...] = lwf
    lwb_ref[...] = lwb
    alf_ref[...] = alf
    alb_ref[...] = alb
    bon_ref[...] = _dot_x2(r * kb * rk, ones_ref[...]) * v


def _rwkv_proj(x, mods, row_of_tile, nw, prm, *, tm, grid_mode, tpb):
    n, d = x.shape
    tn = 256
    nj = d // tn
    hw = GRID_W
    lr = prm["w1c"].shape[1]

    def mod_spec(k):
        return pl.BlockSpec((None, 1, d), lambda i, j: (row_of_tile(i) * N_MOD + 3 + k, 0, 0))

    args, specs = [x], [pl.BlockSpec((tm, d), lambda i, j: (i, 0))]
    if grid_mode:
        nb = n // hw
        r = tm // hw
        args += [x, x]
        specs += [pl.BlockSpec((hw, d), lambda i, j: (jnp.maximum(i * r - 1, 0), 0)),
                  pl.BlockSpec((hw, d), lambda i, j: (jnp.minimum((i + 1) * r, nb - 1), 0))]
    args += [nw, mods, mods, prm["mu"], prm["w1c"], prm["a1c"], prm["g1"], prm["wr"], prm["wk"], prm["wv"],
             prm["w2f"], prm["w2b"], prm["a2f"], prm["a2b"], prm["g2"], prm["vec"], prm["ones_head"]]
    full = lambda shape: pl.BlockSpec(shape, lambda i, j: (0,) * len(shape))
    coltile = lambda rows: pl.BlockSpec((rows, tn), lambda i, j: (0, j))
    specs += [full((1, d)), mod_spec(0), mod_spec(1), full((8, d)), full((d, lr)), full((d, lr)), full((d, lr)),
              coltile(d), coltile(d), coltile(d), coltile(lr), coltile(lr), coltile(lr), coltile(lr), coltile(lr),
              coltile(8), full((tn, tn))]
    out = jax.ShapeDtypeStruct((n, d), F32)
    return pl.pallas_call(
        functools.partial(_rwkv_proj_kernel, grid_mode=grid_mode, tpb=tpb),
        out_shape=[out] * 9,
        grid=(n // tm, nj),
        in_specs=specs,
        out_specs=[pl.BlockSpec((tm, tn), lambda i, j: (i, j))] * 9,
        scratch_shapes=[pltpu.VMEM((tm, d), BF16)] * 3 + [pltpu.VMEM((tm, lr), BF16)] * 3,
        compiler_params=_cparams(("arbitrary", "arbitrary")),
        name="rwkv_proj",
    )(*args)


def _wkv_kernel(r_ref, k_ref, v_ref, lw_ref, al_ref, kk_ref, ka_ref, s0_ref, bm_ref, ms_ref, mi_ref, tri_ref,
                ones_ref, eye_ref, y_ref, sf_ref, s_scr, *, rev, chunk, heads, nq, need_y, group):
    cc = pl.program_id(1)
    c = chunk
    wd = heads * RWKV_HEAD
    rn = heads * c

    @pl.when(cc == 0)
    def _():
        s_scr[...] = s0_ref[...]

    bm = bm_ref[...]
    strict = ms_ref[...] > 0.0
    incl = mi_ref[...] > 0.0
    tri = tri_ref[...]
    ones = ones_ref[...]
    eye = eye_ref[...]

    def stack(x):
        return jnp.concatenate([x] * heads, axis=0) * bm

    if not need_y:
        y_ref[...] = jnp.zeros_like(y_ref)

    def block(q):
        sl = slice(q * wd, (q + 1) * wd)
        r, k, v, lw, al = r_ref[:, sl], k_ref[:, sl], v_ref[:, sl], lw_ref[:, sl], al_ref[:, sl]
        kk0 = k * kk_ref[:, sl]
        kk = kk0 * lax.rsqrt(jnp.maximum(_dot_x2(kk0 * kk0, ones), KK_EPS))
        b = kk * al
        kd = k * (1.0 + (al - 1.0) * ka_ref[:, sl])
        l3 = _split3(lw)
        lg = _dot(tri, l3[0]) + (_dot(tri, l3[1]) + _dot(tri, l3[2]))
        gc = lg[0:1] if rev else lg[c - 1:c]
        gin = jnp.exp(-lg)
        gout = jnp.exp(gc - lg)
        a_s = stack(-kk * jnp.exp(lg - lw)).astype(BF16)
        r_s = stack(r * jnp.exp(lg))
        b_s = stack(b * gin).astype(BF16)
        k_s = stack(kd * gin).astype(BF16)
        v_s = stack(v)
        v_sb = v_s.astype(BF16)
        bo_s = stack(b * gout).astype(BF16)
        ko_s = stack(kd * gout).astype(BF16)
        yield

        bk = jnp.concatenate([b_s, k_s], axis=0)
        ga = _dot_nt(a_s, bk)
        aab = jnp.where(strict, ga[:, :rn], 0.0)
        aak = jnp.where(strict, ga[:, rn:], 0.0)
        if need_y:
            gr = _dot_nt(r_s.astype(BF16), bk)
            arb = jnp.where(incl, gr[:, :rn], 0.0).astype(BF16)
            ark = jnp.where(incl, gr[:, rn:], 0.0).astype(BF16)
        yield
        w0 = _dot(aak.astype(BF16), v_sb)

        xb = aab.astype(BF16)
        pm = eye + aab
        x = _dot(xb, xb)
        yield
        steps = int(math.log2(c))
        for _ in range(steps - 2):
            xb = x.astype(BF16)
            res = _dot(xb, jnp.concatenate([xb, pm.astype(BF16)], axis=1))
            x = res[:, :rn]
            pm = pm + res[:, rn:]
            yield
        t = pm + _dot(x.astype(BF16), pm.astype(BF16))
        yield

        av = _dot(t.astype(BF16), jnp.concatenate([a_s, w0.astype(BF16)], axis=1))
        yield
        s = s_scr[q]
        sb = s.astype(BF16)
        if need_y:
            ry = _dot(arb, av.astype(BF16))
            rh = r_s + ry[:, :wd]
            yk = _dot(ark, v_sb)
        avt = av.T.astype(BF16)
        pd = _dot(avt, bo_s)
        dt = pd[wd:] + _dot(v_s.T.astype(BF16), ko_s)
        yield
        if need_y:
            ys = _dot_nt(rh.astype(BF16), sb) + ry[:, wd:] + yk
            y = ys[0:c]
            for h in range(1, heads):
                y = y + ys[h * c:(h + 1) * c]
            y_ref[:, sl] = y
        s_scr[q] = s * jnp.exp(gc) + _dot(sb, pd[:wd].astype(BF16)) + dt

    for q0 in range(0, nq, group):
        live = [block(q) for q in range(q0, min(q0 + group, nq))]
        while live:
            nxt = []
            for gen in live:
                try:
                    next(gen)
                    nxt.append(gen)
                except StopIteration:
                    pass
            live = nxt

    @pl.when(cc == pl.num_programs(1) - 1)
    def _():
        sf_ref[...] = s_scr[...]


def _wkv_consts(rev):
    c, g = WKV_CHUNK, WKV_HEADS_PER_BLOCK
    wd, rn = g * RWKV_HEAD, g * c
    hrow = np.arange(rn) // c
    trow = np.arange(rn) % c
    hlane = np.arange(wd) // RWKV_HEAD
    bm = (hrow[:, None] == hlane[None, :]).astype(np.float32)
    same = hrow[:, None] == hrow[None, :]
    if rev:
        strict = same & (trow[None, :] > trow[:, None])
        incl = same & (trow[None, :] >= trow[:, None])
        tri = np.triu(np.ones((c, c), np.float32))
    else:
        strict = same & (trow[None, :] < trow[:, None])
        incl = same & (trow[None, :] <= trow[:, None])
        tri = np.tril(np.ones((c, c), np.float32))
    ones = (hlane[:, None] == hlane[None, :]).astype(np.float32)
    return (jnp.asarray(bm), jnp.asarray(strict, F32), jnp.asarray(incl, F32), jnp.asarray(tri, BF16),
            jnp.asarray(ones, BF16), jnp.asarray(np.eye(rn, dtype=np.float32)))


def _wkv(r, k, v, lw, al, kkp, kap, s0, bsz, *, rev, need_y):
    n, d = r.shape
    c, g = WKV_CHUNK, WKV_HEADS_PER_BLOCK
    wd, rn = g * RWKV_HEAD, g * c
    nq = d // wd
    nc = n // bsz // c
    if rev:
        tok = lambda b, cc: (b * nc + (nc - 1 - cc), 0)
    else:
        tok = lambda b, cc: (b * nc + cc, 0)
    tspec = pl.BlockSpec((c, d), tok)
    full = lambda shape: pl.BlockSpec(shape, lambda b, cc: (0,) * len(shape))
    sspec = pl.BlockSpec((None, nq, wd, wd), lambda b, cc: (b, 0, 0, 0))
    consts = _wkv_consts(rev)
    y, sf = pl.pallas_call(
        functools.partial(_wkv_kernel, rev=rev, chunk=c, heads=g, nq=nq, need_y=need_y, group=WKV_GROUP),
        out_shape=[jax.ShapeDtypeStruct((n, d), F32), jax.ShapeDtypeStruct((bsz, nq, wd, wd), F32)],
        grid=(bsz, nc),
        in_specs=[tspec] * 5 + [full((1, d)), full((1, d)), sspec,
                                full((rn, wd)), full((rn, rn)), full((rn, rn)), full((c, c)), full((wd, wd)),
                                full((rn, rn))],
        out_specs=[tspec, sspec],
        scratch_shapes=[pltpu.VMEM((nq, wd, wd), F32)],
        compiler_params=_cparams(("arbitrary", "arbitrary")),
        name="wkv_scan",
    )(r, k, v, lw, al, kkp, kap, s0, *consts)
    return y, sf


def _rwkv_out_kernel(yf_ref, yb_ref, bon_ref, g_ref, x_ref, gt_ref, lnw_ref, lnb_ref, ones_ref, wo_ref, o_ref,
                     *, wd):
    y = yf_ref[...] + yb_ref[...]
    ones = ones_ref[...]
    d = y.shape[1]
    inv = 1.0 / RWKV_HEAD

    def headsum(z):
        return jnp.concatenate([_dot_x2(z[:, q * wd:(q + 1) * wd], ones) for q in range(d // wd)], axis=1)

    dev = y - headsum(y) * inv
    var = headsum(dev * dev) * inv
    yn = dev * lax.rsqrt(var + GN_EPS) * lnw_ref[...] + lnb_ref[...]
    o = (yn + bon_ref[...]) * g_ref[...]
    o_ref[...] = x_ref[...] + gt_ref[...] * _dot(o.astype(BF16), wo_ref[...])


def _rwkv_out(yf, yb, bon, g, x, mods, row_of_tile, lnw, lnb, wo, *, tm):
    n, d = x.shape
    wd = WKV_HEADS_PER_BLOCK * RWKV_HEAD
    hl = np.arange(wd) // RWKV_HEAD
    ones = jnp.asarray(hl[:, None] == hl[None, :], BF16)
    tspec = pl.BlockSpec((tm, d), lambda i: (i, 0))
    full = lambda shape: pl.BlockSpec(shape, lambda i: (0,) * len(shape))
    return pl.pallas_call(
        functools.partial(_rwkv_out_kernel, wd=wd),
        out_shape=jax.ShapeDtypeStruct((n, d), F32),
        grid=(n // tm,),
        in_specs=[tspec] * 5 + [pl.BlockSpec((None, 1, d), lambda i: (row_of_tile(i) * N_MOD + 5, 0, 0)),
                                full((1, d)), full((1, d)), full((wd, wd)), full((d, d))],
        out_specs=tspec,
        compiler_params=_cparams(("arbitrary",)),
        name="rwkv_out",
    )(yf, yb, bon, g, x, mods, lnw, lnb, ones, wo)


def _rwkv_params(j, d, rwkv_mu, rwkv_w_rkv, rwkv_w0, rwkv_w1, rwkv_w2, rwkv_a0, rwkv_a1, rwkv_a2, rwkv_g1,
                 rwkv_g2, rwkv_k_a, rwkv_r_k):
    lr = rwkv_g1.shape[2]
    rank = rwkv_w1.shape[3]
    assert 2 * rank <= lr

    def first(w):
        return jnp.pad(jnp.concatenate([w[0], w[1]], axis=1), ((0, 0), (0, lr - 2 * rank))).astype(BF16)

    def second(w, z):
        return jnp.pad(w, ((z * rank, lr - (z + 1) * rank), (0, 0))).astype(BF16)

    vec = jnp.stack([rwkv_w0[j, 0], rwkv_w0[j, 1], rwkv_a0[j, 0], rwkv_a0[j, 1], rwkv_k_a[j],
                     rwkv_r_k[j].reshape(d), jnp.zeros((d,), F32), jnp.zeros((d,), F32)])
    hl = np.arange(256) // RWKV_HEAD
    return dict(
        mu=jnp.pad(rwkv_mu[j], ((0, 2), (0, 0))),
        w1c=first(rwkv_w1[j]), a1c=first(rwkv_a1[j]), g1=rwkv_g1[j].astype(BF16),
        wr=rwkv_w_rkv[j, 0].astype(BF16), wk=rwkv_w_rkv[j, 1].astype(BF16), wv=rwkv_w_rkv[j, 2].astype(BF16),
        w2f=second(rwkv_w2[j, 0], 0), w2b=second(rwkv_w2[j, 1], 1),
        a2f=second(rwkv_a2[j, 0], 0), a2b=second(rwkv_a2[j, 1], 1),
        g2=rwkv_g2[j].astype(BF16), vec=vec,
        ones_head=jnp.asarray(hl[:, None] == hl[None, :], BF16),
    )


def kernel(x, c, ctx, c_ctx, mod_w, mod_b, norm_w, ffn_w13, ffn_w2, fnet_w_o, fnet_b_o, rwkv_mu, rwkv_w_rkv,
           rwkv_w0, rwkv_w1, rwkv_w2, rwkv_a0, rwkv_a1, rwkv_a2, rwkv_g1, rwkv_g2, rwkv_k_k, rwkv_k_a, rwkv_r_k,
           rwkv_ln_w, rwkv_ln_b, rwkv_w_o, final_norm_w):
    bsz, seq, d = x.shape
    lc = ctx.shape[1]
    depth = mod_w.shape[0]
    assert depth == 2 and bsz == 2, "layer schedule below is written for the two-layer, batch-2 trunk"
    tm = 512
    assert seq % tm == 0 and seq % GRID_W == 0

    c8 = jnp.concatenate([c, c_ctx[None], jnp.zeros((8 - bsz - 1, d), F32)], axis=0)
    mods_all = _adaln(c8, mod_w, mod_b)
    w13 = ffn_w13.astype(BF16)
    w2 = ffn_w2.astype(BF16)
    fw = final_norm_w.reshape(1, d)

    xl = x.reshape(bsz * seq, d)
    xc = ctx.reshape(bsz * lc, d)
    lat_row = lambda i: i // (seq // tm)
    ctx_row = lambda i: 2

    mods = mods_all[0, :3].reshape(3 * N_MOD, 1, d)
    nw = norm_w[0].reshape(3, 1, d)
    xl, hl = _ffn(xl, mods, lat_row, 0, nw[0], w13[0, 0], w2[0, 0], fw, tm=tm, next_nw=nw[1])
    xc, hc = _ffn(xc, mods, ctx_row, 0, nw[0], w13[0, 0], w2[0, 0], fw, tm=lc, next_nw=nw[1])
    wo = fnet_w_o[0].astype(BF16)
    bo = fnet_b_o[0].reshape(1, d)
    xl = _fourier_latent(xl, hl, bsz, seq, mods, wo, bo)
    xc = _fourier_ctx(xc, hc, bsz, lc, mods, wo, bo)
    xl = _ffn(xl, mods, lat_row, 2, nw[2], w13[0, 1], w2[0, 1], fw, tm=tm)
    xc = _ffn(xc, mods, ctx_row, 2, nw[2], w13[0, 1], w2[0, 1], fw, tm=lc)

    mods = mods_all[1, :3].reshape(3 * N_MOD, 1, d)
    nw = norm_w[1].reshape(3, 1, d)
    xl = _ffn(xl, mods, lat_row, 0, nw[0], w13[1, 0], w2[1, 0], fw, tm=tm)
    xc = _ffn(xc, mods, ctx_row, 0, nw[0], w13[1, 0], w2[1, 0], fw, tm=lc)
    prm = _rwkv_params(0, d, rwkv_mu, rwkv_w_rkv, rwkv_w0, rwkv_w1, rwkv_w2, rwkv_a0, rwkv_a1, rwkv_a2, rwkv_g1,
                       rwkv_g2, rwkv_k_a, rwkv_r_k)
    r_c, k_c, v_c, _, lwf_c, lwb_c, alf_c, alb_c, _ = _rwkv_proj(
        xc, mods, ctx_row, nw[1], prm, tm=lc, grid_mode=False, tpb=1)
    r_l, k_l, v_l, g_l, lwf_l, lwb_l, alf_l, alb_l, bon_l = _rwkv_proj(
        xl, mods, lat_row, nw[1], prm, tm=tm, grid_mode=True, tpb=seq // tm)
    kkp = rwkv_k_k[0].reshape(1, d)
    kap = rwkv_k_a[0].reshape(1, d)
    wd = WKV_HEADS_PER_BLOCK * RWKV_HEAD
    s0 = jnp.zeros((bsz, d // wd, wd, wd), F32)
    _, s_f = _wkv(r_c, k_c, v_c, lwf_c, alf_c, kkp, kap, s0, bsz, rev=False, need_y=False)
    _, s_b = _wkv(r_c, k_c, v_c, lwb_c, alb_c, kkp, kap, s0, bsz, rev=True, need_y=False)
    yf, _ = _wkv(r_l, k_l, v_l, lwf_l, alf_l, kkp, kap, s_f, bsz, rev=False, need_y=True)
    yb, _ = _wkv(r_l, k_l, v_l, lwb_l, alb_l, kkp, kap, s_b, bsz, rev=True, need_y=True)
    xl = _rwkv_out(yf, yb, bon_l, g_l, xl, mods, lambda i: i // (seq // 256), rwkv_ln_w[0].reshape(1, d),
                   rwkv_ln_b[0].reshape(1, d), rwkv_w_o[0].astype(BF16), tm=256)
    xl = _ffn(xl, mods, lat_row, 2, nw[2], w13[1, 1], w2[1, 1], fw, tm=tm, final_norm=True)
    return xl.reshape(bsz, seq, d)
```

```python
import functools
import math

import numpy as np
import jax
import jax.numpy as jnp
from jax import lax
from jax.experimental import pallas as pl
from jax.experimental.pallas import tpu as pltpu

F32 = jnp.float32
BF16 = jnp.bfloat16

NORM_EPS = 1e-6
GN_EPS = 64e-5
KK_EPS = 1e-24
GRID_W = 64
FNET_GROUPS = 8
RWKV_HEAD = 64
N_MOD = 9
DECAY_SCALE = math.exp(-0.5)

LANE = 128
WKV_CHUNK = 64
WKV_HEADS_PER_BLOCK = 4
WKV_GROUP = 4
VMEM_LIMIT = 56 * 1024 * 1024


def _cparams(sem):
    return pltpu.CompilerParams(dimension_semantics=sem, vmem_limit_bytes=VMEM_LIMIT)


def _sigmoid(x):
    return 1.0 / (1.0 + jnp.exp(-x))


def _modulate(x, nw, shift, scale):
    y = x * lax.rsqrt(jnp.mean(x * x, axis=-1, keepdims=True) + NORM_EPS)
    return (y * nw) * (1.0 + scale) + shift


def _dot(a, b):
    return jnp.dot(a, b, preferred_element_type=F32)


def _dot_nt(a, b):
    return lax.dot_general(a, b, (((1,), (1,)), ((), ())), preferred_element_type=F32)


def _split2(x):
    hi = x.astype(BF16)
    lo = (x - hi.astype(F32)).astype(BF16)
    return hi, lo


def _split3(x):
    hi = x.astype(BF16)
    r1 = x - hi.astype(F32)
    mid = r1.astype(BF16)
    lo = (r1 - mid.astype(F32)).astype(BF16)
    return hi, mid, lo


def _dot_x2(x, w_bf16):
    hi, lo = _split2(x)
    return _dot(hi, w_bf16) + _dot(lo, w_bf16)


def _adaln_kernel(c_ref, w_ref, b_ref, o_ref):
    c = c_ref[...]
    s = c * _sigmoid(c)
    o_ref[...] = _dot(s.astype(BF16), w_ref[...].astype(BF16)) + b_ref[...]


def _adaln(c8, mod_w, mod_b):
    depth, d, nd = mod_w.shape
    tn = 1024 if nd % 1024 == 0 else 512
    return pl.pallas_call(
        _adaln_kernel,
        out_shape=jax.ShapeDtypeStruct((depth, 8, nd), F32),
        grid=(depth, nd // tn),
        in_specs=[
            pl.BlockSpec((8, d), lambda l, j: (0, 0)),
            pl.BlockSpec((None, d, tn), lambda l, j: (l, 0, j)),
            pl.BlockSpec((None, 1, tn), lambda l, j: (l, 0, j)),
        ],
        out_specs=pl.BlockSpec((None, 8, tn), lambda l, j: (l, 0, j)),
        compiler_params=_cparams(("arbitrary", "arbitrary")),
        name="adaln_linear",
    )(c8, mod_w, mod_b.reshape(depth, 1, nd))


def _ffn_kernel(*refs, final_norm, emit_h):
    x_ref, sh_ref, sc_ref, gt_ref, nw_ref, w1g_ref, w1u_ref, w2_ref, fw_ref = refs[:9]
    if emit_h:
        nw2_ref, sh2_ref, sc2_ref, o_ref, h2_ref, h_ref, acc_ref = refs[9:]
    else:
        o_ref, h_ref, acc_ref = refs[9:]
    j = pl.program_id(1)

    @pl.when(j == 0)
    def _():
        h_ref[...] = _modulate(x_ref[...], nw_ref[...], sh_ref[...], sc_ref[...]).astype(BF16)
        acc_ref[...] = jnp.zeros_like(acc_ref)

    h = h_ref[...]
    g = _dot(h, w1g_ref[...])
    u = _dot(h, w1u_ref[...])
    a = (g * _sigmoid(g)) * u
    acc_ref[...] += _dot(a.astype(BF16), w2_ref[...])

    @pl.when(j == pl.num_programs(1) - 1)
    def _():
        y = x_ref[...] + (0.5 * gt_ref[...]) * acc_ref[...]
        if final_norm:
            y = y * lax.rsqrt(jnp.mean(y * y, axis=-1, keepdims=True) + NORM_EPS) * fw_ref[...]
        o_ref[...] = y
        if emit_h:
            h2_ref[...] = _modulate(y, nw2_ref[...], sh2_ref[...], sc2_ref[...])


def _ffn(x, mods, row_of_tile, sub, nw, w13, w2, wi, fw, *, tm, final_norm=False, next_nw=None):
    n, d = x.shape
    f = w2.shape[1]
    tf = 512
    nf = f // tf
    emit_h = next_nw is not None

    def mod_spec(k):
        return pl.BlockSpec((None, 1, d), lambda i, j: (row_of_tile(i) * N_MOD + 3 * sub + k, 0, 0))

    vec_spec = pl.BlockSpec((1, d), lambda i, j: (0, 0))
    tile_spec = pl.BlockSpec((tm, d), lambda i, j: (i, 0))
    out = jax.ShapeDtypeStruct((n, d), F32)
    args = [x, mods, mods, mods, nw, w13, w13, w2, fw]
    specs = [tile_spec, mod_spec(0), mod_spec(1), mod_spec(2), vec_spec,
             pl.BlockSpec((None, d, tf), lambda i, j: (wi, 0, j)),
             pl.BlockSpec((None, d, tf), lambda i, j: (wi, 0, nf + j)),
             pl.BlockSpec((None, tf, d), lambda i, j: (wi, j, 0)),
             vec_spec]
    if emit_h:
        args += [next_nw, mods, mods]
        specs += [vec_spec, mod_spec(3), mod_spec(4)]
    return pl.pallas_call(
        functools.partial(_ffn_kernel, final_norm=final_norm, emit_h=emit_h),
        out_shape=[out, out] if emit_h else out,
        grid=(n // tm, nf),
        in_specs=specs,
        out_specs=[tile_spec, tile_spec] if emit_h else tile_spec,
        scratch_shapes=[pltpu.VMEM((tm, d), BF16), pltpu.VMEM((tm, d), F32)],
        compiler_params=_cparams(("arbitrary", "arbitrary")),
        name="ffn_swiglu",
    )(*args)


def _dft_tables(n):
    k = np.arange(n, dtype=np.int64)
    ang = 2.0 * np.pi * ((k[:, None] * k[None, :]) % n).astype(np.float64) / n
    return np.cos(ang), np.sin(ang)


def _hi_lo(m):
    m = jnp.asarray(m, F32)
    hi = m.astype(BF16)
    return hi, (m - hi.astype(F32)).astype(BF16)


DFT_SUB = 8
DFT_COLS = 512


def _dft_kernel(*refs, n_planes, mode, twiddle):
    it = iter(refs)
    x_ref = next(it)
    mats = [(next(it)[...], next(it)[...]) for _ in range(n_planes)]
    if twiddle:
        tc_ref, ts_ref = next(it), next(it)
    o_ref = next(it)
    if mode != "flat":
        stage_ref = next(it)

    def dense(p, strided):
        stage_ref[p] = strided
        return stage_ref[p]

    def transform(planes):
        y = None
        for x, (mh, ml) in zip(planes, mats):
            xh, xl = _split2(x)
            part = _dot(mh, xh) + (_dot(mh, xl) + _dot(ml, xh))
            y = part if y is None else y + part
        half = y.shape[0] // 2
        return y[:half], y[half:]

    if mode == "flat":
        yr, yi = transform([x_ref[...]])
        o_ref[0] = yr
        o_ref[1] = yi
    elif mode == "a":
        for j in range(DFT_SUB):
            yr, yi = transform([dense(0, x_ref[:, j, :])])
            if twiddle:
                reps = yr.shape[1] // LANE
                c = jnp.concatenate([tc_ref[j]] * reps, axis=1)
                s = jnp.concatenate([ts_ref[j]] * reps, axis=1)
                yr, yi = yr * c + yi * s, yi * c - yr * s
            o_ref[0, j] = yr
            o_ref[1, j] = yi
    else:
        for j in range(DFT_SUB):
            yr, yi = transform([dense(0, x_ref[0, :, j, :]), dense(1, x_ref[1, :, j, :])])
            o_ref[0, :, j, :] = yr
            o_ref[1, :, j, :] = yi


def _dft_call(mode, x, mats, out_shape, grid, x_spec, o_spec, tw=None, tw_spec=None):
    args, specs = [x], [x_spec]
    for m in mats:
        mh, ml = _hi_lo(m)
        args += [mh, ml]
        specs += [pl.BlockSpec(m.shape, lambda *_: (0, 0))] * 2
    if tw is not None:
        args += list(tw)
        specs += [tw_spec] * 2
    scratch = []
    if mode != "flat":
        scratch = [pltpu.VMEM((len(mats), mats[0].shape[1], x_spec.block_shape[-1]), F32)]
    return pl.pallas_call(
        functools.partial(_dft_kernel, n_planes=len(mats), mode=mode, twiddle=tw is not None),
        out_shape=jax.ShapeDtypeStruct(out_shape, F32),
        grid=grid,
        in_specs=specs,
        out_specs=o_spec,
        scratch_shapes=scratch,
        compiler_params=_cparams(("arbitrary",) * len(grid)),
        name="dft_" + mode,
    )(*args)


def _fmix_kernel(pr_ref, pi_ref, x_ref, gt_ref, ch_ref, cl_ref, wo_ref, bo_ref, o_ref, *, groups):
    pr, pi = pr_ref[...], pi_ref[...]
    gd = pr.shape[1] // groups
    ch, cl = ch_ref[...], cl_ref[...]
    outs = []
    for g in range(groups):
        z = jnp.concatenate([pr[:, g * gd:(g + 1) * gd], pi[:, g * gd:(g + 1) * gd]], axis=1)
        zh, zl = _split2(z)
        outs.append(_dot(zh, ch) + (_dot(zl, ch) + _dot(zh, cl)))
    f = jnp.concatenate(outs, axis=1)
    o = _dot(f.astype(BF16), wo_ref[...]) + bo_ref[...]
    o_ref[...] = x_ref[...] + gt_ref[...] * o


def _fmix(p, x, mods, row_of_tile, wo, bo, *, tm):
    n, d = x.shape
    b, _, l, _ = p.shape
    tpb = l // tm
    gd = d // FNET_GROUPS
    c, s = _dft_tables(gd)
    ch, cl = _hi_lo(np.concatenate([c, s], axis=0) / math.sqrt(gd))
    return pl.pallas_call(
        functools.partial(_fmix_kernel, groups=FNET_GROUPS),
        out_shape=jax.ShapeDtypeStruct((n, d), F32),
        grid=(n // tm,),
        in_specs=[
            pl.BlockSpec((None, None, tm, d), lambda i: (i // tpb, 0, i % tpb, 0)),
            pl.BlockSpec((None, None, tm, d), lambda i: (i // tpb, 1, i % tpb, 0)),
            pl.BlockSpec((tm, d), lambda i: (i, 0)),
            pl.BlockSpec((None, 1, d), lambda i: (row_of_tile(i) * N_MOD + 5, 0, 0)),
            pl.BlockSpec((2 * gd, gd), lambda i: (0, 0)),
            pl.BlockSpec((2 * gd, gd), lambda i: (0, 0)),
            pl.BlockSpec((d, d), lambda i: (0, 0)),
            pl.BlockSpec((1, d), lambda i: (0, 0)),
        ],
        out_specs=pl.BlockSpec((tm, d), lambda i: (i, 0)),
        compiler_params=_cparams(("arbitrary",)),
        name="fourier_out",
    )(p, p, x, mods, ch, cl, wo, bo)


def _fourier_latent(xl, hl, bsz, seq, mods, wo, bo):
    n, d = xl.shape
    la = lb = int(round(math.sqrt(seq)))
    assert la * lb == seq and lb % DFT_SUB == 0 and la % DFT_SUB == 0
    td = min(DFT_COLS, d)
    s8 = DFT_SUB
    ca, sa = _dft_tables(la)
    m_a = np.concatenate([ca, -sa], axis=0) / math.sqrt(la)
    cb, sb = _dft_tables(lb)
    m_br = np.concatenate([cb, -sb], axis=0) / math.sqrt(lb)
    m_bi = np.concatenate([sb, cb], axis=0) / math.sqrt(lb)
    n2 = np.arange(lb, dtype=np.int64)[:, None]
    k1 = np.arange(la, dtype=np.int64)[None, :]
    ang = 2.0 * np.pi * ((n2 * k1) % seq).astype(np.float64) / seq
    twc = jnp.asarray(np.broadcast_to(np.cos(ang)[:, :, None], (lb, la, LANE)), F32)
    tws = jnp.asarray(np.broadcast_to(np.sin(ang)[:, :, None], (lb, la, LANE)), F32)
    grid = (bsz, lb // s8, d // td)
    a = _dft_call(
        "a", hl.reshape(bsz, la, lb, d), [m_a], (bsz, 2, lb, la, d), grid,
        pl.BlockSpec((None, la, s8, td), lambda b, g, c: (b, 0, g, c)),
        pl.BlockSpec((None, 2, s8, la, td), lambda b, g, c: (b, 0, g, 0, c)),
        tw=(twc, tws), tw_spec=pl.BlockSpec((s8, la, LANE), lambda b, g, c: (g, 0, 0)))
    p = _dft_call(
        "b", a, [m_br, m_bi], (bsz, 2, lb, la, d), (bsz, la // s8, d // td),
        pl.BlockSpec((None, 2, lb, s8, td), lambda b, g, c: (b, 0, 0, g, c)),
        pl.BlockSpec((None, 2, lb, s8, td), lambda b, g, c: (b, 0, 0, g, c)))
    p = p.reshape(bsz, 2, seq, d)
    return _fmix(p, xl, mods, lambda i: i // (seq // 256), wo, bo, tm=256)


def _fourier_ctx(xc, hc, bsz, lc, mods, wo, bo):
    n, d = xc.shape
    td = min(DFT_COLS, d)
    c, s = _dft_tables(lc)
    m = np.concatenate([c, -s], axis=0) / math.sqrt(lc)
    p = _dft_call(
        "flat", hc.reshape(bsz, lc, d), [m], (bsz, 2, lc, d), (bsz, d // td),
        pl.BlockSpec((None, lc, td), lambda b, c: (b, 0, c)),
        pl.BlockSpec((None, 2, lc, td), lambda b, c: (b, 0, 0, c)))
    return _fmix(p, xc, mods, lambda i: 2, wo, bo, tm=lc)


def _rwkv_proj_kernel(*refs, grid_mode, tpb):
    it = iter(refs)
    x_ref = next(it)
    if grid_mode:
        xp_ref, xn_ref = next(it), next(it)
    nw_ref, sh_ref, sc_ref, mu_ref = next(it), next(it), next(it), next(it)
    w1_ref, a1_ref, g1_ref = next(it), next(it), next(it)
    wr_ref, wk_ref, wv_ref = next(it), next(it), next(it)
    w2f_ref, w2b_ref, a2f_ref, a2b_ref, g2_ref = next(it), next(it), next(it), next(it), next(it)
    vec_ref, ones_ref = next(it), next(it)
    r_ref, k_ref, v_ref, g_ref, lwf_ref, lwb_ref, alf_ref, alb_ref, bon_ref = (next(it) for _ in range(9))
    hr_ref, hk_ref, hv_ref, tw_ref, ta_ref, tg_ref = (next(it) for _ in range(6))

    i = pl.program_id(0)
    j = pl.program_id(1)

    @pl.when(j == 0)
    def _():
        nw, sh, sc = nw_ref[...], sh_ref[...], sc_ref[...]
        h = _modulate(x_ref[...], nw, sh, sc)
        tm, d = h.shape
        q = d // 4
        row = lax.broadcasted_iota(jnp.int32, (tm, 1), 0)
        if grid_mode:
            tib = i % tpb
            up_ok = (tib != 0).astype(F32)
            dn_ok = (tib != tpb - 1).astype(F32)
            hp = _modulate(xp_ref[...], nw, sh, sc)[:, 2 * q:3 * q] * up_ok
            hn = _modulate(xn_ref[...], nw, sh, sc)[:, 3 * q:] * dn_ok
            col = row % GRID_W
            left = jnp.where(col != 0, pltpu.roll(h[:, :q], 1, 0), 0.0)
            right = jnp.where(col != GRID_W - 1, pltpu.roll(h[:, q:2 * q], tm - 1, 0), 0.0)
            up = jnp.concatenate([hp, h[:tm - GRID_W, 2 * q:3 * q]], axis=0)
            down = jnp.concatenate([h[GRID_W:, 3 * q:], hn], axis=0)
            hs = jnp.concatenate([left, right, up, down], axis=1)
        else:
            prev = jnp.where(row != 0, pltpu.roll(h, 1, 0), 0.0)
            nxt = jnp.where(row != tm - 1, pltpu.roll(h, tm - 1, 0), 0.0)
            hs = jnp.concatenate([prev[:, :q], nxt[:, q:2 * q], prev[:, 2 * q:3 * q], nxt[:, 3 * q:]], axis=1)
        dlt = hs - h
        mu = mu_ref[...]
        hr_ref[...] = (h + dlt * mu[0:1]).astype(BF16)
        hk_ref[...] = (h + dlt * mu[1:2]).astype(BF16)
        hv_ref[...] = (h + dlt * mu[2:3]).astype(BF16)
        tw_ref[...] = jnp.tanh(_dot((h + dlt * mu[3:4]).astype(BF16), w1_ref[...])).astype(BF16)
        ta_ref[...] = _dot((h + dlt * mu[4:5]).astype(BF16), a1_ref[...]).astype(BF16)
        tg_ref[...] = _sigmoid(_dot((h + dlt * mu[5:6]).astype(BF16), g1_ref[...])).astype(BF16)

    vec = vec_ref[...]
    w0f, w0b, a0f, a0b, ka, rk = (vec[n:n + 1] for n in range(6))
    r = _dot(hr_ref[...], wr_ref[...])
    k = _dot(hk_ref[...], wk_ref[...])
    v = _dot(hv_ref[...], wv_ref[...])
    tw = tw_ref[...]
    ta = ta_ref[...]
    lwf = -DECAY_SCALE * _sigmoid(w0f + _dot(tw, w2f_ref[...]))
    lwb = -DECAY_SCALE * _sigmoid(w0b + _dot(tw, w2b_ref[...]))
    alf = _sigmoid(a0f + _dot(ta, a2f_ref[...]))
    alb = _sigmoid(a0b + _dot(ta, a2b_ref[...]))
    kb = k * (1.0 + (0.5 * (alf + alb) - 1.0) * ka)
    r_ref[...] = r
    k_ref[...] = k
    v_ref[...] = v
    g_ref[...] = _dot(tg_ref[...], g2_ref[...])
    lwf_ref[...] = lwf
    lwb_ref[...] = lwb
    alf_ref[...] = alf
    alb_ref[...] = alb
    bon_ref[...] = _dot_x2(r * kb * rk, ones_ref[...]) * v


def _rwkv_proj(x, mods, row_of_tile, nw, prm, *, tm, grid_mode, tpb):
    n, d = x.shape
    tn = 256
    nj = d // tn
    hw = GRID_W
    lr = prm["w1c"].shape[1]

    def mod_spec(k):
        return pl.BlockSpec((None, 1, d), lambda i, j: (row_of_tile(i) * N_MOD + 3 + k, 0, 0))

    args, specs = [x], [pl.BlockSpec((tm, d), lambda i, j: (i, 0))]
    if grid_mode:
        nb = n // hw
        r = tm // hw
        args += [x, x]
        specs += [pl.BlockSpec((hw, d), lambda i, j: (jnp.maximum(i * r - 1, 0), 0)),
                  pl.BlockSpec((hw, d), lambda i, j: (jnp.minimum((i + 1) * r, nb - 1), 0))]
    args += [nw, mods, mods, prm["mu"], prm["w1c"], prm["a1c"], prm["g1"], prm["wr"], prm["wk"], prm["wv"],
             prm["w2f"], prm["w2b"], prm["a2f"], prm["a2b"], prm["g2"], prm["vec"], prm["ones_head"]]
    full = lambda shape: pl.BlockSpec(shape, lambda i, j: (0,) * len(shape))
    coltile = lambda rows: pl.BlockSpec((rows, tn), lambda i, j: (0, j))
    specs += [full((1, d)), mod_spec(0), mod_spec(1), full((8, d)), full((d, lr)), full((d, lr)), full((d, lr)),
              coltile(d), coltile(d), coltile(d), coltile(lr), coltile(lr), coltile(lr), coltile(lr), coltile(lr),
              coltile(8), full((tn, tn))]
    out = jax.ShapeDtypeStruct((n, d), F32)
    return pl.pallas_call(
        functools.partial(_rwkv_proj_kernel, grid_mode=grid_mode, tpb=tpb),
        out_shape=[out] * 9,
        grid=(n // tm, nj),
        in_specs=specs,
        out_specs=[pl.BlockSpec((tm, tn), lambda i, j: (i, j))] * 9,
        scratch_shapes=[pltpu.VMEM((tm, d), BF16)] * 3 + [pltpu.VMEM((tm, lr), BF16)] * 3,
        compiler_params=_cparams(("arbitrary", "arbitrary")),
        name="rwkv_proj",
    )(*args)


def _wkv_kernel(r_ref, k_ref, v_ref, lw_ref, al_ref, kk_ref, ka_ref, s0_ref, bm_ref, ms_ref, mi_ref, tri_ref,
                ones_ref, eye_ref, y_ref, sf_ref, s_scr, *, rev, chunk, heads, nq, need_y, group):
    cc = pl.program_id(1)
    c = chunk
    wd = heads * RWKV_HEAD
    rn = heads * c

    @pl.when(cc == 0)
    def _():
        s_scr[...] = s0_ref[...]

    bm = bm_ref[...]
    bmf = bm.astype(F32)
    strict = ms_ref[...] > 0.0
    incl = mi_ref[...] > 0.0
    tri = tri_ref[...]
    ones = ones_ref[...]
    eye = eye_ref[...]

    def stack(x):
        return jnp.concatenate([x.astype(BF16)] * heads, axis=0) * bm

    if not need_y:
        y_ref[...] = jnp.zeros_like(y_ref)

    def block(q):
        sl = slice(q * wd, (q + 1) * wd)
        r, k, v, lw, al = r_ref[:, sl], k_ref[:, sl], v_ref[:, sl], lw_ref[:, sl], al_ref[:, sl]
        kk0 = k * kk_ref[:, sl]
        kk = kk0 * lax.rsqrt(jnp.maximum(_dot_x2(kk0 * kk0, ones), KK_EPS))
        b = kk * al
        kd = k * (1.0 + (al - 1.0) * ka_ref[:, sl])
        l3 = _split3(lw)
        lg = _dot(tri, l3[0]) + (_dot(tri, l3[1]) + _dot(tri, l3[2]))
        gc = lg[0:1] if rev else lg[c - 1:c]
        gin = jnp.exp(-lg)
        gout = jnp.exp(gc - lg)
        a_n = -kk * jnp.exp(lg - lw)
        r_n = r * jnp.exp(lg)
        a_s = stack(a_n)
        v_s = stack(v)
        bk = jnp.concatenate([stack(b * gin), stack(kd * gin)], axis=0)
        bo_n = (b * gout).astype(BF16)
        ko_n = (kd * gout).astype(BF16)
        yield

        lhs = jnp.concatenate([a_n, r_n], axis=0) if need_y else a_n
        gm = _dot_nt(lhs.astype(BF16), bk)
        aab = jnp.where(strict, gm[:c, :rn], 0.0)
        aak = jnp.where(strict, gm[:c, rn:], 0.0).astype(BF16)
        if need_y:
            arb = jnp.where(incl, gm[c:, :rn], 0.0).astype(BF16)
            ark = jnp.where(incl, gm[c:, rn:], 0.0).astype(BF16)
        yield
        w0 = _dot(aak, v_s)

        pm = eye + aab
        xb = aab.astype(BF16)
        x = _dot(xb, stack(xb))
        yield
        steps = int(math.log2(c))
        for _ in range(steps - 2):
            xb = x.astype(BF16)
            res = _dot(xb, jnp.concatenate([stack(xb), stack(pm)], axis=1))
            x = res[:, :rn]
            pm = pm + res[:, rn:]
            yield
        t = pm + _dot(x.astype(BF16), stack(pm))
        yield

        av = _dot(t.astype(BF16), jnp.concatenate([a_s, stack(w0)], axis=1))
        yield
        s = s_scr[q]
        sb = s.astype(BF16)
        if need_y:
            ry = _dot(arb, jnp.concatenate([stack(av[:, :wd]), stack(av[:, wd:])], axis=1))
            rh = r_n + ry[:, :wd]
            yk = _dot(ark, v_s)
        pd = _dot(av.T.astype(BF16), bo_n)
        dt = (pd[wd:] + _dot(v.T.astype(BF16), ko_n)) * bmf
        pt = (pd[:wd] * bmf).astype(BF16)
        yield
        if need_y:
            y_ref[:, sl] = _dot_nt(rh.astype(BF16), sb) + ry[:, wd:] + yk
        s_scr[q] = s * jnp.exp(gc) + _dot(sb, pt) + dt

    for q0 in range(0, nq, group):
        live = [block(q) for q in range(q0, min(q0 + group, nq))]
        while live:
            nxt = []
            for gen in live:
                try:
                    next(gen)
                    nxt.append(gen)
                except StopIteration:
                    pass
            live = nxt

    @pl.when(cc == pl.num_programs(1) - 1)
    def _():
        sf_ref[...] = s_scr[...]


def _wkv_consts(rev):
    c, g = WKV_CHUNK, WKV_HEADS_PER_BLOCK
    wd, rn = g * RWKV_HEAD, g * c
    assert c == RWKV_HEAD, "one (rn, wd) mask serves both (head, time) and (head, channel) columns"
    hrow = np.arange(rn) // c
    hlane = np.arange(wd) // RWKV_HEAD
    bm = (hrow[:, None] == hlane[None, :]).astype(np.float32)
    t = np.arange(c)[:, None]
    src = (np.arange(rn) % c)[None, :]
    if rev:
        strict, incl = src > t, src >= t
        tri = np.triu(np.ones((c, c), np.float32))
    else:
        strict, incl = src < t, src <= t
        tri = np.tril(np.ones((c, c), np.float32))
    ones = (hlane[:, None] == hlane[None, :]).astype(np.float32)
    return (jnp.asarray(bm, BF16), jnp.asarray(strict, F32), jnp.asarray(incl, F32), jnp.asarray(tri, BF16),
            jnp.asarray(ones, BF16), jnp.asarray(src == t, F32))


def _wkv(r, k, v, lw, al, kkp, kap, s0, bsz, *, rev, need_y):
    n, d = r.shape
    c, g = WKV_CHUNK, WKV_HEADS_PER_BLOCK
    wd, rn = g * RWKV_HEAD, g * c
    nq = d // wd
    nc = n // bsz // c
    if rev:
        tok = lambda b, cc: (b * nc + (nc - 1 - cc), 0)
    else:
        tok = lambda b, cc: (b * nc + cc, 0)
    tspec = pl.BlockSpec((c, d), tok)
    full = lambda shape: pl.BlockSpec(shape, lambda b, cc: (0,) * len(shape))
    sspec = pl.BlockSpec((None, nq, wd, wd), lambda b, cc: (b, 0, 0, 0))
    consts = _wkv_consts(rev)
    y, sf = pl.pallas_call(
        functools.partial(_wkv_kernel, rev=rev, chunk=c, heads=g, nq=nq, need_y=need_y, group=WKV_GROUP),
        out_shape=[jax.ShapeDtypeStruct((n, d), F32), jax.ShapeDtypeStruct((bsz, nq, wd, wd), F32)],
        grid=(bsz, nc),
        in_specs=[tspec] * 5 + [full((1, d)), full((1, d)), sspec,
                                full((rn, wd)), full((c, rn)), full((c, rn)), full((c, c)), full((wd, wd)),
                                full((c, rn))],
        out_specs=[tspec, sspec],
        scratch_shapes=[pltpu.VMEM((nq, wd, wd), F32)],
        compiler_params=_cparams(("arbitrary", "arbitrary")),
        name="wkv_scan",
    )(r, k, v, lw, al, kkp, kap, s0, *consts)
    return y, sf


def _rwkv_out_kernel(yf_ref, yb_ref, bon_ref, g_ref, x_ref, gt_ref, lnw_ref, lnb_ref, ones_ref, wo_ref, o_ref,
                     *, wd):
    y = yf_ref[...] + yb_ref[...]
    ones = ones_ref[...]
    d = y.shape[1]
    inv = 1.0 / RWKV_HEAD

    def headsum(z):
        return jnp.concatenate([_dot_x2(z[:, q * wd:(q + 1) * wd], ones) for q in range(d // wd)], axis=1)

    dev = y - headsum(y) * inv
    var = headsum(dev * dev) * inv
    yn = dev * lax.rsqrt(var + GN_EPS) * lnw_ref[...] + lnb_ref[...]
    o = (yn + bon_ref[...]) * g_ref[...]
    o_ref[...] = x_ref[...] + gt_ref[...] * _dot(o.astype(BF16), wo_ref[...])


def _rwkv_out(yf, yb, bon, g, x, mods, row_of_tile, lnw, lnb, wo, *, tm):
    n, d = x.shape
    wd = WKV_HEADS_PER_BLOCK * RWKV_HEAD
    hl = np.arange(wd) // RWKV_HEAD
    ones = jnp.asarray(hl[:, None] == hl[None, :], BF16)
    tspec = pl.BlockSpec((tm, d), lambda i: (i, 0))
    full = lambda shape: pl.BlockSpec(shape, lambda i: (0,) * len(shape))
    return pl.pallas_call(
        functools.partial(_rwkv_out_kernel, wd=wd),
        out_shape=jax.ShapeDtypeStruct((n, d), F32),
        grid=(n // tm,),
        in_specs=[tspec] * 5 + [pl.BlockSpec((None, 1, d), lambda i: (row_of_tile(i) * N_MOD + 5, 0, 0)),
                                full((1, d)), full((1, d)), full((wd, wd)), full((d, d))],
        out_specs=tspec,
        compiler_params=_cparams(("arbitrary",)),
        name="rwkv_out",
    )(yf, yb, bon, g, x, mods, lnw, lnb, ones, wo)


def _rwkv_params(j, d, rwkv_mu, rwkv_w_rkv, rwkv_w0, rwkv_w1, rwkv_w2, rwkv_a0, rwkv_a1, rwkv_a2, rwkv_g1,
                 rwkv_g2, rwkv_k_a, rwkv_r_k):
    lr = rwkv_g1.shape[2]
    rank = rwkv_w1.shape[3]
    assert 2 * rank <= lr

    def first(w):
        return jnp.pad(jnp.concatenate([w[0], w[1]], axis=1), ((0, 0), (0, lr - 2 * rank))).astype(BF16)

    def second(w, z):
        return jnp.pad(w, ((z * rank, lr - (z + 1) * rank), (0, 0))).astype(BF16)

    vec = jnp.stack([rwkv_w0[j, 0], rwkv_w0[j, 1], rwkv_a0[j, 0], rwkv_a0[j, 1], rwkv_k_a[j],
                     rwkv_r_k[j].reshape(d), jnp.zeros((d,), F32), jnp.zeros((d,), F32)])
    hl = np.arange(256) // RWKV_HEAD
    return dict(
        mu=jnp.pad(rwkv_mu[j], ((0, 2), (0, 0))),
        w1c=first(rwkv_w1[j]), a1c=first(rwkv_a1[j]), g1=rwkv_g1[j].astype(BF16),
        wr=rwkv_w_rkv[j, 0].astype(BF16), wk=rwkv_w_rkv[j, 1].astype(BF16), wv=rwkv_w_rkv[j, 2].astype(BF16),
        w2f=second(rwkv_w2[j, 0], 0), w2b=second(rwkv_w2[j, 1], 1),
        a2f=second(rwkv_a2[j, 0], 0), a2b=second(rwkv_a2[j, 1], 1),
        g2=rwkv_g2[j].astype(BF16), vec=vec,
        ones_head=jnp.asarray(hl[:, None] == hl[None, :], BF16),
    )


def kernel(x, c, ctx, c_ctx, mod_w, mod_b, norm_w, ffn_w13, ffn_w2, fnet_w_o, fnet_b_o, rwkv_mu, rwkv_w_rkv,
           rwkv_w0, rwkv_w1, rwkv_w2, rwkv_a0, rwkv_a1, rwkv_a2, rwkv_g1, rwkv_g2, rwkv_k_k, rwkv_k_a, rwkv_r_k,
           rwkv_ln_w, rwkv_ln_b, rwkv_w_o, final_norm_w):
    bsz, seq, d = x.shape
    lc = ctx.shape[1]
    depth = mod_w.shape[0]
    assert depth == 2 and bsz == 2, "layer schedule below is written for the two-layer, batch-2 trunk"
    tm = 512
    assert seq % tm == 0 and seq % GRID_W == 0

    c8 = jnp.concatenate([c, c_ctx[None], jnp.zeros((8 - bsz - 1, d), F32)], axis=0)
    mods_all = _adaln(c8, mod_w, mod_b)
    f2 = ffn_w13.shape[-1]
    w13 = ffn_w13.astype(BF16).reshape(2 * depth, d, f2)
    w2 = ffn_w2.astype(BF16).reshape(2 * depth, f2 // 2, d)
    fw = final_norm_w.reshape(1, d)

    xl = x.reshape(bsz * seq, d)
    xc = ctx.reshape(bsz * lc, d)
    lat_row = lambda i: i // (seq // tm)
    ctx_row = lambda i: 2

    mods = mods_all[0, :3].reshape(3 * N_MOD, 1, d)
    nw = norm_w[0].reshape(3, 1, d)
    xl, hl = _ffn(xl, mods, lat_row, 0, nw[0], w13, w2, 0, fw, tm=tm, next_nw=nw[1])
    xc, hc = _ffn(xc, mods, ctx_row, 0, nw[0], w13, w2, 0, fw, tm=lc, next_nw=nw[1])
    wo = fnet_w_o[0].astype(BF16)
    bo = fnet_b_o[0].reshape(1, d)
    xl = _fourier_latent(xl, hl, bsz, seq, mods, wo, bo)
    xc = _fourier_ctx(xc, hc, bsz, lc, mods, wo, bo)
    xl = _ffn(xl, mods, lat_row, 2, nw[2], w13, w2, 1, fw, tm=tm)
    xc = _ffn(xc, mods, ctx_row, 2, nw[2], w13, w2, 1, fw, tm=lc)

    mods = mods_all[1, :3].reshape(3 * N_MOD, 1, d)
    nw = norm_w[1].reshape(3, 1, d)
    xl = _ffn(xl, mods, lat_row, 0, nw[0], w13, w2, 2, fw, tm=tm)
    xc = _ffn(xc, mods, ctx_row, 0, nw[0], w13, w2, 2, fw, tm=lc)
    prm = _rwkv_params(0, d, rwkv_mu, rwkv_w_rkv, rwkv_w0, rwkv_w1, rwkv_w2, rwkv_a0, rwkv_a1, rwkv_a2, rwkv_g1,
                       rwkv_g2, rwkv_k_a, rwkv_r_k)
    r_c, k_c, v_c, _, lwf_c, lwb_c, alf_c, alb_c, _ = _rwkv_proj(
        xc, mods, ctx_row, nw[1], prm, tm=lc, grid_mode=False, tpb=1)
    r_l, k_l, v_l, g_l, lwf_l, lwb_l, alf_l, alb_l, bon_l = _rwkv_proj(
        xl, mods, lat_row, nw[1], prm, tm=tm, grid_mode=True, tpb=seq // tm)
    kkp = rwkv_k_k[0].reshape(1, d)
    kap = rwkv_k_a[0].reshape(1, d)
    wd = WKV_HEADS_PER_BLOCK * RWKV_HEAD
    s0 = jnp.zeros((bsz, d // wd, wd, wd), F32)
    _, s_f = _wkv(r_c, k_c, v_c, lwf_c, alf_c, kkp, kap, s0, bsz, rev=False, need_y=False)
    _, s_b = _wkv(r_c, k_c, v_c, lwb_c, alb_c, kkp, kap, s0, bsz, rev=True, need_y=False)
    yf, _ = _wkv(r_l, k_l, v_l, lwf_l, alf_l, kkp, kap, s_f, bsz, rev=False, need_y=True)
    yb, _ = _wkv(r_l, k_l, v_l, lwb_l, alb_l, kkp, kap, s_b, bsz, rev=True, need_y=True)
    xl = _rwkv_out(yf, yb, bon_l, g_l, xl, mods, lambda i: i // (seq // 256), rwkv_ln_w[0].reshape(1, d),
                   rwkv_ln_b[0].reshape(1, d), rwkv_w_o[0].astype(BF16), tm=256)
    xl = _ffn(xl, mods, lat_row, 2, nw[2], w13, w2, 3, fw, tm=tm, final_norm=True)
    return xl.reshape(bsz, seq, d)
```

```python
import functools
import math

import numpy as np
import jax
import jax.numpy as jnp
from jax import lax
from jax.experimental import pallas as pl
from jax.experimental.pallas import tpu as pltpu

F32 = jnp.float32
BF16 = jnp.bfloat16

NORM_EPS = 1e-6
GN_EPS = 64e-5
KK_EPS = 1e-24
GRID_W = 64
FNET_GROUPS = 8
RWKV_HEAD = 64
N_MOD = 9
DECAY_SCALE = math.exp(-0.5)

LANE = 128
WKV_CHUNK = 64
WKV_HEADS_PER_BLOCK = 4
WKV_GROUP = 8
VMEM_LIMIT = 56 * 1024 * 1024


def _cparams(sem):
    return pltpu.CompilerParams(dimension_semantics=sem, vmem_limit_bytes=VMEM_LIMIT)


def _sigmoid(x):
    return 1.0 / (1.0 + jnp.exp(-x))


def _modulate(x, nw, shift, scale):
    y = x * lax.rsqrt(jnp.mean(x * x, axis=-1, keepdims=True) + NORM_EPS)
    return (y * nw) * (1.0 + scale) + shift


def _dot(a, b):
    return jnp.dot(a, b, preferred_element_type=F32)


def _dot_nt(a, b):
    return lax.dot_general(a, b, (((1,), (1,)), ((), ())), preferred_element_type=F32)


def _split2(x):
    hi = x.astype(BF16)
    lo = (x - hi.astype(F32)).astype(BF16)
    return hi, lo


def _split3(x):
    hi = x.astype(BF16)
    r1 = x - hi.astype(F32)
    mid = r1.astype(BF16)
    lo = (r1 - mid.astype(F32)).astype(BF16)
    return hi, mid, lo


def _dot_x2(x, w_bf16):
    hi, lo = _split2(x)
    return _dot(hi, w_bf16) + _dot(lo, w_bf16)


def _adaln_kernel(c_ref, w_ref, b_ref, o_ref):
    c = c_ref[...]
    s = c * _sigmoid(c)
    o_ref[...] = _dot(s.astype(BF16), w_ref[...].astype(BF16)) + b_ref[...]


def _adaln(c8, mod_w, mod_b):
    depth, d, nd = mod_w.shape
    tn = 1024 if nd % 1024 == 0 else 512
    return pl.pallas_call(
        _adaln_kernel,
        out_shape=jax.ShapeDtypeStruct((depth, 8, nd), F32),
        grid=(depth, nd // tn),
        in_specs=[
            pl.BlockSpec((8, d), lambda l, j: (0, 0)),
            pl.BlockSpec((None, d, tn), lambda l, j: (l, 0, j)),
            pl.BlockSpec((None, 1, tn), lambda l, j: (l, 0, j)),
        ],
        out_specs=pl.BlockSpec((None, 8, tn), lambda l, j: (l, 0, j)),
        compiler_params=_cparams(("arbitrary", "arbitrary")),
        name="adaln_linear",
    )(c8, mod_w, mod_b.reshape(depth, 1, nd))


def _ffn_kernel(*refs, final_norm, emit_h):
    x_ref, sh_ref, sc_ref, gt_ref, nw_ref, w1g_ref, w1u_ref, w2_ref, fw_ref = refs[:9]
    if emit_h:
        nw2_ref, sh2_ref, sc2_ref, o_ref, h2_ref, h_ref, acc_ref = refs[9:]
    else:
        o_ref, h_ref, acc_ref = refs[9:]
    j = pl.program_id(1)

    @pl.when(j == 0)
    def _():
        h_ref[...] = _modulate(x_ref[...], nw_ref[...], sh_ref[...], sc_ref[...]).astype(BF16)
        acc_ref[...] = jnp.zeros_like(acc_ref)

    h = h_ref[...]
    g = _dot(h, w1g_ref[...])
    u = _dot(h, w1u_ref[...])
    a = (g * _sigmoid(g)) * u
    acc_ref[...] += _dot(a.astype(BF16), w2_ref[...])

    @pl.when(j == pl.num_programs(1) - 1)
    def _():
        y = x_ref[...] + (0.5 * gt_ref[...]) * acc_ref[...]
        if final_norm:
            y = y * lax.rsqrt(jnp.mean(y * y, axis=-1, keepdims=True) + NORM_EPS) * fw_ref[...]
        o_ref[...] = y
        if emit_h:
            h2_ref[...] = _modulate(y, nw2_ref[...], sh2_ref[...], sc2_ref[...])


def _ffn(x, mods, row_of_tile, sub, nw, w13, w2, wi, fw, *, tm, final_norm=False, next_nw=None):
    n, d = x.shape
    f = w2.shape[1]
    tf = 512
    nf = f // tf
    emit_h = next_nw is not None

    def mod_spec(k):
        return pl.BlockSpec((None, 1, d), lambda i, j: (row_of_tile(i) * N_MOD + 3 * sub + k, 0, 0))

    vec_spec = pl.BlockSpec((1, d), lambda i, j: (0, 0))
    tile_spec = pl.BlockSpec((tm, d), lambda i, j: (i, 0))
    out = jax.ShapeDtypeStruct((n, d), F32)
    args = [x, mods, mods, mods, nw, w13, w13, w2, fw]
    specs = [tile_spec, mod_spec(0), mod_spec(1), mod_spec(2), vec_spec,
             pl.BlockSpec((None, d, tf), lambda i, j: (wi, 0, j)),
             pl.BlockSpec((None, d, tf), lambda i, j: (wi, 0, nf + j)),
             pl.BlockSpec((None, tf, d), lambda i, j: (wi, j, 0)),
             vec_spec]
    if emit_h:
        args += [next_nw, mods, mods]
        specs += [vec_spec, mod_spec(3), mod_spec(4)]
    return pl.pallas_call(
        functools.partial(_ffn_kernel, final_norm=final_norm, emit_h=emit_h),
        out_shape=[out, out] if emit_h else out,
        grid=(n // tm, nf),
        in_specs=specs,
        out_specs=[tile_spec, tile_spec] if emit_h else tile_spec,
        scratch_shapes=[pltpu.VMEM((tm, d), BF16), pltpu.VMEM((tm, d), F32)],
        compiler_params=_cparams(("arbitrary", "arbitrary")),
        name="ffn_swiglu",
    )(*args)


def _dft_tables(n):
    k = np.arange(n, dtype=np.int64)
    ang = 2.0 * np.pi * ((k[:, None] * k[None, :]) % n).astype(np.float64) / n
    return np.cos(ang), np.sin(ang)


def _hi_lo(m):
    m = jnp.asarray(m, F32)
    hi = m.astype(BF16)
    return hi, (m - hi.astype(F32)).astype(BF16)


DFT_SUB = 8
DFT_COLS = 512


def _dft_kernel(*refs, n_planes, mode, twiddle):
    it = iter(refs)
    x_ref = next(it)
    mats = [(next(it)[...], next(it)[...]) for _ in range(n_planes)]
    if twiddle:
        tc_ref, ts_ref = next(it), next(it)
    o_ref = next(it)
    if mode != "flat":
        stage_ref = next(it)

    def dense(p, strided):
        stage_ref[p] = strided
        return stage_ref[p]

    def transform(planes):
        y = None
        for x, (mh, ml) in zip(planes, mats):
            xh, xl = _split2(x)
            part = _dot(mh, xh) + (_dot(mh, xl) + _dot(ml, xh))
            y = part if y is None else y + part
        half = y.shape[0] // 2
        return y[:half], y[half:]

    if mode == "flat":
        yr, yi = transform([x_ref[...]])
        o_ref[0] = yr
        o_ref[1] = yi
    elif mode == "a":
        for j in range(DFT_SUB):
            yr, yi = transform([dense(0, x_ref[:, j, :])])
            if twiddle:
                reps = yr.shape[1] // LANE
                c = jnp.concatenate([tc_ref[j]] * reps, axis=1)
                s = jnp.concatenate([ts_ref[j]] * reps, axis=1)
                yr, yi = yr * c + yi * s, yi * c - yr * s
            o_ref[0, j] = yr
            o_ref[1, j] = yi
    else:
        for j in range(DFT_SUB):
            yr, yi = transform([dense(0, x_ref[0, :, j, :]), dense(1, x_ref[1, :, j, :])])
            o_ref[0, :, j, :] = yr
            o_ref[1, :, j, :] = yi


def _dft_call(mode, x, mats, out_shape, grid, x_spec, o_spec, tw=None, tw_spec=None):
    args, specs = [x], [x_spec]
    for m in mats:
        mh, ml = _hi_lo(m)
        args += [mh, ml]
        specs += [pl.BlockSpec(m.shape, lambda *_: (0, 0))] * 2
    if tw is not None:
        args += list(tw)
        specs += [tw_spec] * 2
    scratch = []
    if mode != "flat":
        scratch = [pltpu.VMEM((len(mats), mats[0].shape[1], x_spec.block_shape[-1]), F32)]
    return pl.pallas_call(
        functools.partial(_dft_kernel, n_planes=len(mats), mode=mode, twiddle=tw is not None),
        out_shape=jax.ShapeDtypeStruct(out_shape, F32),
        grid=grid,
        in_specs=specs,
        out_specs=o_spec,
        scratch_shapes=scratch,
        compiler_params=_cparams(("arbitrary",) * len(grid)),
        name="dft_" + mode,
    )(*args)


def _fmix_kernel(pr_ref, pi_ref, x_ref, gt_ref, ch_ref, cl_ref, wo_ref, bo_ref, o_ref, *, groups):
    pr, pi = pr_ref[...], pi_ref[...]
    gd = pr.shape[1] // groups
    ch, cl = ch_ref[...], cl_ref[...]
    outs = []
    for g in range(groups):
        z = jnp.concatenate([pr[:, g * gd:(g + 1) * gd], pi[:, g * gd:(g + 1) * gd]], axis=1)
        zh, zl = _split2(z)
        outs.append(_dot(zh, ch) + (_dot(zl, ch) + _dot(zh, cl)))
    f = jnp.concatenate(outs, axis=1)
    o = _dot(f.astype(BF16), wo_ref[...]) + bo_ref[...]
    o_ref[...] = x_ref[...] + gt_ref[...] * o


def _fmix(p, x, mods, row_of_tile, wo, bo, *, tm):
    n, d = x.shape
    b, _, l, _ = p.shape
    tpb = l // tm
    gd = d // FNET_GROUPS
    c, s = _dft_tables(gd)
    ch, cl = _hi_lo(np.concatenate([c, s], axis=0) / math.sqrt(gd))
    return pl.pallas_call(
        functools.partial(_fmix_kernel, groups=FNET_GROUPS),
        out_shape=jax.ShapeDtypeStruct((n, d), F32),
        grid=(n // tm,),
        in_specs=[
            pl.BlockSpec((None, None, tm, d), lambda i: (i // tpb, 0, i % tpb, 0)),
            pl.BlockSpec((None, None, tm, d), lambda i: (i // tpb, 1, i % tpb, 0)),
            pl.BlockSpec((tm, d), lambda i: (i, 0)),
            pl.BlockSpec((None, 1, d), lambda i: (row_of_tile(i) * N_MOD + 5, 0, 0)),
            pl.BlockSpec((2 * gd, gd), lambda i: (0, 0)),
            pl.BlockSpec((2 * gd, gd), lambda i: (0, 0)),
            pl.BlockSpec((d, d), lambda i: (0, 0)),
            pl.BlockSpec((1, d), lambda i: (0, 0)),
        ],
        out_specs=pl.BlockSpec((tm, d), lambda i: (i, 0)),
        compiler_params=_cparams(("arbitrary",)),
        name="fourier_out",
    )(p, p, x, mods, ch, cl, wo, bo)


def _fourier_latent(xl, hl, bsz, seq, mods, wo, bo):
    n, d = xl.shape
    la = lb = int(round(math.sqrt(seq)))
    assert la * lb == seq and lb % DFT_SUB == 0 and la % DFT_SUB == 0
    td = min(DFT_COLS, d)
    s8 = DFT_SUB
    ca, sa = _dft_tables(la)
    m_a = np.concatenate([ca, -sa], axis=0) / math.sqrt(la)
    cb, sb = _dft_tables(lb)
    m_br = np.concatenate([cb, -sb], axis=0) / math.sqrt(lb)
    m_bi = np.concatenate([sb, cb], axis=0) / math.sqrt(lb)
    n2 = np.arange(lb, dtype=np.int64)[:, None]
    k1 = np.arange(la, dtype=np.int64)[None, :]
    ang = 2.0 * np.pi * ((n2 * k1) % seq).astype(np.float64) / seq
    twc = jnp.asarray(np.broadcast_to(np.cos(ang)[:, :, None], (lb, la, LANE)), F32)
    tws = jnp.asarray(np.broadcast_to(np.sin(ang)[:, :, None], (lb, la, LANE)), F32)
    grid = (bsz, lb // s8, d // td)
    a = _dft_call(
        "a", hl.reshape(bsz, la, lb, d), [m_a], (bsz, 2, lb, la, d), grid,
        pl.BlockSpec((None, la, s8, td), lambda b, g, c: (b, 0, g, c)),
        pl.BlockSpec((None, 2, s8, la, td), lambda b, g, c: (b, 0, g, 0, c)),
        tw=(twc, tws), tw_spec=pl.BlockSpec((s8, la, LANE), lambda b, g, c: (g, 0, 0)))
    p = _dft_call(
        "b", a, [m_br, m_bi], (bsz, 2, lb, la, d), (bsz, la // s8, d // td),
        pl.BlockSpec((None, 2, lb, s8, td), lambda b, g, c: (b, 0, 0, g, c)),
        pl.BlockSpec((None, 2, lb, s8, td), lambda b, g, c: (b, 0, 0, g, c)))
    p = p.reshape(bsz, 2, seq, d)
    return _fmix(p, xl, mods, lambda i: i // (seq // 256), wo, bo, tm=256)


def _fourier_ctx(xc, hc, bsz, lc, mods, wo, bo):
    n, d = xc.shape
    td = min(DFT_COLS, d)
    c, s = _dft_tables(lc)
    m = np.concatenate([c, -s], axis=0) / math.sqrt(lc)
    p = _dft_call(
        "flat", hc.reshape(bsz, lc, d), [m], (bsz, 2, lc, d), (bsz, d // td),
        pl.BlockSpec((None, lc, td), lambda b, c: (b, 0, c)),
        pl.BlockSpec((None, 2, lc, td), lambda b, c: (b, 0, 0, c)))
    return _fmix(p, xc, mods, lambda i: 2, wo, bo, tm=lc)


def _rwkv_proj_kernel(*refs, grid_mode, tpb):
    it = iter(refs)
    x_ref = next(it)
    if grid_mode:
        xp_ref, xn_ref = next(it), next(it)
    nw_ref, sh_ref, sc_ref, mu_ref = next(it), next(it), next(it), next(it)
    w1_ref, a1_ref, g1_ref = next(it), next(it), next(it)
    wr_ref, wk_ref, wv_ref = next(it), next(it), next(it)
    w2f_ref, w2b_ref, a2f_ref, a2b_ref, g2_ref = next(it), next(it), next(it), next(it), next(it)
    vec_ref, ones_ref = next(it), next(it)
    r_ref, k_ref, v_ref, g_ref, lwf_ref, lwb_ref, alf_ref, alb_ref, bon_ref = (next(it) for _ in range(9))
    hr_ref, hk_ref, hv_ref, tw_ref, ta_ref, tg_ref = (next(it) for _ in range(6))

    i = pl.program_id(0)
    j = pl.program_id(1)

    @pl.when(j == 0)
    def _():
        nw, sh, sc = nw_ref[...], sh_ref[...], sc_ref[...]
        h = _modulate(x_ref[...], nw, sh, sc)
        tm, d = h.shape
        q = d // 4
        row = lax.broadcasted_iota(jnp.int32, (tm, 1), 0)
        if grid_mode:
            tib = i % tpb
            up_ok = (tib != 0).astype(F32)
            dn_ok = (tib != tpb - 1).astype(F32)
            hp = _modulate(xp_ref[...], nw, sh, sc)[:, 2 * q:3 * q] * up_ok
            hn = _modulate(xn_ref[...], nw, sh, sc)[:, 3 * q:] * dn_ok
            col = row % GRID_W
            left = jnp.where(col != 0, pltpu.roll(h[:, :q], 1, 0), 0.0)
            right = jnp.where(col != GRID_W - 1, pltpu.roll(h[:, q:2 * q], tm - 1, 0), 0.0)
            up = jnp.concatenate([hp, h[:tm - GRID_W, 2 * q:3 * q]], axis=0)
            down = jnp.concatenate([h[GRID_W:, 3 * q:], hn], axis=0)
            hs = jnp.concatenate([left, right, up, down], axis=1)
        else:
            prev = jnp.where(row != 0, pltpu.roll(h, 1, 0), 0.0)
            nxt = jnp.where(row != tm - 1, pltpu.roll(h, tm - 1, 0), 0.0)
            hs = jnp.concatenate([prev[:, :q], nxt[:, q:2 * q], prev[:, 2 * q:3 * q], nxt[:, 3 * q:]], axis=1)
        dlt = hs - h
        mu = mu_ref[...]
        hr_ref[...] = (h + dlt * mu[0:1]).astype(BF16)
        hk_ref[...] = (h + dlt * mu[1:2]).astype(BF16)
        hv_ref[...] = (h + dlt * mu[2:3]).astype(BF16)
        tw_ref[...] = jnp.tanh(_dot((h + dlt * mu[3:4]).astype(BF16), w1_ref[...])).astype(BF16)
        ta_ref[...] = _dot((h + dlt * mu[4:5]).astype(BF16), a1_ref[...]).astype(BF16)
        tg_ref[...] = _sigmoid(_dot((h + dlt * mu[5:6]).astype(BF16), g1_ref[...])).astype(BF16)

    vec = vec_ref[...]
    w0f, w0b, a0f, a0b, ka, rk = (vec[n:n + 1] for n in range(6))
    r = _dot(hr_ref[...], wr_ref[...])
    k = _dot(hk_ref[...], wk_ref[...])
    v = _dot(hv_ref[...], wv_ref[...])
    tw = tw_ref[...]
    ta = ta_ref[...]
    lwf = -DECAY_SCALE * _sigmoid(w0f + _dot(tw, w2f_ref[...]))
    lwb = -DECAY_SCALE * _sigmoid(w0b + _dot(tw, w2b_ref[...]))
    alf = _sigmoid(a0f + _dot(ta, a2f_ref[...]))
    alb = _sigmoid(a0b + _dot(ta, a2b_ref[...]))
    kb = k * (1.0 + (0.5 * (alf + alb) - 1.0) * ka)
    r_ref[...] = r.astype(r_ref.dtype)
    k_ref[...] = k.astype(k_ref.dtype)
    v_ref[...] = v.astype(v_ref.dtype)
    g_ref[...] = _dot(tg_ref[...], g2_ref[...]).astype(g_ref.dtype)
    lwf_ref[...] = lwf
    lwb_ref[...] = lwb
    alf_ref[...] = alf
    alb_ref[...] = alb
    bon_ref[...] = (_dot_x2(r * kb * rk, ones_ref[...]) * v).astype(bon_ref.dtype)


def _rwkv_proj(x, mods, row_of_tile, nw, prm, *, tm, grid_mode, tpb):
    n, d = x.shape
    tn = 256
    nj = d // tn
    hw = GRID_W
    lr = prm["w1c"].shape[1]

    def mod_spec(k):
        return pl.BlockSpec((None, 1, d), lambda i, j: (row_of_tile(i) * N_MOD + 3 + k, 0, 0))

    args, specs = [x], [pl.BlockSpec((tm, d), lambda i, j: (i, 0))]
    if grid_mode:
        nb = n // hw
        r = tm // hw
        args += [x, x]
        specs += [pl.BlockSpec((hw, d), lambda i, j: (jnp.maximum(i * r - 1, 0), 0)),
                  pl.BlockSpec((hw, d), lambda i, j: (jnp.minimum((i + 1) * r, nb - 1), 0))]
    args += [nw, mods, mods, prm["mu"], prm["w1c"], prm["a1c"], prm["g1"], prm["wr"], prm["wk"], prm["wv"],
             prm["w2f"], prm["w2b"], prm["a2f"], prm["a2b"], prm["g2"], prm["vec"], prm["ones_head"]]
    full = lambda shape: pl.BlockSpec(shape, lambda i, j: (0,) * len(shape))
    coltile = lambda rows: pl.BlockSpec((rows, tn), lambda i, j: (0, j))
    specs += [full((1, d)), mod_spec(0), mod_spec(1), full((8, d)), full((d, lr)), full((d, lr)), full((d, lr)),
              coltile(d), coltile(d), coltile(d), coltile(lr), coltile(lr), coltile(lr), coltile(lr), coltile(lr),
              coltile(8), full((tn, tn))]
    lo, hi = jax.ShapeDtypeStruct((n, d), BF16), jax.ShapeDtypeStruct((n, d), F32)
    return pl.pallas_call(
        functools.partial(_rwkv_proj_kernel, grid_mode=grid_mode, tpb=tpb),
        out_shape=[lo, lo, lo, lo, hi, hi, hi, hi, lo],
        grid=(n // tm, nj),
        in_specs=specs,
        out_specs=[pl.BlockSpec((tm, tn), lambda i, j: (i, j))] * 9,
        scratch_shapes=[pltpu.VMEM((tm, d), BF16)] * 3 + [pltpu.VMEM((tm, lr), BF16)] * 3,
        compiler_params=_cparams(("arbitrary", "arbitrary")),
        name="rwkv_proj",
    )(*args)


def _wkv_kernel(r_ref, k_ref, v_ref, lw_ref, al_ref, kk_ref, ka_ref, s0_ref, bm_ref, ms_ref, mi_ref, tri_ref,
                ones_ref, eye_ref, y_ref, sf_ref, s_scr, *, rev, chunk, heads, nq, need_y, group):
    cc = pl.program_id(1)
    c = chunk
    wd = heads * RWKV_HEAD
    rn = heads * c

    @pl.when(cc == 0)
    def _():
        s_scr[...] = s0_ref[...]

    bm = bm_ref[...]
    bmf = bm.astype(F32)
    strict = ms_ref[...] > 0.0
    incl = mi_ref[...] > 0.0
    tri = tri_ref[...]
    ones = ones_ref[...]
    eye = eye_ref[...]

    def stack(x):
        return jnp.concatenate([x.astype(BF16)] * heads, axis=0) * bm

    if not need_y:
        y_ref[...] = jnp.zeros_like(y_ref)

    def block(q):
        sl = slice(q * wd, (q + 1) * wd)
        r, k, v = r_ref[:, sl].astype(F32), k_ref[:, sl].astype(F32), v_ref[:, sl].astype(F32)
        lw, al = lw_ref[:, sl], al_ref[:, sl]
        kk0 = k * kk_ref[:, sl]
        kk = kk0 * lax.rsqrt(jnp.maximum(_dot_x2(kk0 * kk0, ones), KK_EPS))
        b = kk * al
        kd = k * (1.0 + (al - 1.0) * ka_ref[:, sl])
        l3 = _split3(lw)
        lg = _dot(tri, l3[0]) + (_dot(tri, l3[1]) + _dot(tri, l3[2]))
        gc = lg[0:1] if rev else lg[c - 1:c]
        gin = jnp.exp(-lg)
        gout = jnp.exp(gc - lg)
        a_n = -kk * jnp.exp(lg - lw)
        r_n = r * jnp.exp(lg)
        a_s = stack(a_n)
        v_s = stack(v)
        bk = jnp.concatenate([stack(b * gin), stack(kd * gin)], axis=0)
        bo_n = (b * gout).astype(BF16)
        ko_n = (kd * gout).astype(BF16)
        yield

        lhs = jnp.concatenate([a_n, r_n], axis=0) if need_y else a_n
        gm = _dot_nt(lhs.astype(BF16), bk)
        aab = jnp.where(strict, gm[:c, :rn], 0.0)
        aak = jnp.where(strict, gm[:c, rn:], 0.0).astype(BF16)
        if need_y:
            arb = jnp.where(incl, gm[c:, :rn], 0.0).astype(BF16)
            ark = jnp.where(incl, gm[c:, rn:], 0.0).astype(BF16)
        yield
        wy = _dot(jnp.concatenate([aak, ark], axis=0) if need_y else aak, v_s)
        w0 = wy[:c]

        pm = eye + aab
        xb = aab.astype(BF16)
        x = _dot(xb, stack(xb))
        yield
        steps = int(math.log2(c))
        for _ in range(steps - 2):
            xb = x.astype(BF16)
            res = _dot(xb, jnp.concatenate([stack(xb), stack(pm)], axis=1))
            x = res[:, :rn]
            pm = pm + res[:, rn:]
            yield
        t = (pm + _dot(x.astype(BF16), stack(pm))).astype(BF16)
        yield
        if need_y:
            t = jnp.concatenate([t, _dot(arb, stack(t)).astype(BF16)], axis=0)
            yield

        tv = _dot(t, jnp.concatenate([a_s, stack(w0)], axis=1))
        av = tv[:c]
        yield
        s = s_scr[q]
        sb = s.astype(BF16)
        if need_y:
            ry = tv[c:]
            rh = r_n + ry[:, :wd]
            yk = wy[c:]
        pd = _dot(av.T.astype(BF16), bo_n)
        dt = (pd[wd:] + _dot(v.T.astype(BF16), ko_n)) * bmf
        pt = (pd[:wd] * bmf).astype(BF16)
        yield
        if need_y:
            y_ref[:, sl] = _dot_nt(rh.astype(BF16), sb) + ry[:, wd:] + yk
        s_scr[q] = s * jnp.exp(gc) + _dot(sb, pt) + dt

    for q0 in range(0, nq, group):
        live = [block(q) for q in range(q0, min(q0 + group, nq))]
        while live:
            nxt = []
            for gen in live:
                try:
                    next(gen)
                    nxt.append(gen)
                except StopIteration:
                    pass
            live = nxt

    @pl.when(cc == pl.num_programs(1) - 1)
    def _():
        sf_ref[...] = s_scr[...]


def _wkv_consts(rev):
    c, g = WKV_CHUNK, WKV_HEADS_PER_BLOCK
    wd, rn = g * RWKV_HEAD, g * c
    assert c == RWKV_HEAD, "one (rn, wd) mask serves both (head, time) and (head, channel) columns"
    hrow = np.arange(rn) // c
    hlane = np.arange(wd) // RWKV_HEAD
    bm = (hrow[:, None] == hlane[None, :]).astype(np.float32)
    t = np.arange(c)[:, None]
    src = (np.arange(rn) % c)[None, :]
    if rev:
        strict, incl = src > t, src >= t
        tri = np.triu(np.ones((c, c), np.float32))
    else:
        strict, incl = src < t, src <= t
        tri = np.tril(np.ones((c, c), np.float32))
    ones = (hlane[:, None] == hlane[None, :]).astype(np.float32)
    return (jnp.asarray(bm, BF16), jnp.asarray(strict, F32), jnp.asarray(incl, F32), jnp.asarray(tri, BF16),
            jnp.asarray(ones, BF16), jnp.asarray(src == t, F32))


def _wkv(r, k, v, lw, al, kkp, kap, s0, bsz, *, rev, need_y):
    n, d = r.shape
    c, g = WKV_CHUNK, WKV_HEADS_PER_BLOCK
    wd, rn = g * RWKV_HEAD, g * c
    nq = d // wd
    nc = n // bsz // c
    if rev:
        tok = lambda b, cc: (b * nc + (nc - 1 - cc), 0)
    else:
        tok = lambda b, cc: (b * nc + cc, 0)
    tspec = pl.BlockSpec((c, d), tok)
    full = lambda shape: pl.BlockSpec(shape, lambda b, cc: (0,) * len(shape))
    sspec = pl.BlockSpec((None, nq, wd, wd), lambda b, cc: (b, 0, 0, 0))
    consts = _wkv_consts(rev)
    y, sf = pl.pallas_call(
        functools.partial(_wkv_kernel, rev=rev, chunk=c, heads=g, nq=nq, need_y=need_y, group=WKV_GROUP),
        out_shape=[jax.ShapeDtypeStruct((n, d), F32), jax.ShapeDtypeStruct((bsz, nq, wd, wd), F32)],
        grid=(bsz, nc),
        in_specs=[tspec] * 5 + [full((1, d)), full((1, d)), sspec,
                                full((rn, wd)), full((c, rn)), full((c, rn)), full((c, c)), full((wd, wd)),
                                full((c, rn))],
        out_specs=[tspec, sspec],
        scratch_shapes=[pltpu.VMEM((nq, wd, wd), F32)],
        compiler_params=_cparams(("arbitrary", "arbitrary")),
        name="wkv_scan",
    )(r, k, v, lw, al, kkp, kap, s0, *consts)
    return y, sf


def _rwkv_out_kernel(yf_ref, yb_ref, bon_ref, g_ref, x_ref, gt_ref, lnw_ref, lnb_ref, ones_ref, wo_ref, o_ref,
                     *, wd):
    y = yf_ref[...] + yb_ref[...]
    ones = ones_ref[...]
    d = y.shape[1]
    inv = 1.0 / RWKV_HEAD

    def headsum(z):
        return jnp.concatenate([_dot_x2(z[:, q * wd:(q + 1) * wd], ones) for q in range(d // wd)], axis=1)

    dev = y - headsum(y) * inv
    var = headsum(dev * dev) * inv
    yn = dev * lax.rsqrt(var + GN_EPS) * lnw_ref[...] + lnb_ref[...]
    o = (yn + bon_ref[...].astype(F32)) * g_ref[...].astype(F32)
    o_ref[...] = x_ref[...] + gt_ref[...] * _dot(o.astype(BF16), wo_ref[...])


def _rwkv_out(yf, yb, bon, g, x, mods, row_of_tile, lnw, lnb, wo, *, tm):
    n, d = x.shape
    wd = WKV_HEADS_PER_BLOCK * RWKV_HEAD
    hl = np.arange(wd) // RWKV_HEAD
    ones = jnp.asarray(hl[:, None] == hl[None, :], BF16)
    tspec = pl.BlockSpec((tm, d), lambda i: (i, 0))
    full = lambda shape: pl.BlockSpec(shape, lambda i: (0,) * len(shape))
    return pl.pallas_call(
        functools.partial(_rwkv_out_kernel, wd=wd),
        out_shape=jax.ShapeDtypeStruct((n, d), F32),
        grid=(n // tm,),
        in_specs=[tspec] * 5 + [pl.BlockSpec((None, 1, d), lambda i: (row_of_tile(i) * N_MOD + 5, 0, 0)),
                                full((1, d)), full((1, d)), full((wd, wd)), full((d, d))],
        out_specs=tspec,
        compiler_params=_cparams(("arbitrary",)),
        name="rwkv_out",
    )(yf, yb, bon, g, x, mods, lnw, lnb, ones, wo)


def _rwkv_params(j, d, rwkv_mu, rwkv_w_rkv, rwkv_w0, rwkv_w1, rwkv_w2, rwkv_a0, rwkv_a1, rwkv_a2, rwkv_g1,
                 rwkv_g2, rwkv_k_a, rwkv_r_k):
    lr = rwkv_g1.shape[2]
    rank = rwkv_w1.shape[3]
    assert 2 * rank <= lr

    def first(w):
        return jnp.pad(jnp.concatenate([w[0], w[1]], axis=1), ((0, 0), (0, lr - 2 * rank))).astype(BF16)

    def second(w, z):
        return jnp.pad(w, ((z * rank, lr - (z + 1) * rank), (0, 0))).astype(BF16)

    vec = jnp.stack([rwkv_w0[j, 0], rwkv_w0[j, 1], rwkv_a0[j, 0], rwkv_a0[j, 1], rwkv_k_a[j],
                     rwkv_r_k[j].reshape(d), jnp.zeros((d,), F32), jnp.zeros((d,), F32)])
    hl = np.arange(256) // RWKV_HEAD
    return dict(
        mu=jnp.pad(rwkv_mu[j], ((0, 2), (0, 0))),
        w1c=first(rwkv_w1[j]), a1c=first(rwkv_a1[j]), g1=rwkv_g1[j].astype(BF16),
        wr=rwkv_w_rkv[j, 0].astype(BF16), wk=rwkv_w_rkv[j, 1].astype(BF16), wv=rwkv_w_rkv[j, 2].astype(BF16),
        w2f=second(rwkv_w2[j, 0], 0), w2b=second(rwkv_w2[j, 1], 1),
        a2f=second(rwkv_a2[j, 0], 0), a2b=second(rwkv_a2[j, 1], 1),
        g2=rwkv_g2[j].astype(BF16), vec=vec,
        ones_head=jnp.asarray(hl[:, None] == hl[None, :], BF16),
    )


def kernel(x, c, ctx, c_ctx, mod_w, mod_b, norm_w, ffn_w13, ffn_w2, fnet_w_o, fnet_b_o, rwkv_mu, rwkv_w_rkv,
           rwkv_w0, rwkv_w1, rwkv_w2, rwkv_a0, rwkv_a1, rwkv_a2, rwkv_g1, rwkv_g2, rwkv_k_k, rwkv_k_a, rwkv_r_k,
           rwkv_ln_w, rwkv_ln_b, rwkv_w_o, final_norm_w):
    bsz, seq, d = x.shape
    lc = ctx.shape[1]
    depth = mod_w.shape[0]
    assert depth == 2 and bsz == 2, "layer schedule below is written for the two-layer, batch-2 trunk"
    tm = 512
    assert seq % tm == 0 and seq % GRID_W == 0

    c8 = jnp.concatenate([c, c_ctx[None], jnp.zeros((8 - bsz - 1, d), F32)], axis=0)
    mods_all = _adaln(c8, mod_w, mod_b)
    f2 = ffn_w13.shape[-1]
    w13 = ffn_w13.astype(BF16).reshape(2 * depth, d, f2)
    w2 = ffn_w2.astype(BF16).reshape(2 * depth, f2 // 2, d)
    fw = final_norm_w.reshape(1, d)

    xl = x.reshape(bsz * seq, d)
    xc = ctx.reshape(bsz * lc, d)
    lat_row = lambda i: i // (seq // tm)
    ctx_row = lambda i: 2

    mods = mods_all[0, :3].reshape(3 * N_MOD, 1, d)
    nw = norm_w[0].reshape(3, 1, d)
    xl, hl = _ffn(xl, mods, lat_row, 0, nw[0], w13, w2, 0, fw, tm=tm, next_nw=nw[1])
    xc, hc = _ffn(xc, mods, ctx_row, 0, nw[0], w13, w2, 0, fw, tm=lc, next_nw=nw[1])
    wo = fnet_w_o[0].astype(BF16)
    bo = fnet_b_o[0].reshape(1, d)
    xl = _fourier_latent(xl, hl, bsz, seq, mods, wo, bo)
    xc = _fourier_ctx(xc, hc, bsz, lc, mods, wo, bo)
    xl = _ffn(xl, mods, lat_row, 2, nw[2], w13, w2, 1, fw, tm=tm)
    xc = _ffn(xc, mods, ctx_row, 2, nw[2], w13, w2, 1, fw, tm=lc)

    mods = mods_all[1, :3].reshape(3 * N_MOD, 1, d)
    nw = norm_w[1].reshape(3, 1, d)
    xl = _ffn(xl, mods, lat_row, 0, nw[0], w13, w2, 2, fw, tm=tm)
    xc = _ffn(xc, mods, ctx_row, 0, nw[0], w13, w2, 2, fw, tm=lc)
    prm = _rwkv_params(0, d, rwkv_mu, rwkv_w_rkv, rwkv_w0, rwkv_w1, rwkv_w2, rwkv_a0, rwkv_a1, rwkv_a2, rwkv_g1,
                       rwkv_g2, rwkv_k_a, rwkv_r_k)
    r_c, k_c, v_c, _, lwf_c, lwb_c, alf_c, alb_c, _ = _rwkv_proj(
        xc, mods, ctx_row, nw[1], prm, tm=lc, grid_mode=False, tpb=1)
    r_l, k_l, v_l, g_l, lwf_l, lwb_l, alf_l, alb_l, bon_l = _rwkv_proj(
        xl, mods, lat_row, nw[1], prm, tm=tm, grid_mode=True, tpb=seq // tm)
    kkp = rwkv_k_k[0].reshape(1, d)
    kap = rwkv_k_a[0].reshape(1, d)
    wd = WKV_HEADS_PER_BLOCK * RWKV_HEAD
    s0 = jnp.zeros((bsz, d // wd, wd, wd), F32)
    _, s_f = _wkv(r_c, k_c, v_c, lwf_c, alf_c, kkp, kap, s0, bsz, rev=False, need_y=False)
    _, s_b = _wkv(r_c, k_c, v_c, lwb_c, alb_c, kkp, kap, s0, bsz, rev=True, need_y=False)
    yf, _ = _wkv(r_l, k_l, v_l, lwf_l, alf_l, kkp, kap, s_f, bsz, rev=False, need_y=True)
    yb, _ = _wkv(r_l, k_l, v_l, lwb_l, alb_l, kkp, kap, s_b, bsz, rev=True, need_y=True)
    xl = _rwkv_out(yf, yb, bon_l, g_l, xl, mods, lambda i: i // (seq // 256), rwkv_ln_w[0].reshape(1, d),
                   rwkv_ln_b[0].reshape(1, d), rwkv_w_o[0].astype(BF16), tm=256)
    xl = _ffn(xl, mods, lat_row, 2, nw[2], w13, w2, 3, fw, tm=tm, final_norm=True)
    return xl.reshape(bsz, seq, d)
```

```python
import functools
import math

import numpy as np
import jax
import jax.numpy as jnp
from jax import lax
from jax.experimental import pallas as pl
from jax.experimental.pallas import tpu as pltpu

F32 = jnp.float32
BF16 = jnp.bfloat16

NORM_EPS = 1e-6
GN_EPS = 64e-5
KK_EPS = 1e-24
GRID_W = 64
FNET_GROUPS = 8
RWKV_HEAD = 64
N_MOD = 9
DECAY_SCALE = math.exp(-0.5)

LANE = 128
WKV_CHUNK = 64
WKV_HEADS_PER_BLOCK = 2
WKV_GROUP = 16
VMEM_LIMIT = 56 * 1024 * 1024


def _cparams(sem):
    return pltpu.CompilerParams(dimension_semantics=sem, vmem_limit_bytes=VMEM_LIMIT)


def _sigmoid(x):
    return 1.0 / (1.0 + jnp.exp(-x))


def _modulate(x, nw, shift, scale):
    y = x * lax.rsqrt(jnp.mean(x * x, axis=-1, keepdims=True) + NORM_EPS)
    return (y * nw) * (1.0 + scale) + shift


def _dot(a, b):
    return jnp.dot(a, b, preferred_element_type=F32)


def _dot_nt(a, b):
    return lax.dot_general(a, b, (((1,), (1,)), ((), ())), preferred_element_type=F32)


def _split2(x):
    hi = x.astype(BF16)
    lo = (x - hi.astype(F32)).astype(BF16)
    return hi, lo


def _split3(x):
    hi = x.astype(BF16)
    r1 = x - hi.astype(F32)
    mid = r1.astype(BF16)
    lo = (r1 - mid.astype(F32)).astype(BF16)
    return hi, mid, lo


def _dot_x2(x, w_bf16):
    hi, lo = _split2(x)
    return _dot(hi, w_bf16) + _dot(lo, w_bf16)


def _adaln_kernel(c_ref, w_ref, b_ref, o_ref):
    c = c_ref[...]
    s = c * _sigmoid(c)
    o_ref[...] = _dot(s.astype(BF16), w_ref[...].astype(BF16)) + b_ref[...]


def _adaln(c8, mod_w, mod_b):
    depth, d, nd = mod_w.shape
    tn = 1024 if nd % 1024 == 0 else 512
    return pl.pallas_call(
        _adaln_kernel,
        out_shape=jax.ShapeDtypeStruct((depth, 8, nd), F32),
        grid=(depth, nd // tn),
        in_specs=[
            pl.BlockSpec((8, d), lambda l, j: (0, 0)),
            pl.BlockSpec((None, d, tn), lambda l, j: (l, 0, j)),
            pl.BlockSpec((None, 1, tn), lambda l, j: (l, 0, j)),
        ],
        out_specs=pl.BlockSpec((None, 8, tn), lambda l, j: (l, 0, j)),
        compiler_params=_cparams(("arbitrary", "arbitrary")),
        name="adaln_linear",
    )(c8, mod_w, mod_b.reshape(depth, 1, nd))


def _ffn_kernel(*refs, final_norm, emit_h):
    x_ref, sh_ref, sc_ref, gt_ref, nw_ref, w1g_ref, w1u_ref, w2_ref, fw_ref = refs[:9]
    if emit_h:
        nw2_ref, sh2_ref, sc2_ref, o_ref, h2_ref, h_ref, acc_ref = refs[9:]
    else:
        o_ref, h_ref, acc_ref = refs[9:]
    j = pl.program_id(1)

    @pl.when(j == 0)
    def _():
        h_ref[...] = _modulate(x_ref[...], nw_ref[...], sh_ref[...], sc_ref[...]).astype(BF16)
        acc_ref[...] = jnp.zeros_like(acc_ref)

    h = h_ref[...]
    g = _dot(h, w1g_ref[...])
    u = _dot(h, w1u_ref[...])
    a = (g * _sigmoid(g)) * u
    acc_ref[...] += _dot(a.astype(BF16), w2_ref[...])

    @pl.when(j == pl.num_programs(1) - 1)
    def _():
        y = x_ref[...] + (0.5 * gt_ref[...]) * acc_ref[...]
        if final_norm:
            y = y * lax.rsqrt(jnp.mean(y * y, axis=-1, keepdims=True) + NORM_EPS) * fw_ref[...]
        o_ref[...] = y
        if emit_h:
            h2_ref[...] = _modulate(y, nw2_ref[...], sh2_ref[...], sc2_ref[...])


def _ffn(x, mods, row_of_tile, sub, nw, w13, w2, wi, fw, *, tm, final_norm=False, next_nw=None):
    n, d = x.shape
    f = w2.shape[1]
    tf = 512
    nf = f // tf
    emit_h = next_nw is not None

    def mod_spec(k):
        return pl.BlockSpec((None, 1, d), lambda i, j: (row_of_tile(i) * N_MOD + 3 * sub + k, 0, 0))

    vec_spec = pl.BlockSpec((1, d), lambda i, j: (0, 0))
    tile_spec = pl.BlockSpec((tm, d), lambda i, j: (i, 0))
    out = jax.ShapeDtypeStruct((n, d), F32)
    args = [x, mods, mods, mods, nw, w13, w13, w2, fw]
    specs = [tile_spec, mod_spec(0), mod_spec(1), mod_spec(2), vec_spec,
             pl.BlockSpec((None, d, tf), lambda i, j: (wi, 0, j)),
             pl.BlockSpec((None, d, tf), lambda i, j: (wi, 0, nf + j)),
             pl.BlockSpec((None, tf, d), lambda i, j: (wi, j, 0)),
             vec_spec]
    if emit_h:
        args += [next_nw, mods, mods]
        specs += [vec_spec, mod_spec(3), mod_spec(4)]
    return pl.pallas_call(
        functools.partial(_ffn_kernel, final_norm=final_norm, emit_h=emit_h),
        out_shape=[out, out] if emit_h else out,
        grid=(n // tm, nf),
        in_specs=specs,
        out_specs=[tile_spec, tile_spec] if emit_h else tile_spec,
        scratch_shapes=[pltpu.VMEM((tm, d), BF16), pltpu.VMEM((tm, d), F32)],
        compiler_params=_cparams(("arbitrary", "arbitrary")),
        name="ffn_swiglu",
    )(*args)


def _dft_tables(n):
    k = np.arange(n, dtype=np.int64)
    ang = 2.0 * np.pi * ((k[:, None] * k[None, :]) % n).astype(np.float64) / n
    return np.cos(ang), np.sin(ang)


def _hi_lo(m):
    m = jnp.asarray(m, F32)
    hi = m.astype(BF16)
    return hi, (m - hi.astype(F32)).astype(BF16)


DFT_SUB = 8
DFT_COLS = 512


def _dft_kernel(*refs, n_planes, mode, twiddle):
    it = iter(refs)
    x_ref = next(it)
    mats = [(next(it)[...], next(it)[...]) for _ in range(n_planes)]
    if twiddle:
        tc_ref, ts_ref = next(it), next(it)
    o_ref = next(it)
    if mode != "flat":
        stage_ref = next(it)

    def dense(p, strided):
        stage_ref[p] = strided
        return stage_ref[p]

    def transform(planes):
        y = None
        for x, (mh, ml) in zip(planes, mats):
            xh, xl = _split2(x)
            part = _dot(mh, xh) + (_dot(mh, xl) + _dot(ml, xh))
            y = part if y is None else y + part
        half = y.shape[0] // 2
        return y[:half], y[half:]

    if mode == "flat":
        yr, yi = transform([x_ref[...]])
        o_ref[0] = yr
        o_ref[1] = yi
    elif mode == "a":
        for j in range(DFT_SUB):
            yr, yi = transform([dense(0, x_ref[:, j, :])])
            if twiddle:
                reps = yr.shape[1] // LANE
                c = jnp.concatenate([tc_ref[j]] * reps, axis=1)
                s = jnp.concatenate([ts_ref[j]] * reps, axis=1)
                yr, yi = yr * c + yi * s, yi * c - yr * s
            o_ref[0, j] = yr
            o_ref[1, j] = yi
    else:
        for j in range(DFT_SUB):
            yr, yi = transform([dense(0, x_ref[0, :, j, :]), dense(1, x_ref[1, :, j, :])])
            o_ref[0, :, j, :] = yr
            o_ref[1, :, j, :] = yi


def _dft_call(mode, x, mats, out_shape, grid, x_spec, o_spec, tw=None, tw_spec=None):
    args, specs = [x], [x_spec]
    for m in mats:
        mh, ml = _hi_lo(m)
        args += [mh, ml]
        specs += [pl.BlockSpec(m.shape, lambda *_: (0, 0))] * 2
    if tw is not None:
        args += list(tw)
        specs += [tw_spec] * 2
    scratch = []
    if mode != "flat":
        scratch = [pltpu.VMEM((len(mats), mats[0].shape[1], x_spec.block_shape[-1]), F32)]
    return pl.pallas_call(
        functools.partial(_dft_kernel, n_planes=len(mats), mode=mode, twiddle=tw is not None),
        out_shape=jax.ShapeDtypeStruct(out_shape, F32),
        grid=grid,
        in_specs=specs,
        out_specs=o_spec,
        scratch_shapes=scratch,
        compiler_params=_cparams(("arbitrary",) * len(grid)),
        name="dft_" + mode,
    )(*args)


def _fmix_kernel(pr_ref, pi_ref, x_ref, gt_ref, ch_ref, cl_ref, wo_ref, bo_ref, o_ref, *, groups):
    pr, pi = pr_ref[...], pi_ref[...]
    gd = pr.shape[1] // groups
    ch, cl = ch_ref[...], cl_ref[...]
    outs = []
    for g in range(groups):
        z = jnp.concatenate([pr[:, g * gd:(g + 1) * gd], pi[:, g * gd:(g + 1) * gd]], axis=1)
        zh, zl = _split2(z)
        outs.append(_dot(zh, ch) + (_dot(zl, ch) + _dot(zh, cl)))
    f = jnp.concatenate(outs, axis=1)
    o = _dot(f.astype(BF16), wo_ref[...]) + bo_ref[...]
    o_ref[...] = x_ref[...] + gt_ref[...] * o


def _fmix(p, x, mods, row_of_tile, wo, bo, *, tm):
    n, d = x.shape
    b, _, l, _ = p.shape
    tpb = l // tm
    gd = d // FNET_GROUPS
    c, s = _dft_tables(gd)
    ch, cl = _hi_lo(np.concatenate([c, s], axis=0) / math.sqrt(gd))
    return pl.pallas_call(
        functools.partial(_fmix_kernel, groups=FNET_GROUPS),
        out_shape=jax.ShapeDtypeStruct((n, d), F32),
        grid=(n // tm,),
        in_specs=[
            pl.BlockSpec((None, None, tm, d), lambda i: (i // tpb, 0, i % tpb, 0)),
            pl.BlockSpec((None, None, tm, d), lambda i: (i // tpb, 1, i % tpb, 0)),
            pl.BlockSpec((tm, d), lambda i: (i, 0)),
            pl.BlockSpec((None, 1, d), lambda i: (row_of_tile(i) * N_MOD + 5, 0, 0)),
            pl.BlockSpec((2 * gd, gd), lambda i: (0, 0)),
            pl.BlockSpec((2 * gd, gd), lambda i: (0, 0)),
            pl.BlockSpec((d, d), lambda i: (0, 0)),
            pl.BlockSpec((1, d), lambda i: (0, 0)),
        ],
        out_specs=pl.BlockSpec((tm, d), lambda i: (i, 0)),
        compiler_params=_cparams(("arbitrary",)),
        name="fourier_out",
    )(p, p, x, mods, ch, cl, wo, bo)


def _fourier_latent(xl, hl, bsz, seq, mods, wo, bo):
    n, d = xl.shape
    la = lb = int(round(math.sqrt(seq)))
    assert la * lb == seq and lb % DFT_SUB == 0 and la % DFT_SUB == 0
    td = min(DFT_COLS, d)
    s8 = DFT_SUB
    ca, sa = _dft_tables(la)
    m_a = np.concatenate([ca, -sa], axis=0) / math.sqrt(la)
    cb, sb = _dft_tables(lb)
    m_br = np.concatenate([cb, -sb], axis=0) / math.sqrt(lb)
    m_bi = np.concatenate([sb, cb], axis=0) / math.sqrt(lb)
    n2 = np.arange(lb, dtype=np.int64)[:, None]
    k1 = np.arange(la, dtype=np.int64)[None, :]
    ang = 2.0 * np.pi * ((n2 * k1) % seq).astype(np.float64) / seq
    twc = jnp.asarray(np.broadcast_to(np.cos(ang)[:, :, None], (lb, la, LANE)), F32)
    tws = jnp.asarray(np.broadcast_to(np.sin(ang)[:, :, None], (lb, la, LANE)), F32)
    grid = (bsz, lb // s8, d // td)
    a = _dft_call(
        "a", hl.reshape(bsz, la, lb, d), [m_a], (bsz, 2, lb, la, d), grid,
        pl.BlockSpec((None, la, s8, td), lambda b, g, c: (b, 0, g, c)),
        pl.BlockSpec((None, 2, s8, la, td), lambda b, g, c: (b, 0, g, 0, c)),
        tw=(twc, tws), tw_spec=pl.BlockSpec((s8, la, LANE), lambda b, g, c: (g, 0, 0)))
    p = _dft_call(
        "b", a, [m_br, m_bi], (bsz, 2, lb, la, d), (bsz, la // s8, d // td),
        pl.BlockSpec((None, 2, lb, s8, td), lambda b, g, c: (b, 0, 0, g, c)),
        pl.BlockSpec((None, 2, lb, s8, td), lambda b, g, c: (b, 0, 0, g, c)))
    p = p.reshape(bsz, 2, seq, d)
    return _fmix(p, xl, mods, lambda i: i // (seq // 256), wo, bo, tm=256)


def _fourier_ctx(xc, hc, bsz, lc, mods, wo, bo):
    n, d = xc.shape
    td = min(DFT_COLS, d)
    c, s = _dft_tables(lc)
    m = np.concatenate([c, -s], axis=0) / math.sqrt(lc)
    p = _dft_call(
        "flat", hc.reshape(bsz, lc, d), [m], (bsz, 2, lc, d), (bsz, d // td),
        pl.BlockSpec((None, lc, td), lambda b, c: (b, 0, c)),
        pl.BlockSpec((None, 2, lc, td), lambda b, c: (b, 0, 0, c)))
    return _fmix(p, xc, mods, lambda i: 2, wo, bo, tm=lc)


def _rwkv_proj_kernel(*refs, grid_mode, tpb):
    it = iter(refs)
    x_ref = next(it)
    if grid_mode:
        xp_ref, xn_ref = next(it), next(it)
    nw_ref, sh_ref, sc_ref, mu_ref = next(it), next(it), next(it), next(it)
    w1_ref, a1_ref, g1_ref = next(it), next(it), next(it)
    wr_ref, wk_ref, wv_ref = next(it), next(it), next(it)
    w2f_ref, w2b_ref, a2f_ref, a2b_ref, g2_ref = next(it), next(it), next(it), next(it), next(it)
    vec_ref, ones_ref = next(it), next(it)
    r_ref, k_ref, v_ref, g_ref, lwf_ref, lwb_ref, alf_ref, alb_ref, bon_ref = (next(it) for _ in range(9))
    hr_ref, hk_ref, hv_ref, tw_ref, ta_ref, tg_ref = (next(it) for _ in range(6))

    i = pl.program_id(0)
    j = pl.program_id(1)

    @pl.when(j == 0)
    def _():
        nw, sh, sc = nw_ref[...], sh_ref[...], sc_ref[...]
        h = _modulate(x_ref[...], nw, sh, sc)
        tm, d = h.shape
        q = d // 4
        row = lax.broadcasted_iota(jnp.int32, (tm, 1), 0)
        if grid_mode:
            tib = i % tpb
            up_ok = (tib != 0).astype(F32)
            dn_ok = (tib != tpb - 1).astype(F32)
            hp = _modulate(xp_ref[...], nw, sh, sc)[:, 2 * q:3 * q] * up_ok
            hn = _modulate(xn_ref[...], nw, sh, sc)[:, 3 * q:] * dn_ok
            col = row % GRID_W
            left = jnp.where(col != 0, pltpu.roll(h[:, :q], 1, 0), 0.0)
            right = jnp.where(col != GRID_W - 1, pltpu.roll(h[:, q:2 * q], tm - 1, 0), 0.0)
            up = jnp.concatenate([hp, h[:tm - GRID_W, 2 * q:3 * q]], axis=0)
            down = jnp.concatenate([h[GRID_W:, 3 * q:], hn], axis=0)
            hs = jnp.concatenate([left, right, up, down], axis=1)
        else:
            prev = jnp.where(row != 0, pltpu.roll(h, 1, 0), 0.0)
            nxt = jnp.where(row != tm - 1, pltpu.roll(h, tm - 1, 0), 0.0)
            hs = jnp.concatenate([prev[:, :q], nxt[:, q:2 * q], prev[:, 2 * q:3 * q], nxt[:, 3 * q:]], axis=1)
        dlt = hs - h
        mu = mu_ref[...]
        hr_ref[...] = (h + dlt * mu[0:1]).astype(BF16)
        hk_ref[...] = (h + dlt * mu[1:2]).astype(BF16)
        hv_ref[...] = (h + dlt * mu[2:3]).astype(BF16)
        tw_ref[...] = jnp.tanh(_dot((h + dlt * mu[3:4]).astype(BF16), w1_ref[...])).astype(BF16)
        ta_ref[...] = _dot((h + dlt * mu[4:5]).astype(BF16), a1_ref[...]).astype(BF16)
        tg_ref[...] = _sigmoid(_dot((h + dlt * mu[5:6]).astype(BF16), g1_ref[...])).astype(BF16)

    vec = vec_ref[...]
    w0f, w0b, a0f, a0b, ka, rk = (vec[n:n + 1] for n in range(6))
    r = _dot(hr_ref[...], wr_ref[...])
    k = _dot(hk_ref[...], wk_ref[...])
    v = _dot(hv_ref[...], wv_ref[...])
    tw = tw_ref[...]
    ta = ta_ref[...]
    lwf = -DECAY_SCALE * _sigmoid(w0f + _dot(tw, w2f_ref[...]))
    lwb = -DECAY_SCALE * _sigmoid(w0b + _dot(tw, w2b_ref[...]))
    alf = _sigmoid(a0f + _dot(ta, a2f_ref[...]))
    alb = _sigmoid(a0b + _dot(ta, a2b_ref[...]))
    kb = k * (1.0 + (0.5 * (alf + alb) - 1.0) * ka)
    r_ref[...] = r.astype(r_ref.dtype)
    k_ref[...] = k.astype(k_ref.dtype)
    v_ref[...] = v.astype(v_ref.dtype)
    g_ref[...] = _dot(tg_ref[...], g2_ref[...]).astype(g_ref.dtype)
    lwf_ref[...] = lwf
    lwb_ref[...] = lwb
    alf_ref[...] = alf
    alb_ref[...] = alb
    bon_ref[...] = (_dot_x2(r * kb * rk, ones_ref[...]) * v).astype(bon_ref.dtype)


def _rwkv_proj(x, mods, row_of_tile, nw, prm, *, tm, grid_mode, tpb):
    n, d = x.shape
    tn = 256
    nj = d // tn
    hw = GRID_W
    lr = prm["w1c"].shape[1]

    def mod_spec(k):
        return pl.BlockSpec((None, 1, d), lambda i, j: (row_of_tile(i) * N_MOD + 3 + k, 0, 0))

    args, specs = [x], [pl.BlockSpec((tm, d), lambda i, j: (i, 0))]
    if grid_mode:
        nb = n // hw
        r = tm // hw
        args += [x, x]
        specs += [pl.BlockSpec((hw, d), lambda i, j: (jnp.maximum(i * r - 1, 0), 0)),
                  pl.BlockSpec((hw, d), lambda i, j: (jnp.minimum((i + 1) * r, nb - 1), 0))]
    args += [nw, mods, mods, prm["mu"], prm["w1c"], prm["a1c"], prm["g1"], prm["wr"], prm["wk"], prm["wv"],
             prm["w2f"], prm["w2b"], prm["a2f"], prm["a2b"], prm["g2"], prm["vec"], prm["ones_head"]]
    full = lambda shape: pl.BlockSpec(shape, lambda i, j: (0,) * len(shape))
    coltile = lambda rows: pl.BlockSpec((rows, tn), lambda i, j: (0, j))
    specs += [full((1, d)), mod_spec(0), mod_spec(1), full((8, d)), full((d, lr)), full((d, lr)), full((d, lr)),
              coltile(d), coltile(d), coltile(d), coltile(lr), coltile(lr), coltile(lr), coltile(lr), coltile(lr),
              coltile(8), full((tn, tn))]
    lo, hi = jax.ShapeDtypeStruct((n, d), BF16), jax.ShapeDtypeStruct((n, d), F32)
    return pl.pallas_call(
        functools.partial(_rwkv_proj_kernel, grid_mode=grid_mode, tpb=tpb),
        out_shape=[lo, lo, lo, lo, hi, hi, hi, hi, lo],
        grid=(n // tm, nj),
        in_specs=specs,
        out_specs=[pl.BlockSpec((tm, tn), lambda i, j: (i, j))] * 9,
        scratch_shapes=[pltpu.VMEM((tm, d), BF16)] * 3 + [pltpu.VMEM((tm, lr), BF16)] * 3,
        compiler_params=_cparams(("arbitrary", "arbitrary")),
        name="rwkv_proj",
    )(*args)


def _wkv_kernel(r_ref, k_ref, v_ref, lw_ref, al_ref, kk_ref, ka_ref, s0_ref, bm_ref, ms_ref, mi_ref, tri_ref,
                ones_ref, eye_ref, y_ref, sf_ref, s_scr, *, rev, chunk, heads, nq, need_y, group):
    cc = pl.program_id(1)
    c = chunk
    wd = heads * RWKV_HEAD
    rn = heads * c

    @pl.when(cc == 0)
    def _():
        s_scr[...] = s0_ref[...]

    bm = bm_ref[...]
    bmf = bm.astype(F32)
    strict = ms_ref[...] > 0.0
    incl = mi_ref[...] > 0.0
    tri = tri_ref[...]
    ones = ones_ref[...]
    eye = eye_ref[...]

    def stack(x):
        return jnp.concatenate([x.astype(BF16)] * heads, axis=0) * bm

    if not need_y:
        y_ref[...] = jnp.zeros_like(y_ref)

    def block(q):
        sl = slice(q * wd, (q + 1) * wd)
        r, k, v = r_ref[:, sl].astype(F32), k_ref[:, sl].astype(F32), v_ref[:, sl].astype(F32)
        lw, al = lw_ref[:, sl], al_ref[:, sl]
        kk0 = k * kk_ref[:, sl]
        kk = kk0 * lax.rsqrt(jnp.maximum(_dot_x2(kk0 * kk0, ones), KK_EPS))
        b = kk * al
        kd = k * (1.0 + (al - 1.0) * ka_ref[:, sl])
        l3 = _split3(lw)
        lg = _dot(tri, l3[0]) + (_dot(tri, l3[1]) + _dot(tri, l3[2]))
        gc = lg[0:1] if rev else lg[c - 1:c]
        gin = jnp.exp(-lg)
        gout = jnp.exp(gc - lg)
        a_n = -kk * jnp.exp(lg - lw)
        r_n = r * jnp.exp(lg)
        a_s = stack(a_n)
        v_s = stack(v)
        bk = jnp.concatenate([stack(b * gin), stack(kd * gin)], axis=0)
        bo_n = (b * gout).astype(BF16)
        ko_n = (kd * gout).astype(BF16)
        yield

        lhs = jnp.concatenate([a_n, r_n], axis=0) if need_y else a_n
        gm = _dot_nt(lhs.astype(BF16), bk)
        aab = jnp.where(strict, gm[:c, :rn], 0.0)
        aak = jnp.where(strict, gm[:c, rn:], 0.0).astype(BF16)
        if need_y:
            arb = jnp.where(incl, gm[c:, :rn], 0.0).astype(BF16)
            ark = jnp.where(incl, gm[c:, rn:], 0.0).astype(BF16)
        yield
        wy = _dot(jnp.concatenate([aak, ark], axis=0) if need_y else aak, v_s)
        w0 = wy[:c]

        pm = eye + aab
        xb = aab.astype(BF16)
        x = _dot(xb, stack(xb))
        yield
        steps = int(math.log2(c))
        for _ in range(steps - 2):
            xb = x.astype(BF16)
            res = _dot(xb, jnp.concatenate([stack(xb), stack(pm)], axis=1))
            x = res[:, :rn]
            pm = pm + res[:, rn:]
            yield
        t = (pm + _dot(x.astype(BF16), stack(pm))).astype(BF16)
        yield
        if need_y:
            t = jnp.concatenate([t, _dot(arb, stack(t)).astype(BF16)], axis=0)
            yield

        tv = _dot(t, jnp.concatenate([a_s, stack(w0)], axis=1))
        av = tv[:c]
        yield
        s = s_scr[q]
        sb = s.astype(BF16)
        if need_y:
            ry = tv[c:]
            rh = r_n + ry[:, :wd]
            yk = wy[c:]
        pd = _dot(av.T.astype(BF16), bo_n)
        dt = (pd[wd:] + _dot(v.T.astype(BF16), ko_n)) * bmf
        pt = (pd[:wd] * bmf).astype(BF16)
        yield
        if need_y:
            y_ref[:, sl] = _dot_nt(rh.astype(BF16), sb) + ry[:, wd:] + yk
        s_scr[q] = s * jnp.exp(gc) + _dot(sb, pt) + dt

    for q0 in range(0, nq, group):
        live = [block(q) for q in range(q0, min(q0 + group, nq))]
        while live:
            nxt = []
            for gen in live:
                try:
                    next(gen)
                    nxt.append(gen)
                except StopIteration:
                    pass
            live = nxt

    @pl.when(cc == pl.num_programs(1) - 1)
    def _():
        sf_ref[...] = s_scr[...]


def _wkv_consts(rev):
    c, g = WKV_CHUNK, WKV_HEADS_PER_BLOCK
    wd, rn = g * RWKV_HEAD, g * c
    assert c == RWKV_HEAD, "one (rn, wd) mask serves both (head, time) and (head, channel) columns"
    hrow = np.arange(rn) // c
    hlane = np.arange(wd) // RWKV_HEAD
    bm = (hrow[:, None] == hlane[None, :]).astype(np.float32)
    t = np.arange(c)[:, None]
    src = (np.arange(rn) % c)[None, :]
    if rev:
        strict, incl = src > t, src >= t
        tri = np.triu(np.ones((c, c), np.float32))
    else:
        strict, incl = src < t, src <= t
        tri = np.tril(np.ones((c, c), np.float32))
    ones = (hlane[:, None] == hlane[None, :]).astype(np.float32)
    return (jnp.asarray(bm, BF16), jnp.asarray(strict, F32), jnp.asarray(incl, F32), jnp.asarray(tri, BF16),
            jnp.asarray(ones, BF16), jnp.asarray(src == t, F32))


def _wkv(r, k, v, lw, al, kkp, kap, s0, bsz, *, rev, need_y):
    n, d = r.shape
    c, g = WKV_CHUNK, WKV_HEADS_PER_BLOCK
    wd, rn = g * RWKV_HEAD, g * c
    nq = d // wd
    nc = n // bsz // c
    if rev:
        tok = lambda b, cc: (b * nc + (nc - 1 - cc), 0)
    else:
        tok = lambda b, cc: (b * nc + cc, 0)
    tspec = pl.BlockSpec((c, d), tok)
    full = lambda shape: pl.BlockSpec(shape, lambda b, cc: (0,) * len(shape))
    sspec = pl.BlockSpec((None, nq, wd, wd), lambda b, cc: (b, 0, 0, 0))
    consts = _wkv_consts(rev)
    y, sf = pl.pallas_call(
        functools.partial(_wkv_kernel, rev=rev, chunk=c, heads=g, nq=nq, need_y=need_y, group=WKV_GROUP),
        out_shape=[jax.ShapeDtypeStruct((n, d), F32), jax.ShapeDtypeStruct((bsz, nq, wd, wd), F32)],
        grid=(bsz, nc),
        in_specs=[tspec] * 5 + [full((1, d)), full((1, d)), sspec,
                                full((rn, wd)), full((c, rn)), full((c, rn)), full((c, c)), full((wd, wd)),
                                full((c, rn))],
        out_specs=[tspec, sspec],
        scratch_shapes=[pltpu.VMEM((nq, wd, wd), F32)],
        compiler_params=_cparams(("arbitrary", "arbitrary")),
        name="wkv_scan",
    )(r, k, v, lw, al, kkp, kap, s0, *consts)
    return y, sf


def _rwkv_out_kernel(yf_ref, yb_ref, bon_ref, g_ref, x_ref, gt_ref, lnw_ref, lnb_ref, ones_ref, wo_ref, o_ref,
                     *, wd):
    y = yf_ref[...] + yb_ref[...]
    ones = ones_ref[...]
    d = y.shape[1]
    inv = 1.0 / RWKV_HEAD

    def headsum(z):
        return jnp.concatenate([_dot_x2(z[:, q * wd:(q + 1) * wd], ones) for q in range(d // wd)], axis=1)

    dev = y - headsum(y) * inv
    var = headsum(dev * dev) * inv
    yn = dev * lax.rsqrt(var + GN_EPS) * lnw_ref[...] + lnb_ref[...]
    o = (yn + bon_ref[...].astype(F32)) * g_ref[...].astype(F32)
    o_ref[...] = x_ref[...] + gt_ref[...] * _dot(o.astype(BF16), wo_ref[...])


def _rwkv_out(yf, yb, bon, g, x, mods, row_of_tile, lnw, lnb, wo, *, tm):
    n, d = x.shape
    wd = WKV_HEADS_PER_BLOCK * RWKV_HEAD
    hl = np.arange(wd) // RWKV_HEAD
    ones = jnp.asarray(hl[:, None] == hl[None, :], BF16)
    tspec = pl.BlockSpec((tm, d), lambda i: (i, 0))
    full = lambda shape: pl.BlockSpec(shape, lambda i: (0,) * len(shape))
    return pl.pallas_call(
        functools.partial(_rwkv_out_kernel, wd=wd),
        out_shape=jax.ShapeDtypeStruct((n, d), F32),
        grid=(n // tm,),
        in_specs=[tspec] * 5 + [pl.BlockSpec((None, 1, d), lambda i: (row_of_tile(i) * N_MOD + 5, 0, 0)),
                                full((1, d)), full((1, d)), full((wd, wd)), full((d, d))],
        out_specs=tspec,
        compiler_params=_cparams(("arbitrary",)),
        name="rwkv_out",
    )(yf, yb, bon, g, x, mods, lnw, lnb, ones, wo)


def _rwkv_params(j, d, rwkv_mu, rwkv_w_rkv, rwkv_w0, rwkv_w1, rwkv_w2, rwkv_a0, rwkv_a1, rwkv_a2, rwkv_g1,
                 rwkv_g2, rwkv_k_a, rwkv_r_k):
    lr = rwkv_g1.shape[2]
    rank = rwkv_w1.shape[3]
    assert 2 * rank <= lr

    def first(w):
        return jnp.pad(jnp.concatenate([w[0], w[1]], axis=1), ((0, 0), (0, lr - 2 * rank))).astype(BF16)

    def second(w, z):
        return jnp.pad(w, ((z * rank, lr - (z + 1) * rank), (0, 0))).astype(BF16)

    vec = jnp.stack([rwkv_w0[j, 0], rwkv_w0[j, 1], rwkv_a0[j, 0], rwkv_a0[j, 1], rwkv_k_a[j],
                     rwkv_r_k[j].reshape(d), jnp.zeros((d,), F32), jnp.zeros((d,), F32)])
    hl = np.arange(256) // RWKV_HEAD
    return dict(
        mu=jnp.pad(rwkv_mu[j], ((0, 2), (0, 0))),
        w1c=first(rwkv_w1[j]), a1c=first(rwkv_a1[j]), g1=rwkv_g1[j].astype(BF16),
        wr=rwkv_w_rkv[j, 0].astype(BF16), wk=rwkv_w_rkv[j, 1].astype(BF16), wv=rwkv_w_rkv[j, 2].astype(BF16),
        w2f=second(rwkv_w2[j, 0], 0), w2b=second(rwkv_w2[j, 1], 1),
        a2f=second(rwkv_a2[j, 0], 0), a2b=second(rwkv_a2[j, 1], 1),
        g2=rwkv_g2[j].astype(BF16), vec=vec,
        ones_head=jnp.asarray(hl[:, None] == hl[None, :], BF16),
    )


def kernel(x, c, ctx, c_ctx, mod_w, mod_b, norm_w, ffn_w13, ffn_w2, fnet_w_o, fnet_b_o, rwkv_mu, rwkv_w_rkv,
           rwkv_w0, rwkv_w1, rwkv_w2, rwkv_a0, rwkv_a1, rwkv_a2, rwkv_g1, rwkv_g2, rwkv_k_k, rwkv_k_a, rwkv_r_k,
           rwkv_ln_w, rwkv_ln_b, rwkv_w_o, final_norm_w):
    bsz, seq, d = x.shape
    lc = ctx.shape[1]
    depth = mod_w.shape[0]
    assert depth == 2 and bsz == 2, "layer schedule below is written for the two-layer, batch-2 trunk"
    tm = 512
    assert seq % tm == 0 and seq % GRID_W == 0

    c8 = jnp.concatenate([c, c_ctx[None], jnp.zeros((8 - bsz - 1, d), F32)], axis=0)
    mods_all = _adaln(c8, mod_w, mod_b)
    f2 = ffn_w13.shape[-1]
    w13 = ffn_w13.astype(BF16).reshape(2 * depth, d, f2)
    w2 = ffn_w2.astype(BF16).reshape(2 * depth, f2 // 2, d)
    fw = final_norm_w.reshape(1, d)

    xl = x.reshape(bsz * seq, d)
    xc = ctx.reshape(bsz * lc, d)
    lat_row = lambda i: i // (seq // tm)
    ctx_row = lambda i: 2

    mods = mods_all[0, :3].reshape(3 * N_MOD, 1, d)
    nw = norm_w[0].reshape(3, 1, d)
    xl, hl = _ffn(xl, mods, lat_row, 0, nw[0], w13, w2, 0, fw, tm=tm, next_nw=nw[1])
    xc, hc = _ffn(xc, mods, ctx_row, 0, nw[0], w13, w2, 0, fw, tm=lc, next_nw=nw[1])
    wo = fnet_w_o[0].astype(BF16)
    bo = fnet_b_o[0].reshape(1, d)
    xl = _fourier_latent(xl, hl, bsz, seq, mods, wo, bo)
    xc = _fourier_ctx(xc, hc, bsz, lc, mods, wo, bo)
    xl = _ffn(xl, mods, lat_row, 2, nw[2], w13, w2, 1, fw, tm=tm)
    xc = _ffn(xc, mods, ctx_row, 2, nw[2], w13, w2, 1, fw, tm=lc)

    mods = mods_all[1, :3].reshape(3 * N_MOD, 1, d)
    nw = norm_w[1].reshape(3, 1, d)
    xl = _ffn(xl, mods, lat_row, 0, nw[0], w13, w2, 2, fw, tm=tm)
    xc = _ffn(xc, mods, ctx_row, 0, nw[0], w13, w2, 2, fw, tm=lc)
    prm = _rwkv_params(0, d, rwkv_mu, rwkv_w_rkv, rwkv_w0, rwkv_w1, rwkv_w2, rwkv_a0, rwkv_a1, rwkv_a2, rwkv_g1,
                       rwkv_g2, rwkv_k_a, rwkv_r_k)
    r_c, k_c, v_c, _, lwf_c, lwb_c, alf_c, alb_c, _ = _rwkv_proj(
        xc, mods, ctx_row, nw[1], prm, tm=lc, grid_mode=False, tpb=1)
    r_l, k_l, v_l, g_l, lwf_l, lwb_l, alf_l, alb_l, bon_l = _rwkv_proj(
        xl, mods, lat_row, nw[1], prm, tm=tm, grid_mode=True, tpb=seq // tm)
    kkp = rwkv_k_k[0].reshape(1, d)
    kap = rwkv_k_a[0].reshape(1, d)
    wd = WKV_HEADS_PER_BLOCK * RWKV_HEAD
    s0 = jnp.zeros((bsz, d // wd, wd, wd), F32)
    _, s_f = _wkv(r_c, k_c, v_c, lwf_c, alf_c, kkp, kap, s0, bsz, rev=False, need_y=False)
    _, s_b = _wkv(r_c, k_c, v_c, lwb_c, alb_c, kkp, kap, s0, bsz, rev=True, need_y=False)
    yf, _ = _wkv(r_l, k_l, v_l, lwf_l, alf_l, kkp, kap, s_f, bsz, rev=False, need_y=True)
    yb, _ = _wkv(r_l, k_l, v_l, lwb_l, alb_l, kkp, kap, s_b, bsz, rev=True, need_y=True)
    xl = _rwkv_out(yf, yb, bon_l, g_l, xl, mods, lambda i: i // (seq // 256), rwkv_ln_w[0].reshape(1, d),
                   rwkv_ln_b[0].reshape(1, d), rwkv_w_o[0].astype(BF16), tm=256)
    xl = _ffn(xl, mods, lat_row, 2, nw[2], w13, w2, 3, fw, tm=tm, final_norm=True)
    return xl.reshape(bsz, seq, d)
```

```python
import functools
import math

import numpy as np
import jax
import jax.numpy as jnp
from jax import lax
from jax.experimental import pallas as pl
from jax.experimental.pallas import tpu as pltpu

F32 = jnp.float32
BF16 = jnp.bfloat16

NORM_EPS = 1e-6
GN_EPS = 64e-5
KK_EPS = 1e-24
GRID_W = 64
FNET_GROUPS = 8
RWKV_HEAD = 64
N_MOD = 9
DECAY_SCALE = math.exp(-0.5)

LANE = 128
WKV_CHUNK = 64
WKV_HEADS_PER_BLOCK = 2
WKV_GROUP = 16
VMEM_LIMIT = 56 * 1024 * 1024


def _cparams(sem):
    return pltpu.CompilerParams(dimension_semantics=sem, vmem_limit_bytes=VMEM_LIMIT)


def _sigmoid(x):
    return 1.0 / (1.0 + jnp.exp(-x))


def _modulate(x, nw, shift, scale):
    y = x * lax.rsqrt(jnp.mean(x * x, axis=-1, keepdims=True) + NORM_EPS)
    return (y * nw) * (1.0 + scale) + shift


def _dot(a, b):
    return jnp.dot(a, b, preferred_element_type=F32)


def _dot_nt(a, b):
    return lax.dot_general(a, b, (((1,), (1,)), ((), ())), preferred_element_type=F32)


def _split2(x):
    hi = x.astype(BF16)
    lo = (x - hi.astype(F32)).astype(BF16)
    return hi, lo


def _split3(x):
    hi = x.astype(BF16)
    r1 = x - hi.astype(F32)
    mid = r1.astype(BF16)
    lo = (r1 - mid.astype(F32)).astype(BF16)
    return hi, mid, lo


def _dot_x2(x, w_bf16):
    hi, lo = _split2(x)
    return _dot(hi, w_bf16) + _dot(lo, w_bf16)


def _adaln_kernel(c_ref, w_ref, b_ref, o_ref):
    c = c_ref[...]
    s = c * _sigmoid(c)
    o_ref[...] = _dot(s.astype(BF16), w_ref[...].astype(BF16)) + b_ref[...]


def _adaln(c8, mod_w, mod_b):
    depth, d, nd = mod_w.shape
    tn = 1024 if nd % 1024 == 0 else 512
    return pl.pallas_call(
        _adaln_kernel,
        out_shape=jax.ShapeDtypeStruct((depth, 8, nd), F32),
        grid=(depth, nd // tn),
        in_specs=[
            pl.BlockSpec((8, d), lambda l, j: (0, 0)),
            pl.BlockSpec((None, d, tn), lambda l, j: (l, 0, j)),
            pl.BlockSpec((None, 1, tn), lambda l, j: (l, 0, j)),
        ],
        out_specs=pl.BlockSpec((None, 8, tn), lambda l, j: (l, 0, j)),
        compiler_params=_cparams(("arbitrary", "arbitrary")),
        name="adaln_linear",
    )(c8, mod_w, mod_b.reshape(depth, 1, nd))


def _ffn_kernel(*refs, final_norm, emit_h, ahead):
    it = iter(refs)
    x_ref = next(it)
    if ahead:
        xn_ref = next(it)
    sh_ref, sc_ref = next(it), next(it)
    if ahead:
        shn_ref, scn_ref = next(it), next(it)
    gt_ref, nw_ref, w1g_ref, w1u_ref, w2_ref, fw_ref = (next(it) for _ in range(6))
    if emit_h:
        nw2_ref, sh2_ref, sc2_ref = next(it), next(it), next(it)
    o_ref = next(it)
    if emit_h:
        h2_ref = next(it)
    h_ref, acc_ref = next(it), next(it)
    i = pl.program_id(0)
    j = pl.program_id(1)

    @pl.when(j == 0)
    def _():
        acc_ref[...] = jnp.zeros_like(acc_ref)

    if ahead:
        slot = i % 2
        rc = x_ref.shape[0] // ahead

        @pl.when(jnp.logical_and(i == 0, j == 0))
        def _():
            h_ref[0] = _modulate(x_ref[...], nw_ref[...], sh_ref[...], sc_ref[...]).astype(BF16)

        rows = pl.ds(pl.multiple_of(jnp.minimum(j, ahead - 1) * rc, rc), rc)
        h_ref[1 - slot, rows, :] = _modulate(
            xn_ref[rows, :], nw_ref[...], shn_ref[...], scn_ref[...]).astype(BF16)
        h = h_ref[slot]
    else:
        @pl.when(j == 0)
        def _():
            h_ref[0] = _modulate(x_ref[...], nw_ref[...], sh_ref[...], sc_ref[...]).astype(BF16)

        h = h_ref[0]
    g = _dot(h, w1g_ref[...])
    u = _dot(h, w1u_ref[...])
    a = (g * _sigmoid(g)) * u
    acc_ref[...] += _dot(a.astype(BF16), w2_ref[...])

    @pl.when(j == pl.num_programs(1) - 1)
    def _():
        y = x_ref[...] + (0.5 * gt_ref[...]) * acc_ref[...]
        if final_norm:
            y = y * lax.rsqrt(jnp.mean(y * y, axis=-1, keepdims=True) + NORM_EPS) * fw_ref[...]
        o_ref[...] = y
        if emit_h:
            h2_ref[...] = _modulate(y, nw2_ref[...], sh2_ref[...], sc2_ref[...])


def _ffn(x, mods, row_of_tile, sub, nw, w13, w2, wi, fw, *, tm, final_norm=False, next_nw=None):
    n, d = x.shape
    f = w2.shape[1]
    tf = 512
    nf = f // tf
    emit_h = next_nw is not None

    nt = n // tm
    ahead = 0 if emit_h else max(c for c in (1, 2, 4, 8) if c <= nf and tm % (8 * c) == 0)
    nxt = lambda i: jnp.minimum(i + 1, nt - 1)

    def mod_spec(k, tile=lambda i: i):
        return pl.BlockSpec((None, 1, d), lambda i, j: (row_of_tile(tile(i)) * N_MOD + 3 * sub + k, 0, 0))

    vec_spec = pl.BlockSpec((1, d), lambda i, j: (0, 0))
    tile_spec = pl.BlockSpec((tm, d), lambda i, j: (i, 0))
    out = jax.ShapeDtypeStruct((n, d), F32)
    args, specs = [x], [tile_spec]
    if ahead:
        args += [x]
        specs += [pl.BlockSpec((tm, d), lambda i, j: (nxt(i), 0))]
    args += [mods, mods]
    specs += [mod_spec(0), mod_spec(1)]
    if ahead:
        args += [mods, mods]
        specs += [mod_spec(0, nxt), mod_spec(1, nxt)]
    args += [mods, nw, w13, w13, w2, fw]
    specs += [mod_spec(2), vec_spec,
              pl.BlockSpec((None, d, tf), lambda i, j: (wi, 0, j)),
              pl.BlockSpec((None, d, tf), lambda i, j: (wi, 0, nf + j)),
              pl.BlockSpec((None, tf, d), lambda i, j: (wi, j, 0)),
              vec_spec]
    if emit_h:
        args += [next_nw, mods, mods]
        specs += [vec_spec, mod_spec(3), mod_spec(4)]
    return pl.pallas_call(
        functools.partial(_ffn_kernel, final_norm=final_norm, emit_h=emit_h, ahead=ahead),
        out_shape=[out, out] if emit_h else out,
        grid=(n // tm, nf),
        in_specs=specs,
        out_specs=[tile_spec, tile_spec] if emit_h else tile_spec,
        scratch_shapes=[pltpu.VMEM((2 if ahead else 1, tm, d), BF16), pltpu.VMEM((tm, d), F32)],
        compiler_params=_cparams(("arbitrary", "arbitrary")),
        name="ffn_swiglu",
    )(*args)


def _dft_tables(n):
    k = np.arange(n, dtype=np.int64)
    ang = 2.0 * np.pi * ((k[:, None] * k[None, :]) % n).astype(np.float64) / n
    return np.cos(ang), np.sin(ang)


def _hi_lo(m):
    m = jnp.asarray(m, F32)
    hi = m.astype(BF16)
    return hi, (m - hi.astype(F32)).astype(BF16)


DFT_SUB = 8
DFT_COLS = 512


def _dft_kernel(*refs, n_planes, mode, twiddle):
    it = iter(refs)
    x_ref = next(it)
    mats = [(next(it)[...], next(it)[...]) for _ in range(n_planes)]
    if twiddle:
        tc_ref, ts_ref = next(it), next(it)
    o_ref = next(it)
    if mode != "flat":
        stage_ref = next(it)

    def dense(p, strided):
        stage_ref[p] = strided
        return stage_ref[p]

    def transform(planes):
        y = None
        for x, (mh, ml) in zip(planes, mats):
            xh, xl = _split2(x)
            part = _dot(mh, xh) + (_dot(mh, xl) + _dot(ml, xh))
            y = part if y is None else y + part
        half = y.shape[0] // 2
        return y[:half], y[half:]

    if mode == "flat":
        yr, yi = transform([x_ref[...]])
        o_ref[0] = yr
        o_ref[1] = yi
    elif mode == "a":
        for j in range(DFT_SUB):
            yr, yi = transform([dense(0, x_ref[:, j, :])])
            if twiddle:
                reps = yr.shape[1] // LANE
                c = jnp.concatenate([tc_ref[j]] * reps, axis=1)
                s = jnp.concatenate([ts_ref[j]] * reps, axis=1)
                yr, yi = yr * c + yi * s, yi * c - yr * s
            o_ref[0, j] = yr
            o_ref[1, j] = yi
    else:
        for j in range(DFT_SUB):
            yr, yi = transform([dense(0, x_ref[0, :, j, :]), dense(1, x_ref[1, :, j, :])])
            o_ref[0, :, j, :] = yr
            o_ref[1, :, j, :] = yi


def _dft_call(mode, x, mats, out_shape, grid, x_spec, o_spec, tw=None, tw_spec=None):
    args, specs = [x], [x_spec]
    for m in mats:
        mh, ml = _hi_lo(m)
        args += [mh, ml]
        specs += [pl.BlockSpec(m.shape, lambda *_: (0, 0))] * 2
    if tw is not None:
        args += list(tw)
        specs += [tw_spec] * 2
    scratch = []
    if mode != "flat":
        scratch = [pltpu.VMEM((len(mats), mats[0].shape[1], x_spec.block_shape[-1]), F32)]
    return pl.pallas_call(
        functools.partial(_dft_kernel, n_planes=len(mats), mode=mode, twiddle=tw is not None),
        out_shape=jax.ShapeDtypeStruct(out_shape, F32),
        grid=grid,
        in_specs=specs,
        out_specs=o_spec,
        scratch_shapes=scratch,
        compiler_params=_cparams(("arbitrary",) * len(grid)),
        name="dft_" + mode,
    )(*args)


def _fmix_kernel(pr_ref, pi_ref, x_ref, gt_ref, ch_ref, cl_ref, wo_ref, bo_ref, o_ref, *, groups):
    pr, pi = pr_ref[...], pi_ref[...]
    gd = pr.shape[1] // groups
    ch, cl = ch_ref[...], cl_ref[...]
    outs = []
    for g in range(groups):
        z = jnp.concatenate([pr[:, g * gd:(g + 1) * gd], pi[:, g * gd:(g + 1) * gd]], axis=1)
        zh, zl = _split2(z)
        outs.append(_dot(zh, ch) + (_dot(zl, ch) + _dot(zh, cl)))
    f = jnp.concatenate(outs, axis=1)
    o = _dot(f.astype(BF16), wo_ref[...]) + bo_ref[...]
    o_ref[...] = x_ref[...] + gt_ref[...] * o


def _fmix(p, x, mods, row_of_tile, wo, bo, *, tm):
    n, d = x.shape
    b, _, l, _ = p.shape
    tpb = l // tm
    gd = d // FNET_GROUPS
    c, s = _dft_tables(gd)
    ch, cl = _hi_lo(np.concatenate([c, s], axis=0) / math.sqrt(gd))
    return pl.pallas_call(
        functools.partial(_fmix_kernel, groups=FNET_GROUPS),
        out_shape=jax.ShapeDtypeStruct((n, d), F32),
        grid=(n // tm,),
        in_specs=[
            pl.BlockSpec((None, None, tm, d), lambda i: (i // tpb, 0, i % tpb, 0)),
            pl.BlockSpec((None, None, tm, d), lambda i: (i // tpb, 1, i % tpb, 0)),
            pl.BlockSpec((tm, d), lambda i: (i, 0)),
            pl.BlockSpec((None, 1, d), lambda i: (row_of_tile(i) * N_MOD + 5, 0, 0)),
            pl.BlockSpec((2 * gd, gd), lambda i: (0, 0)),
            pl.BlockSpec((2 * gd, gd), lambda i: (0, 0)),
            pl.BlockSpec((d, d), lambda i: (0, 0)),
            pl.BlockSpec((1, d), lambda i: (0, 0)),
        ],
        out_specs=pl.BlockSpec((tm, d), lambda i: (i, 0)),
        compiler_params=_cparams(("arbitrary",)),
        name="fourier_out",
    )(p, p, x, mods, ch, cl, wo, bo)


def _fourier_latent(xl, hl, bsz, seq, mods, wo, bo):
    n, d = xl.shape
    la = lb = int(round(math.sqrt(seq)))
    assert la * lb == seq and lb % DFT_SUB == 0 and la % DFT_SUB == 0
    td = min(DFT_COLS, d)
    s8 = DFT_SUB
    ca, sa = _dft_tables(la)
    m_a = np.concatenate([ca, -sa], axis=0) / math.sqrt(la)
    cb, sb = _dft_tables(lb)
    m_br = np.concatenate([cb, -sb], axis=0) / math.sqrt(lb)
    m_bi = np.concatenate([sb, cb], axis=0) / math.sqrt(lb)
    n2 = np.arange(lb, dtype=np.int64)[:, None]
    k1 = np.arange(la, dtype=np.int64)[None, :]
    ang = 2.0 * np.pi * ((n2 * k1) % seq).astype(np.float64) / seq
    twc = jnp.asarray(np.broadcast_to(np.cos(ang)[:, :, None], (lb, la, LANE)), F32)
    tws = jnp.asarray(np.broadcast_to(np.sin(ang)[:, :, None], (lb, la, LANE)), F32)
    grid = (bsz, lb // s8, d // td)
    a = _dft_call(
        "a", hl.reshape(bsz, la, lb, d), [m_a], (bsz, 2, lb, la, d), grid,
        pl.BlockSpec((None, la, s8, td), lambda b, g, c: (b, 0, g, c)),
        pl.BlockSpec((None, 2, s8, la, td), lambda b, g, c: (b, 0, g, 0, c)),
        tw=(twc, tws), tw_spec=pl.BlockSpec((s8, la, LANE), lambda b, g, c: (g, 0, 0)))
    p = _dft_call(
        "b", a, [m_br, m_bi], (bsz, 2, lb, la, d), (bsz, la // s8, d // td),
        pl.BlockSpec((None, 2, lb, s8, td), lambda b, g, c: (b, 0, 0, g, c)),
        pl.BlockSpec((None, 2, lb, s8, td), lambda b, g, c: (b, 0, 0, g, c)))
    p = p.reshape(bsz, 2, seq, d)
    return _fmix(p, xl, mods, lambda i: i // (seq // 256), wo, bo, tm=256)


def _fourier_ctx(xc, hc, bsz, lc, mods, wo, bo):
    n, d = xc.shape
    td = min(DFT_COLS, d)
    c, s = _dft_tables(lc)
    m = np.concatenate([c, -s], axis=0) / math.sqrt(lc)
    p = _dft_call(
        "flat", hc.reshape(bsz, lc, d), [m], (bsz, 2, lc, d), (bsz, d // td),
        pl.BlockSpec((None, lc, td), lambda b, c: (b, 0, c)),
        pl.BlockSpec((None, 2, lc, td), lambda b, c: (b, 0, 0, c)))
    return _fmix(p, xc, mods, lambda i: 2, wo, bo, tm=lc)


def _rwkv_proj_kernel(*refs, grid_mode, tpb):
    it = iter(refs)
    x_ref = next(it)
    if grid_mode:
        xp_ref, xn_ref = next(it), next(it)
    nw_ref, sh_ref, sc_ref, mu_ref = next(it), next(it), next(it), next(it)
    w1_ref, a1_ref, g1_ref = next(it), next(it), next(it)
    wr_ref, wk_ref, wv_ref = next(it), next(it), next(it)
    w2f_ref, w2b_ref, a2f_ref, a2b_ref, g2_ref = next(it), next(it), next(it), next(it), next(it)
    vec_ref, ones_ref = next(it), next(it)
    r_ref, k_ref, v_ref, g_ref, lwf_ref, lwb_ref, alf_ref, alb_ref, bon_ref = (next(it) for _ in range(9))
    hr_ref, hk_ref, hv_ref, tw_ref, ta_ref, tg_ref = (next(it) for _ in range(6))

    i = pl.program_id(0)
    j = pl.program_id(1)

    @pl.when(j == 0)
    def _():
        nw, sh, sc = nw_ref[...], sh_ref[...], sc_ref[...]
        h = _modulate(x_ref[...], nw, sh, sc)
        tm, d = h.shape
        q = d // 4
        row = lax.broadcasted_iota(jnp.int32, (tm, 1), 0)
        if grid_mode:
            tib = i % tpb
            up_ok = (tib != 0).astype(F32)
            dn_ok = (tib != tpb - 1).astype(F32)
            hp = _modulate(xp_ref[...], nw, sh, sc)[:, 2 * q:3 * q] * up_ok
            hn = _modulate(xn_ref[...], nw, sh, sc)[:, 3 * q:] * dn_ok
            col = row % GRID_W
            left = jnp.where(col != 0, pltpu.roll(h[:, :q], 1, 0), 0.0)
            right = jnp.where(col != GRID_W - 1, pltpu.roll(h[:, q:2 * q], tm - 1, 0), 0.0)
            up = jnp.concatenate([hp, h[:tm - GRID_W, 2 * q:3 * q]], axis=0)
            down = jnp.concatenate([h[GRID_W:, 3 * q:], hn], axis=0)
            hs = jnp.concatenate([left, right, up, down], axis=1)
        else:
            prev = jnp.where(row != 0, pltpu.roll(h, 1, 0), 0.0)
            nxt = jnp.where(row != tm - 1, pltpu.roll(h, tm - 1, 0), 0.0)
            hs = jnp.concatenate([prev[:, :q], nxt[:, q:2 * q], prev[:, 2 * q:3 * q], nxt[:, 3 * q:]], axis=1)
        dlt = hs - h
        mu = mu_ref[...]
        hr_ref[...] = (h + dlt * mu[0:1]).astype(BF16)
        hk_ref[...] = (h + dlt * mu[1:2]).astype(BF16)
        hv_ref[...] = (h + dlt * mu[2:3]).astype(BF16)
        tw_ref[...] = jnp.tanh(_dot((h + dlt * mu[3:4]).astype(BF16), w1_ref[...])).astype(BF16)
        ta_ref[...] = _dot((h + dlt * mu[4:5]).astype(BF16), a1_ref[...]).astype(BF16)
        tg_ref[...] = _sigmoid(_dot((h + dlt * mu[5:6]).astype(BF16), g1_ref[...])).astype(BF16)

    vec = vec_ref[...]
    w0f, w0b, a0f, a0b, ka, rk = (vec[n:n + 1] for n in range(6))
    r = _dot(hr_ref[...], wr_ref[...])
    k = _dot(hk_ref[...], wk_ref[...])
    v = _dot(hv_ref[...], wv_ref[...])
    tw = tw_ref[...]
    ta = ta_ref[...]
    lwf = -DECAY_SCALE * _sigmoid(w0f + _dot(tw, w2f_ref[...]))
    lwb = -DECAY_SCALE * _sigmoid(w0b + _dot(tw, w2b_ref[...]))
    alf = _sigmoid(a0f + _dot(ta, a2f_ref[...]))
    alb = _sigmoid(a0b + _dot(ta, a2b_ref[...]))
    kb = k * (1.0 + (0.5 * (alf + alb) - 1.0) * ka)
    r_ref[...] = r.astype(r_ref.dtype)
    k_ref[...] = k.astype(k_ref.dtype)
    v_ref[...] = v.astype(v_ref.dtype)
    g_ref[...] = _dot(tg_ref[...], g2_ref[...]).astype(g_ref.dtype)
    lwf_ref[...] = lwf
    lwb_ref[...] = lwb
    alf_ref[...] = alf.astype(alf_ref.dtype)
    alb_ref[...] = alb.astype(alb_ref.dtype)
    bon_ref[...] = (_dot_x2(r * kb * rk, ones_ref[...]) * v).astype(bon_ref.dtype)


def _rwkv_proj(x, mods, row_of_tile, nw, prm, *, tm, grid_mode, tpb):
    n, d = x.shape
    tn = 256
    nj = d // tn
    hw = GRID_W
    lr = prm["w1c"].shape[1]

    def mod_spec(k):
        return pl.BlockSpec((None, 1, d), lambda i, j: (row_of_tile(i) * N_MOD + 3 + k, 0, 0))

    args, specs = [x], [pl.BlockSpec((tm, d), lambda i, j: (i, 0))]
    if grid_mode:
        nb = n // hw
        r = tm // hw
        args += [x, x]
        specs += [pl.BlockSpec((hw, d), lambda i, j: (jnp.maximum(i * r - 1, 0), 0)),
                  pl.BlockSpec((hw, d), lambda i, j: (jnp.minimum((i + 1) * r, nb - 1), 0))]
    args += [nw, mods, mods, prm["mu"], prm["w1c"], prm["a1c"], prm["g1"], prm["wr"], prm["wk"], prm["wv"],
             prm["w2f"], prm["w2b"], prm["a2f"], prm["a2b"], prm["g2"], prm["vec"], prm["ones_head"]]
    full = lambda shape: pl.BlockSpec(shape, lambda i, j: (0,) * len(shape))
    coltile = lambda rows: pl.BlockSpec((rows, tn), lambda i, j: (0, j))
    specs += [full((1, d)), mod_spec(0), mod_spec(1), full((8, d)), full((d, lr)), full((d, lr)), full((d, lr)),
              coltile(d), coltile(d), coltile(d), coltile(lr), coltile(lr), coltile(lr), coltile(lr), coltile(lr),
              coltile(8), full((tn, tn))]
    lo, hi = jax.ShapeDtypeStruct((n, d), BF16), jax.ShapeDtypeStruct((n, d), F32)
    return pl.pallas_call(
        functools.partial(_rwkv_proj_kernel, grid_mode=grid_mode, tpb=tpb),
        out_shape=[lo, lo, lo, lo, hi, hi, lo, lo, lo],
        grid=(n // tm, nj),
        in_specs=specs,
        out_specs=[pl.BlockSpec((tm, tn), lambda i, j: (i, j))] * 9,
        scratch_shapes=[pltpu.VMEM((tm, d), BF16)] * 3 + [pltpu.VMEM((tm, lr), BF16)] * 3,
        compiler_params=_cparams(("arbitrary", "arbitrary")),
        name="rwkv_proj",
    )(*args)


def _wkv_kernel(r_ref, k_ref, v_ref, lw_ref, al_ref, kk_ref, ka_ref, s0_ref, bm_ref, ms_ref, mi_ref, tri_ref,
                ones_ref, eye_ref, y_ref, sf_ref, s_scr, *, rev, chunk, heads, nq, need_y, group):
    cc = pl.program_id(1)
    c = chunk
    wd = heads * RWKV_HEAD
    rn = heads * c

    @pl.when(cc == 0)
    def _():
        s_scr[...] = s0_ref[...]

    bm = bm_ref[...]
    bmf = bm.astype(F32)
    strict = ms_ref[...] > 0.0
    incl = mi_ref[...] > 0.0
    tri = tri_ref[...]
    ones = ones_ref[...]
    eye = eye_ref[...]

    def stack(x):
        return jnp.concatenate([x.astype(BF16)] * heads, axis=0) * bm

    if not need_y:
        y_ref[...] = jnp.zeros_like(y_ref)

    def block(q):
        sl = slice(q * wd, (q + 1) * wd)
        r, k, v = r_ref[:, sl].astype(F32), k_ref[:, sl].astype(F32), v_ref[:, sl].astype(F32)
        lw, al = lw_ref[:, sl], al_ref[:, sl].astype(F32)
        kk0 = k * kk_ref[:, sl]
        kk = kk0 * lax.rsqrt(jnp.maximum(_dot_x2(kk0 * kk0, ones), KK_EPS))
        b = kk * al
        kd = k * (1.0 + (al - 1.0) * ka_ref[:, sl])
        l3 = _split3(lw)
        lg = _dot(tri, l3[0]) + (_dot(tri, l3[1]) + _dot(tri, l3[2]))
        gc = lg[0:1] if rev else lg[c - 1:c]
        gin = jnp.exp(-lg)
        gout = jnp.exp(gc - lg)
        a_n = -kk * jnp.exp(lg - lw)
        r_n = r * jnp.exp(lg)
        a_s = stack(a_n)
        v_s = stack(v)
        bk = jnp.concatenate([stack(b * gin), stack(kd * gin)], axis=0)
        bo_n = (b * gout).astype(BF16)
        ko_n = (kd * gout).astype(BF16)
        yield

        lhs = jnp.concatenate([a_n, r_n], axis=0) if need_y else a_n
        gm = _dot_nt(lhs.astype(BF16), bk)
        aab = jnp.where(strict, gm[:c, :rn], 0.0)
        aak = jnp.where(strict, gm[:c, rn:], 0.0).astype(BF16)
        if need_y:
            arb = jnp.where(incl, gm[c:, :rn], 0.0).astype(BF16)
            ark = jnp.where(incl, gm[c:, rn:], 0.0).astype(BF16)
        yield
        wy = _dot(jnp.concatenate([aak, ark], axis=0) if need_y else aak, v_s)
        w0 = wy[:c]

        pm = eye + aab
        xb = aab.astype(BF16)
        x = _dot(xb, stack(xb))
        yield
        steps = int(math.log2(c))
        for _ in range(steps - 2):
            xb = x.astype(BF16)
            res = _dot(xb, jnp.concatenate([stack(xb), stack(pm)], axis=1))
            x = res[:, :rn]
            pm = pm + res[:, rn:]
            yield
        t = (pm + _dot(x.astype(BF16), stack(pm))).astype(BF16)
        yield
        if need_y:
            t = jnp.concatenate([t, _dot(arb, stack(t)).astype(BF16)], axis=0)
            yield

        tv = _dot(t, jnp.concatenate([a_s, stack(w0)], axis=1))
        av = tv[:c]
        yield
        s = s_scr[q]
        sb = s.astype(BF16)
        if need_y:
            ry = tv[c:]
            rh = r_n + ry[:, :wd]
            yk = wy[c:]
        pd = _dot(av.T.astype(BF16), bo_n)
        dt = (pd[wd:] + _dot(v.T.astype(BF16), ko_n)) * bmf
        pt = (pd[:wd] * bmf).astype(BF16)
        yield
        if need_y:
            y_ref[:, sl] = _dot_nt(rh.astype(BF16), sb) + ry[:, wd:] + yk
        s_scr[q] = s * jnp.exp(gc) + _dot(sb, pt) + dt

    for q0 in range(0, nq, group):
        live = [block(q) for q in range(q0, min(q0 + group, nq))]
        while live:
            nxt = []
            for gen in live:
                try:
                    next(gen)
                    nxt.append(gen)
                except StopIteration:
                    pass
            live = nxt

    @pl.when(cc == pl.num_programs(1) - 1)
    def _():
        sf_ref[...] = s_scr[...]


def _wkv_consts(rev):
    c, g = WKV_CHUNK, WKV_HEADS_PER_BLOCK
    wd, rn = g * RWKV_HEAD, g * c
    assert c == RWKV_HEAD, "one (rn, wd) mask serves both (head, time) and (head, channel) columns"
    hrow = np.arange(rn) // c
    hlane = np.arange(wd) // RWKV_HEAD
    bm = (hrow[:, None] == hlane[None, :]).astype(np.float32)
    t = np.arange(c)[:, None]
    src = (np.arange(rn) % c)[None, :]
    if rev:
        strict, incl = src > t, src >= t
        tri = np.triu(np.ones((c, c), np.float32))
    else:
        strict, incl = src < t, src <= t
        tri = np.tril(np.ones((c, c), np.float32))
    ones = (hlane[:, None] == hlane[None, :]).astype(np.float32)
    return (jnp.asarray(bm, BF16), jnp.asarray(strict, F32), jnp.asarray(incl, F32), jnp.asarray(tri, BF16),
            jnp.asarray(ones, BF16), jnp.asarray(src == t, F32))


def _wkv(r, k, v, lw, al, kkp, kap, s0, bsz, *, rev, need_y):
    n, d = r.shape
    c, g = WKV_CHUNK, WKV_HEADS_PER_BLOCK
    wd, rn = g * RWKV_HEAD, g * c
    nq = d // wd
    nc = n // bsz // c
    if rev:
        tok = lambda b, cc: (b * nc + (nc - 1 - cc), 0)
    else:
        tok = lambda b, cc: (b * nc + cc, 0)
    tspec = pl.BlockSpec((c, d), tok)
    full = lambda shape: pl.BlockSpec(shape, lambda b, cc: (0,) * len(shape))
    sspec = pl.BlockSpec((None, nq, wd, wd), lambda b, cc: (b, 0, 0, 0))
    consts = _wkv_consts(rev)
    y, sf = pl.pallas_call(
        functools.partial(_wkv_kernel, rev=rev, chunk=c, heads=g, nq=nq, need_y=need_y, group=WKV_GROUP),
        out_shape=[jax.ShapeDtypeStruct((n, d), F32), jax.ShapeDtypeStruct((bsz, nq, wd, wd), F32)],
        grid=(bsz, nc),
        in_specs=[tspec] * 5 + [full((1, d)), full((1, d)), sspec,
                                full((rn, wd)), full((c, rn)), full((c, rn)), full((c, c)), full((wd, wd)),
                                full((c, rn))],
        out_specs=[tspec, sspec],
        scratch_shapes=[pltpu.VMEM((nq, wd, wd), F32)],
        compiler_params=_cparams(("arbitrary", "arbitrary")),
        name="wkv_scan",
    )(r, k, v, lw, al, kkp, kap, s0, *consts)
    return y, sf


def _rwkv_out_kernel(yf_ref, yb_ref, bon_ref, g_ref, x_ref, gt_ref, lnw_ref, lnb_ref, ones_ref, wo_ref, o_ref,
                     *, wd):
    y = yf_ref[...] + yb_ref[...]
    ones = ones_ref[...]
    d = y.shape[1]
    inv = 1.0 / RWKV_HEAD

    def headsum(z):
        return jnp.concatenate([_dot_x2(z[:, q * wd:(q + 1) * wd], ones) for q in range(d // wd)], axis=1)

    dev = y - headsum(y) * inv
    var = headsum(dev * dev) * inv
    yn = dev * lax.rsqrt(var + GN_EPS) * lnw_ref[...] + lnb_ref[...]
    o = (yn + bon_ref[...].astype(F32)) * g_ref[...].astype(F32)
    o_ref[...] = x_ref[...] + gt_ref[...] * _dot(o.astype(BF16), wo_ref[...])


def _rwkv_out(yf, yb, bon, g, x, mods, row_of_tile, lnw, lnb, wo, *, tm):
    n, d = x.shape
    wd = WKV_HEADS_PER_BLOCK * RWKV_HEAD
    hl = np.arange(wd) // RWKV_HEAD
    ones = jnp.asarray(hl[:, None] == hl[None, :], BF16)
    tspec = pl.BlockSpec((tm, d), lambda i: (i, 0))
    full = lambda shape: pl.BlockSpec(shape, lambda i: (0,) * len(shape))
    return pl.pallas_call(
        functools.partial(_rwkv_out_kernel, wd=wd),
        out_shape=jax.ShapeDtypeStruct((n, d), F32),
        grid=(n // tm,),
        in_specs=[tspec] * 5 + [pl.BlockSpec((None, 1, d), lambda i: (row_of_tile(i) * N_MOD + 5, 0, 0)),
                                full((1, d)), full((1, d)), full((wd, wd)), full((d, d))],
        out_specs=tspec,
        compiler_params=_cparams(("arbitrary",)),
        name="rwkv_out",
    )(yf, yb, bon, g, x, mods, lnw, lnb, ones, wo)


def _rwkv_params(j, d, rwkv_mu, rwkv_w_rkv, rwkv_w0, rwkv_w1, rwkv_w2, rwkv_a0, rwkv_a1, rwkv_a2, rwkv_g1,
                 rwkv_g2, rwkv_k_a, rwkv_r_k):
    lr = rwkv_g1.shape[2]
    rank = rwkv_w1.shape[3]
    assert 2 * rank <= lr

    def first(w):
        return jnp.pad(jnp.concatenate([w[0], w[1]], axis=1), ((0, 0), (0, lr - 2 * rank))).astype(BF16)

    def second(w, z):
        return jnp.pad(w, ((z * rank, lr - (z + 1) * rank), (0, 0))).astype(BF16)

    vec = jnp.stack([rwkv_w0[j, 0], rwkv_w0[j, 1], rwkv_a0[j, 0], rwkv_a0[j, 1], rwkv_k_a[j],
                     rwkv_r_k[j].reshape(d), jnp.zeros((d,), F32), jnp.zeros((d,), F32)])
    hl = np.arange(256) // RWKV_HEAD
    return dict(
        mu=jnp.pad(rwkv_mu[j], ((0, 2), (0, 0))),
        w1c=first(rwkv_w1[j]), a1c=first(rwkv_a1[j]), g1=rwkv_g1[j].astype(BF16),
        wr=rwkv_w_rkv[j, 0].astype(BF16), wk=rwkv_w_rkv[j, 1].astype(BF16), wv=rwkv_w_rkv[j, 2].astype(BF16),
        w2f=second(rwkv_w2[j, 0], 0), w2b=second(rwkv_w2[j, 1], 1),
        a2f=second(rwkv_a2[j, 0], 0), a2b=second(rwkv_a2[j, 1], 1),
        g2=rwkv_g2[j].astype(BF16), vec=vec,
        ones_head=jnp.asarray(hl[:, None] == hl[None, :], BF16),
    )


def kernel(x, c, ctx, c_ctx, mod_w, mod_b, norm_w, ffn_w13, ffn_w2, fnet_w_o, fnet_b_o, rwkv_mu, rwkv_w_rkv,
           rwkv_w0, rwkv_w1, rwkv_w2, rwkv_a0, rwkv_a1, rwkv_a2, rwkv_g1, rwkv_g2, rwkv_k_k, rwkv_k_a, rwkv_r_k,
           rwkv_ln_w, rwkv_ln_b, rwkv_w_o, final_norm_w):
    bsz, seq, d = x.shape
    lc = ctx.shape[1]
    depth = mod_w.shape[0]
    assert depth == 2 and bsz == 2, "layer schedule below is written for the two-layer, batch-2 trunk"
    tm = 512
    assert seq % tm == 0 and seq % GRID_W == 0

    c8 = jnp.concatenate([c, c_ctx[None], jnp.zeros((8 - bsz - 1, d), F32)], axis=0)
    mods_all = _adaln(c8, mod_w, mod_b)
    f2 = ffn_w13.shape[-1]
    w13 = ffn_w13.astype(BF16).reshape(2 * depth, d, f2)
    w2 = ffn_w2.astype(BF16).reshape(2 * depth, f2 // 2, d)
    fw = final_norm_w.reshape(1, d)

    xl = x.reshape(bsz * seq, d)
    xc = ctx.reshape(bsz * lc, d)
    lat_row = lambda i: i // (seq // tm)
    ctx_row = lambda i: 2

    mods = mods_all[0, :3].reshape(3 * N_MOD, 1, d)
    nw = norm_w[0].reshape(3, 1, d)
    xl, hl = _ffn(xl, mods, lat_row, 0, nw[0], w13, w2, 0, fw, tm=tm, next_nw=nw[1])
    xc, hc = _ffn(xc, mods, ctx_row, 0, nw[0], w13, w2, 0, fw, tm=lc, next_nw=nw[1])
    wo = fnet_w_o[0].astype(BF16)
    bo = fnet_b_o[0].reshape(1, d)
    xl = _fourier_latent(xl, hl, bsz, seq, mods, wo, bo)
    xc = _fourier_ctx(xc, hc, bsz, lc, mods, wo, bo)
    xl = _ffn(xl, mods, lat_row, 2, nw[2], w13, w2, 1, fw, tm=tm)
    xc = _ffn(xc, mods, ctx_row, 2, nw[2], w13, w2, 1, fw, tm=lc)

    mods = mods_all[1, :3].reshape(3 * N_MOD, 1, d)
    nw = norm_w[1].reshape(3, 1, d)
    xl = _ffn(xl, mods, lat_row, 0, nw[0], w13, w2, 2, fw, tm=tm)
    xc = _ffn(xc, mods, ctx_row, 0, nw[0], w13, w2, 2, fw, tm=lc)
    prm = _rwkv_params(0, d, rwkv_mu, rwkv_w_rkv, rwkv_w0, rwkv_w1, rwkv_w2, rwkv_a0, rwkv_a1, rwkv_a2, rwkv_g1,
                       rwkv_g2, rwkv_k_a, rwkv_r_k)
    r_c, k_c, v_c, _, lwf_c, lwb_c, alf_c, alb_c, _ = _rwkv_proj(
        xc, mods, ctx_row, nw[1], prm, tm=lc, grid_mode=False, tpb=1)
    r_l, k_l, v_l, g_l, lwf_l, lwb_l, alf_l, alb_l, bon_l = _rwkv_proj(
        xl, mods, lat_row, nw[1], prm, tm=tm, grid_mode=True, tpb=seq // tm)
    kkp = rwkv_k_k[0].reshape(1, d)
    kap = rwkv_k_a[0].reshape(1, d)
    wd = WKV_HEADS_PER_BLOCK * RWKV_HEAD
    s0 = jnp.zeros((bsz, d // wd, wd, wd), F32)
    _, s_f = _wkv(r_c, k_c, v_c, lwf_c, alf_c, kkp, kap, s0, bsz, rev=False, need_y=False)
    _, s_b = _wkv(r_c, k_c, v_c, lwb_c, alb_c, kkp, kap, s0, bsz, rev=True, need_y=False)
    yf, _ = _wkv(r_l, k_l, v_l, lwf_l, alf_l, kkp, kap, s_f, bsz, rev=False, need_y=True)
    yb, _ = _wkv(r_l, k_l, v_l, lwb_l, alb_l, kkp, kap, s_b, bsz, rev=True, need_y=True)
    xl = _rwkv_out(yf, yb, bon_l, g_l, xl, mods, lambda i: i // (seq // 256), rwkv_ln_w[0].reshape(1, d),
                   rwkv_ln_b[0].reshape(1, d), rwkv_w_o[0].astype(BF16), tm=256)
    xl = _ffn(xl, mods, lat_row, 2, nw[2], w13, w2, 3, fw, tm=tm, final_norm=True)
    return xl.reshape(bsz, seq, d)
```

```python
import functools
import math

import numpy as np
import jax
import jax.numpy as jnp
from jax import lax
from jax.experimental import pallas as pl
from jax.experimental.pallas import tpu as pltpu

F32 = jnp.float32
BF16 = jnp.bfloat16

NORM_EPS = 1e-6
GN_EPS = 64e-5
KK_EPS = 1e-24
GRID_W = 64
FNET_GROUPS = 8
RWKV_HEAD = 64
N_MOD = 9
DECAY_SCALE = math.exp(-0.5)

LANE = 128
WKV_CHUNK = 64
WKV_HEADS_PER_BLOCK = 2
WKV_GROUP = 16
VMEM_LIMIT = 56 * 1024 * 1024


def _cparams(sem):
    return pltpu.CompilerParams(dimension_semantics=sem, vmem_limit_bytes=VMEM_LIMIT)


def _sigmoid(x):
    return 1.0 / (1.0 + jnp.exp(-x))


def _modulate(x, nw, shift, scale):
    y = x * lax.rsqrt(jnp.mean(x * x, axis=-1, keepdims=True) + NORM_EPS)
    return (y * nw) * (1.0 + scale) + shift


def _dot(a, b):
    return jnp.dot(a, b, preferred_element_type=F32)


def _dot_nt(a, b):
    return lax.dot_general(a, b, (((1,), (1,)), ((), ())), preferred_element_type=F32)


def _split2(x):
    hi = x.astype(BF16)
    lo = (x - hi.astype(F32)).astype(BF16)
    return hi, lo


def _split3(x):
    hi = x.astype(BF16)
    r1 = x - hi.astype(F32)
    mid = r1.astype(BF16)
    lo = (r1 - mid.astype(F32)).astype(BF16)
    return hi, mid, lo


def _dot_x2(x, w_bf16):
    hi, lo = _split2(x)
    return _dot(hi, w_bf16) + _dot(lo, w_bf16)


def _adaln_kernel(c_ref, w_ref, b_ref, o_ref):
    c = c_ref[...]
    s = c * _sigmoid(c)
    o_ref[...] = _dot(s.astype(BF16), w_ref[...].astype(BF16)) + b_ref[...]


def _adaln(c8, mod_w, mod_b):
    depth, d, nd = mod_w.shape
    tn = 1024 if nd % 1024 == 0 else 512
    return pl.pallas_call(
        _adaln_kernel,
        out_shape=jax.ShapeDtypeStruct((depth, 8, nd), F32),
        grid=(depth, nd // tn),
        in_specs=[
            pl.BlockSpec((8, d), lambda l, j: (0, 0)),
            pl.BlockSpec((None, d, tn), lambda l, j: (l, 0, j)),
            pl.BlockSpec((None, 1, tn), lambda l, j: (l, 0, j)),
        ],
        out_specs=pl.BlockSpec((None, 8, tn), lambda l, j: (l, 0, j)),
        compiler_params=_cparams(("arbitrary", "arbitrary")),
        name="adaln_linear",
    )(c8, mod_w, mod_b.reshape(depth, 1, nd))


def _ffn_kernel(*refs, final_norm, emit_h):
    x_ref, sh_ref, sc_ref, gt_ref, nw_ref, w1g_ref, w1u_ref, w2_ref, fw_ref = refs[:9]
    if emit_h:
        nw2_ref, sh2_ref, sc2_ref, o_ref, h2_ref, h_ref = refs[9:]
    else:
        o_ref, h_ref = refs[9:]
    j = pl.program_id(1)

    @pl.when(j == 0)
    def _():
        h_ref[...] = _modulate(x_ref[...], nw_ref[...], sh_ref[...], sc_ref[...]).astype(BF16)
        o_ref[...] = jnp.zeros_like(o_ref)

    h = h_ref[...]
    g = _dot(h, w1g_ref[...])
    u = _dot(h, w1u_ref[...])
    a = (g * _sigmoid(g)) * u
    o_ref[...] += _dot(a.astype(BF16), w2_ref[...])

    @pl.when(j == pl.num_programs(1) - 1)
    def _():
        y = x_ref[...] + (0.5 * gt_ref[...]) * o_ref[...]
        if final_norm:
            y = y * lax.rsqrt(jnp.mean(y * y, axis=-1, keepdims=True) + NORM_EPS) * fw_ref[...]
        o_ref[...] = y
        if emit_h:
            h2_ref[...] = _modulate(y, nw2_ref[...], sh2_ref[...], sc2_ref[...])


def _ffn(x, mods, row_of_tile, sub, nw, w13, w2, wi, fw, *, tm, final_norm=False, next_nw=None):
    n, d = x.shape
    f = w2.shape[1]
    tf = 512
    nf = f // tf
    emit_h = next_nw is not None

    def mod_spec(k):
        return pl.BlockSpec((None, 1, d), lambda i, j: (row_of_tile(i) * N_MOD + 3 * sub + k, 0, 0))

    vec_spec = pl.BlockSpec((1, d), lambda i, j: (0, 0))
    tile_spec = pl.BlockSpec((tm, d), lambda i, j: (i, 0))
    out = jax.ShapeDtypeStruct((n, d), F32)
    args = [x, mods, mods, mods, nw, w13, w13, w2, fw]
    specs = [tile_spec, mod_spec(0), mod_spec(1), mod_spec(2), vec_spec,
             pl.BlockSpec((None, d, tf), lambda i, j: (wi, 0, j)),
             pl.BlockSpec((None, d, tf), lambda i, j: (wi, 0, nf + j)),
             pl.BlockSpec((None, tf, d), lambda i, j: (wi, j, 0)),
             vec_spec]
    if emit_h:
        args += [next_nw, mods, mods]
        specs += [vec_spec, mod_spec(3), mod_spec(4)]
    return pl.pallas_call(
        functools.partial(_ffn_kernel, final_norm=final_norm, emit_h=emit_h),
        out_shape=[out, out] if emit_h else out,
        grid=(n // tm, nf),
        in_specs=specs,
        out_specs=[tile_spec, tile_spec] if emit_h else tile_spec,
        scratch_shapes=[pltpu.VMEM((tm, d), BF16)],
        compiler_params=_cparams(("arbitrary", "arbitrary")),
        name="ffn_swiglu",
    )(*args)


def _dft_tables(n):
    k = np.arange(n, dtype=np.int64)
    ang = 2.0 * np.pi * ((k[:, None] * k[None, :]) % n).astype(np.float64) / n
    return np.cos(ang), np.sin(ang)


def _hi_lo(m):
    m = jnp.asarray(m, F32)
    hi = m.astype(BF16)
    return hi, (m - hi.astype(F32)).astype(BF16)


DFT_SUB = 8
DFT_COLS = 512


def _dft_kernel(*refs, n_planes, mode, twiddle):
    it = iter(refs)
    x_ref = next(it)
    mats = [(next(it)[...], next(it)[...]) for _ in range(n_planes)]
    if twiddle:
        tc_ref, ts_ref = next(it), next(it)
    o_ref = next(it)
    if mode != "flat":
        stage_ref = next(it)

    def dense(p, strided):
        stage_ref[p] = strided
        return stage_ref[p]

    def transform(planes):
        y = None
        for x, (mh, ml) in zip(planes, mats):
            xh, xl = _split2(x)
            part = _dot(mh, xh) + (_dot(mh, xl) + _dot(ml, xh))
            y = part if y is None else y + part
        half = y.shape[0] // 2
        return y[:half], y[half:]

    if mode == "flat":
        yr, yi = transform([x_ref[...]])
        o_ref[0] = yr
        o_ref[1] = yi
    elif mode == "a":
        for j in range(DFT_SUB):
            yr, yi = transform([dense(0, x_ref[:, j, :])])
            if twiddle:
                reps = yr.shape[1] // LANE
                c = jnp.concatenate([tc_ref[j]] * reps, axis=1)
                s = jnp.concatenate([ts_ref[j]] * reps, axis=1)
                yr, yi = yr * c + yi * s, yi * c - yr * s
            o_ref[0, j] = yr
            o_ref[1, j] = yi
    else:
        for j in range(DFT_SUB):
            yr, yi = transform([dense(0, x_ref[0, :, j, :]), dense(1, x_ref[1, :, j, :])])
            o_ref[0, :, j, :] = yr
            o_ref[1, :, j, :] = yi


def _dft_call(mode, x, mats, out_shape, grid, x_spec, o_spec, tw=None, tw_spec=None):
    args, specs = [x], [x_spec]
    for m in mats:
        mh, ml = _hi_lo(m)
        args += [mh, ml]
        specs += [pl.BlockSpec(m.shape, lambda *_: (0, 0))] * 2
    if tw is not None:
        args += list(tw)
        specs += [tw_spec] * 2
    scratch = []
    if mode != "flat":
        scratch = [pltpu.VMEM((len(mats), mats[0].shape[1], x_spec.block_shape[-1]), F32)]
    return pl.pallas_call(
        functools.partial(_dft_kernel, n_planes=len(mats), mode=mode, twiddle=tw is not None),
        out_shape=jax.ShapeDtypeStruct(out_shape, F32),
        grid=grid,
        in_specs=specs,
        out_specs=o_spec,
        scratch_shapes=scratch,
        compiler_params=_cparams(("arbitrary",) * len(grid)),
        name="dft_" + mode,
    )(*args)


def _fmix_kernel(pr_ref, pi_ref, x_ref, gt_ref, ch_ref, cl_ref, wo_ref, bo_ref, o_ref, *, groups):
    pr, pi = pr_ref[...], pi_ref[...]
    gd = pr.shape[1] // groups
    ch, cl = ch_ref[...], cl_ref[...]
    outs = []
    for g in range(groups):
        z = jnp.concatenate([pr[:, g * gd:(g + 1) * gd], pi[:, g * gd:(g + 1) * gd]], axis=1)
        zh, zl = _split2(z)
        outs.append(_dot(zh, ch) + (_dot(zl, ch) + _dot(zh, cl)))
    f = jnp.concatenate(outs, axis=1)
    o = _dot(f.astype(BF16), wo_ref[...]) + bo_ref[...]
    o_ref[...] = x_ref[...] + gt_ref[...] * o


def _fmix(p, x, mods, row_of_tile, wo, bo, *, tm):
    n, d = x.shape
    b, _, l, _ = p.shape
    tpb = l // tm
    gd = d // FNET_GROUPS
    c, s = _dft_tables(gd)
    ch, cl = _hi_lo(np.concatenate([c, s], axis=0) / math.sqrt(gd))
    return pl.pallas_call(
        functools.partial(_fmix_kernel, groups=FNET_GROUPS),
        out_shape=jax.ShapeDtypeStruct((n, d), F32),
        grid=(n // tm,),
        in_specs=[
            pl.BlockSpec((None, None, tm, d), lambda i: (i // tpb, 0, i % tpb, 0)),
            pl.BlockSpec((None, None, tm, d), lambda i: (i // tpb, 1, i % tpb, 0)),
            pl.BlockSpec((tm, d), lambda i: (i, 0)),
            pl.BlockSpec((None, 1, d), lambda i: (row_of_tile(i) * N_MOD + 5, 0, 0)),
            pl.BlockSpec((2 * gd, gd), lambda i: (0, 0)),
            pl.BlockSpec((2 * gd, gd), lambda i: (0, 0)),
            pl.BlockSpec((d, d), lambda i: (0, 0)),
            pl.BlockSpec((1, d), lambda i: (0, 0)),
        ],
        out_specs=pl.BlockSpec((tm, d), lambda i: (i, 0)),
        compiler_params=_cparams(("arbitrary",)),
        name="fourier_out",
    )(p, p, x, mods, ch, cl, wo, bo)


def _fourier_latent(xl, hl, bsz, seq, mods, wo, bo):
    n, d = xl.shape
    la = lb = int(round(math.sqrt(seq)))
    assert la * lb == seq and lb % DFT_SUB == 0 and la % DFT_SUB == 0
    td = min(DFT_COLS, d)
    s8 = DFT_SUB
    ca, sa = _dft_tables(la)
    m_a = np.concatenate([ca, -sa], axis=0) / math.sqrt(la)
    cb, sb = _dft_tables(lb)
    m_br = np.concatenate([cb, -sb], axis=0) / math.sqrt(lb)
    m_bi = np.concatenate([sb, cb], axis=0) / math.sqrt(lb)
    n2 = np.arange(lb, dtype=np.int64)[:, None]
    k1 = np.arange(la, dtype=np.int64)[None, :]
    ang = 2.0 * np.pi * ((n2 * k1) % seq).astype(np.float64) / seq
    twc = jnp.asarray(np.broadcast_to(np.cos(ang)[:, :, None], (lb, la, LANE)), F32)
    tws = jnp.asarray(np.broadcast_to(np.sin(ang)[:, :, None], (lb, la, LANE)), F32)
    grid = (bsz, lb // s8, d // td)
    a = _dft_call(
        "a", hl.reshape(bsz, la, lb, d), [m_a], (bsz, 2, lb, la, d), grid,
        pl.BlockSpec((None, la, s8, td), lambda b, g, c: (b, 0, g, c)),
        pl.BlockSpec((None, 2, s8, la, td), lambda b, g, c: (b, 0, g, 0, c)),
        tw=(twc, tws), tw_spec=pl.BlockSpec((s8, la, LANE), lambda b, g, c: (g, 0, 0)))
    p = _dft_call(
        "b", a, [m_br, m_bi], (bsz, 2, lb, la, d), (bsz, la // s8, d // td),
        pl.BlockSpec((None, 2, lb, s8, td), lambda b, g, c: (b, 0, 0, g, c)),
        pl.BlockSpec((None, 2, lb, s8, td), lambda b, g, c: (b, 0, 0, g, c)))
    p = p.reshape(bsz, 2, seq, d)
    return _fmix(p, xl, mods, lambda i: i // (seq // 256), wo, bo, tm=256)


def _fourier_ctx(xc, hc, bsz, lc, mods, wo, bo):
    n, d = xc.shape
    td = min(DFT_COLS, d)
    c, s = _dft_tables(lc)
    m = np.concatenate([c, -s], axis=0) / math.sqrt(lc)
    p = _dft_call(
        "flat", hc.reshape(bsz, lc, d), [m], (bsz, 2, lc, d), (bsz, d // td),
        pl.BlockSpec((None, lc, td), lambda b, c: (b, 0, c)),
        pl.BlockSpec((None, 2, lc, td), lambda b, c: (b, 0, 0, c)))
    return _fmix(p, xc, mods, lambda i: 2, wo, bo, tm=lc)


def _rwkv_proj_kernel(*refs, grid_mode, tpb):
    it = iter(refs)
    x_ref = next(it)
    if grid_mode:
        xp_ref, xn_ref = next(it), next(it)
    nw_ref, sh_ref, sc_ref, mu_ref = next(it), next(it), next(it), next(it)
    w1_ref, a1_ref, g1_ref = next(it), next(it), next(it)
    wr_ref, wk_ref, wv_ref = next(it), next(it), next(it)
    w2f_ref, w2b_ref, a2f_ref, a2b_ref, g2_ref = next(it), next(it), next(it), next(it), next(it)
    vec_ref, ones_ref = next(it), next(it)
    r_ref, k_ref, v_ref, g_ref, lwf_ref, lwb_ref, alf_ref, alb_ref, bon_ref = (next(it) for _ in range(9))
    hr_ref, hk_ref, hv_ref, tw_ref, ta_ref, tg_ref = (next(it) for _ in range(6))

    i = pl.program_id(0)
    j = pl.program_id(1)

    @pl.when(j == 0)
    def _():
        nw, sh, sc = nw_ref[...], sh_ref[...], sc_ref[...]
        h = _modulate(x_ref[...], nw, sh, sc)
        tm, d = h.shape
        q = d // 4
        row = lax.broadcasted_iota(jnp.int32, (tm, 1), 0)
        if grid_mode:
            tib = i % tpb
            up_ok = (tib != 0).astype(F32)
            dn_ok = (tib != tpb - 1).astype(F32)
            hp = _modulate(xp_ref[...], nw, sh, sc)[:, 2 * q:3 * q] * up_ok
            hn = _modulate(xn_ref[...], nw, sh, sc)[:, 3 * q:] * dn_ok
            col = row % GRID_W
            left = jnp.where(col != 0, pltpu.roll(h[:, :q], 1, 0), 0.0)
            right = jnp.where(col != GRID_W - 1, pltpu.roll(h[:, q:2 * q], tm - 1, 0), 0.0)
            up = jnp.concatenate([hp, h[:tm - GRID_W, 2 * q:3 * q]], axis=0)
            down = jnp.concatenate([h[GRID_W:, 3 * q:], hn], axis=0)
            hs = jnp.concatenate([left, right, up, down], axis=1)
        else:
            prev = jnp.where(row != 0, pltpu.roll(h, 1, 0), 0.0)
            nxt = jnp.where(row != tm - 1, pltpu.roll(h, tm - 1, 0), 0.0)
            hs = jnp.concatenate([prev[:, :q], nxt[:, q:2 * q], prev[:, 2 * q:3 * q], nxt[:, 3 * q:]], axis=1)
        dlt = hs - h
        mu = mu_ref[...]
        hr_ref[...] = (h + dlt * mu[0:1]).astype(BF16)
        hk_ref[...] = (h + dlt * mu[1:2]).astype(BF16)
        hv_ref[...] = (h + dlt * mu[2:3]).astype(BF16)
        tw_ref[...] = jnp.tanh(_dot((h + dlt * mu[3:4]).astype(BF16), w1_ref[...])).astype(BF16)
        ta_ref[...] = _dot((h + dlt * mu[4:5]).astype(BF16), a1_ref[...]).astype(BF16)
        tg_ref[...] = _sigmoid(_dot((h + dlt * mu[5:6]).astype(BF16), g1_ref[...])).astype(BF16)

    vec = vec_ref[...]
    w0f, w0b, a0f, a0b, ka, rk = (vec[n:n + 1] for n in range(6))
    r = _dot(hr_ref[...], wr_ref[...])
    k = _dot(hk_ref[...], wk_ref[...])
    v = _dot(hv_ref[...], wv_ref[...])
    tw = tw_ref[...]
    ta = ta_ref[...]
    lwf = -DECAY_SCALE * _sigmoid(w0f + _dot(tw, w2f_ref[...]))
    lwb = -DECAY_SCALE * _sigmoid(w0b + _dot(tw, w2b_ref[...]))
    alf = _sigmoid(a0f + _dot(ta, a2f_ref[...]))
    alb = _sigmoid(a0b + _dot(ta, a2b_ref[...]))
    kb = k * (1.0 + (0.5 * (alf + alb) - 1.0) * ka)
    r_ref[...] = r.astype(r_ref.dtype)
    k_ref[...] = k.astype(k_ref.dtype)
    v_ref[...] = v.astype(v_ref.dtype)
    g_ref[...] = _dot(tg_ref[...], g2_ref[...]).astype(g_ref.dtype)
    lwf_ref[...] = lwf
    lwb_ref[...] = lwb
    alf_ref[...] = alf.astype(alf_ref.dtype)
    alb_ref[...] = alb.astype(alb_ref.dtype)
    bon_ref[...] = (_dot_x2(r * kb * rk, ones_ref[...]) * v).astype(bon_ref.dtype)


def _rwkv_proj(x, mods, row_of_tile, nw, prm, *, tm, grid_mode, tpb):
    n, d = x.shape
    tn = 256
    nj = d // tn
    hw = GRID_W
    lr = prm["w1c"].shape[1]

    def mod_spec(k):
        return pl.BlockSpec((None, 1, d), lambda i, j: (row_of_tile(i) * N_MOD + 3 + k, 0, 0))

    args, specs = [x], [pl.BlockSpec((tm, d), lambda i, j: (i, 0))]
    if grid_mode:
        nb = n // hw
        r = tm // hw
        args += [x, x]
        specs += [pl.BlockSpec((hw, d), lambda i, j: (jnp.maximum(i * r - 1, 0), 0)),
                  pl.BlockSpec((hw, d), lambda i, j: (jnp.minimum((i + 1) * r, nb - 1), 0))]
    args += [nw, mods, mods, prm["mu"], prm["w1c"], prm["a1c"], prm["g1"], prm["wr"], prm["wk"], prm["wv"],
             prm["w2f"], prm["w2b"], prm["a2f"], prm["a2b"], prm["g2"], prm["vec"], prm["ones_head"]]
    full = lambda shape: pl.BlockSpec(shape, lambda i, j: (0,) * len(shape))
    coltile = lambda rows: pl.BlockSpec((rows, tn), lambda i, j: (0, j))
    specs += [full((1, d)), mod_spec(0), mod_spec(1), full((8, d)), full((d, lr)), full((d, lr)), full((d, lr)),
              coltile(d), coltile(d), coltile(d), coltile(lr), coltile(lr), coltile(lr), coltile(lr), coltile(lr),
              coltile(8), full((tn, tn))]
    lo, hi = jax.ShapeDtypeStruct((n, d), BF16), jax.ShapeDtypeStruct((n, d), F32)
    return pl.pallas_call(
        functools.partial(_rwkv_proj_kernel, grid_mode=grid_mode, tpb=tpb),
        out_shape=[lo, lo, lo, lo, hi, hi, lo, lo, lo],
        grid=(n // tm, nj),
        in_specs=specs,
        out_specs=[pl.BlockSpec((tm, tn), lambda i, j: (i, j))] * 9,
        scratch_shapes=[pltpu.VMEM((tm, d), BF16)] * 3 + [pltpu.VMEM((tm, lr), BF16)] * 3,
        compiler_params=_cparams(("arbitrary", "arbitrary")),
        name="rwkv_proj",
    )(*args)


def _wkv_kernel(r_ref, k_ref, v_ref, lw_ref, al_ref, kk_ref, ka_ref, s0_ref, bm_ref, ms_ref, mi_ref, tri_ref,
                ones_ref, eye_ref, lvl_ref, y_ref, sf_ref, s_scr, *, rev, chunk, heads, nq, need_y, group):
    cc = pl.program_id(1)
    c = chunk
    wd = heads * RWKV_HEAD
    rn = heads * c

    @pl.when(cc == 0)
    def _():
        s_scr[...] = s0_ref[...]

    bm = bm_ref[...]
    bmf = bm.astype(F32)
    strict = ms_ref[...] > 0.0
    incl = mi_ref[...] > 0.0
    tri = tri_ref[...]
    ones = ones_ref[...]
    eye = eye_ref[...]

    def stack(x):
        return jnp.concatenate([x.astype(BF16)] * heads, axis=0) * bm

    if not need_y:
        y_ref[...] = jnp.zeros_like(y_ref)

    def block(q):
        sl = slice(q * wd, (q + 1) * wd)
        r, k, v = r_ref[:, sl].astype(F32), k_ref[:, sl].astype(F32), v_ref[:, sl].astype(F32)
        lw, al = lw_ref[:, sl], al_ref[:, sl].astype(F32)
        kk0 = k * kk_ref[:, sl]
        kk = kk0 * lax.rsqrt(jnp.maximum(_dot_x2(kk0 * kk0, ones), KK_EPS))
        b = kk * al
        kd = k * (1.0 + (al - 1.0) * ka_ref[:, sl])
        l3 = _split3(lw)
        lg = _dot(tri, l3[0]) + (_dot(tri, l3[1]) + _dot(tri, l3[2]))
        gc = lg[0:1] if rev else lg[c - 1:c]
        gin = jnp.exp(-lg)
        gout = jnp.exp(gc - lg)
        a_n = -kk * jnp.exp(lg - lw)
        r_n = r * jnp.exp(lg)
        a_s = stack(a_n)
        v_s = stack(v)
        bk = jnp.concatenate([stack(b * gin), stack(kd * gin)], axis=0)
        bo_n = (b * gout).astype(BF16)
        ko_n = (kd * gout).astype(BF16)
        yield

        lhs = jnp.concatenate([a_n, r_n], axis=0) if need_y else a_n
        gm = _dot_nt(lhs.astype(BF16), bk)
        gab = gm[:c, :rn]
        aak = jnp.where(strict, gm[:c, rn:], 0.0).astype(BF16)
        if need_y:
            arb = jnp.where(incl, gm[c:, :rn], 0.0).astype(BF16)
            ark = jnp.where(incl, gm[c:, rn:], 0.0).astype(BF16)
        yield
        wy = _dot(jnp.concatenate([aak, ark], axis=0) if need_y else aak, v_s)
        w0 = wy[:c]

        t = eye + jnp.where(lvl_ref[0] > 0.0, gab, 0.0)
        for lvl in range(1, int(math.log2(c))):
            n_l = jnp.where(lvl_ref[lvl] > 0.0, gab, 0.0)
            u = _dot(t.astype(BF16), stack(n_l))
            yield
            t = t + _dot(u.astype(BF16), stack(t))
            yield
        t = t.astype(BF16)
        if need_y:
            t = jnp.concatenate([t, _dot(arb, stack(t)).astype(BF16)], axis=0)
            yield

        tv = _dot(t, jnp.concatenate([a_s, stack(w0)], axis=1))
        av = tv[:c]
        yield
        s = s_scr[q]
        sb = s.astype(BF16)
        if need_y:
            ry = tv[c:]
            rh = r_n + ry[:, :wd]
            yk = wy[c:]
        pd = _dot(av.T.astype(BF16), bo_n)
        dt = (pd[wd:] + _dot(v.T.astype(BF16), ko_n)) * bmf
        pt = (pd[:wd] * bmf).astype(BF16)
        yield
        if need_y:
            y_ref[:, sl] = _dot_nt(rh.astype(BF16), sb) + ry[:, wd:] + yk
        s_scr[q] = s * jnp.exp(gc) + _dot(sb, pt) + dt

    for q0 in range(0, nq, group):
        live = [block(q) for q in range(q0, min(q0 + group, nq))]
        while live:
            nxt = []
            for gen in live:
                try:
                    next(gen)
                    nxt.append(gen)
                except StopIteration:
                    pass
            live = nxt

    @pl.when(cc == pl.num_programs(1) - 1)
    def _():
        sf_ref[...] = s_scr[...]


def _wkv_consts(rev):
    c, g = WKV_CHUNK, WKV_HEADS_PER_BLOCK
    wd, rn = g * RWKV_HEAD, g * c
    assert c == RWKV_HEAD, "one (rn, wd) mask serves both (head, time) and (head, channel) columns"
    hrow = np.arange(rn) // c
    hlane = np.arange(wd) // RWKV_HEAD
    bm = (hrow[:, None] == hlane[None, :]).astype(np.float32)
    t = np.arange(c)[:, None]
    src = (np.arange(rn) % c)[None, :]
    if rev:
        strict, incl = src > t, src >= t
        tri = np.triu(np.ones((c, c), np.float32))
    else:
        strict, incl = src < t, src <= t
        tri = np.tril(np.ones((c, c), np.float32))
    ones = (hlane[:, None] == hlane[None, :]).astype(np.float32)
    levels = [strict & ((t >> (l + 1)) == (src >> (l + 1))) & ((t >> l) != (src >> l))
              for l in range(int(math.log2(c)))]
    return (jnp.asarray(bm, BF16), jnp.asarray(strict, F32), jnp.asarray(incl, F32), jnp.asarray(tri, BF16),
            jnp.asarray(ones, BF16), jnp.asarray(src == t, F32), jnp.asarray(np.stack(levels), F32))


def _wkv(r, k, v, lw, al, kkp, kap, s0, bsz, *, rev, need_y):
    n, d = r.shape
    c, g = WKV_CHUNK, WKV_HEADS_PER_BLOCK
    wd, rn = g * RWKV_HEAD, g * c
    nq = d // wd
    nc = n // bsz // c
    if rev:
        tok = lambda b, cc: (b * nc + (nc - 1 - cc), 0)
    else:
        tok = lambda b, cc: (b * nc + cc, 0)
    tspec = pl.BlockSpec((c, d), tok)
    full = lambda shape: pl.BlockSpec(shape, lambda b, cc: (0,) * len(shape))
    sspec = pl.BlockSpec((None, nq, wd, wd), lambda b, cc: (b, 0, 0, 0))
    consts = _wkv_consts(rev)
    y, sf = pl.pallas_call(
        functools.partial(_wkv_kernel, rev=rev, chunk=c, heads=g, nq=nq, need_y=need_y, group=WKV_GROUP),
        out_shape=[jax.ShapeDtypeStruct((n, d), F32), jax.ShapeDtypeStruct((bsz, nq, wd, wd), F32)],
        grid=(bsz, nc),
        in_specs=[tspec] * 5 + [full((1, d)), full((1, d)), sspec,
                                full((rn, wd)), full((c, rn)), full((c, rn)), full((c, c)), full((wd, wd)),
                                full((c, rn)), full((int(math.log2(c)), c, rn))],
        out_specs=[tspec, sspec],
        scratch_shapes=[pltpu.VMEM((nq, wd, wd), F32)],
        compiler_params=_cparams(("arbitrary", "arbitrary")),
        name="wkv_scan",
    )(r, k, v, lw, al, kkp, kap, s0, *consts)
    return y, sf


def _rwkv_out_kernel(yf_ref, yb_ref, bon_ref, g_ref, x_ref, gt_ref, lnw_ref, lnb_ref, ones_ref, wo_ref, o_ref,
                     *, wd):
    y = yf_ref[...] + yb_ref[...]
    ones = ones_ref[...]
    d = y.shape[1]
    inv = 1.0 / RWKV_HEAD

    def headsum(z):
        return jnp.concatenate([_dot_x2(z[:, q * wd:(q + 1) * wd], ones) for q in range(d // wd)], axis=1)

    dev = y - headsum(y) * inv
    var = headsum(dev * dev) * inv
    yn = dev * lax.rsqrt(var + GN_EPS) * lnw_ref[...] + lnb_ref[...]
    o = (yn + bon_ref[...].astype(F32)) * g_ref[...].astype(F32)
    o_ref[...] = x_ref[...] + gt_ref[...] * _dot(o.astype(BF16), wo_ref[...])


def _rwkv_out(yf, yb, bon, g, x, mods, row_of_tile, lnw, lnb, wo, *, tm):
    n, d = x.shape
    wd = WKV_HEADS_PER_BLOCK * RWKV_HEAD
    hl = np.arange(wd) // RWKV_HEAD
    ones = jnp.asarray(hl[:, None] == hl[None, :], BF16)
    tspec = pl.BlockSpec((tm, d), lambda i: (i, 0))
    full = lambda shape: pl.BlockSpec(shape, lambda i: (0,) * len(shape))
    return pl.pallas_call(
        functools.partial(_rwkv_out_kernel, wd=wd),
        out_shape=jax.ShapeDtypeStruct((n, d), F32),
        grid=(n // tm,),
        in_specs=[tspec] * 5 + [pl.BlockSpec((None, 1, d), lambda i: (row_of_tile(i) * N_MOD + 5, 0, 0)),
                                full((1, d)), full((1, d)), full((wd, wd)), full((d, d))],
        out_specs=tspec,
        compiler_params=_cparams(("arbitrary",)),
        name="rwkv_out",
    )(yf, yb, bon, g, x, mods, lnw, lnb, ones, wo)


def _rwkv_params(j, d, rwkv_mu, rwkv_w_rkv, rwkv_w0, rwkv_w1, rwkv_w2, rwkv_a0, rwkv_a1, rwkv_a2, rwkv_g1,
                 rwkv_g2, rwkv_k_a, rwkv_r_k):
    lr = rwkv_g1.shape[2]
    rank = rwkv_w1.shape[3]
    assert 2 * rank <= lr

    def first(w):
        return jnp.pad(jnp.concatenate([w[0], w[1]], axis=1), ((0, 0), (0, lr - 2 * rank))).astype(BF16)

    def second(w, z):
        return jnp.pad(w, ((z * rank, lr - (z + 1) * rank), (0, 0))).astype(BF16)

    vec = jnp.stack([rwkv_w0[j, 0], rwkv_w0[j, 1], rwkv_a0[j, 0], rwkv_a0[j, 1], rwkv_k_a[j],
                     rwkv_r_k[j].reshape(d), jnp.zeros((d,), F32), jnp.zeros((d,), F32)])
    hl = np.arange(256) // RWKV_HEAD
    return dict(
        mu=jnp.pad(rwkv_mu[j], ((0, 2), (0, 0))),
        w1c=first(rwkv_w1[j]), a1c=first(rwkv_a1[j]), g1=rwkv_g1[j].astype(BF16),
        wr=rwkv_w_rkv[j, 0].astype(BF16), wk=rwkv_w_rkv[j, 1].astype(BF16), wv=rwkv_w_rkv[j, 2].astype(BF16),
        w2f=second(rwkv_w2[j, 0], 0), w2b=second(rwkv_w2[j, 1], 1),
        a2f=second(rwkv_a2[j, 0], 0), a2b=second(rwkv_a2[j, 1], 1),
        g2=rwkv_g2[j].astype(BF16), vec=vec,
        ones_head=jnp.asarray(hl[:, None] == hl[None, :], BF16),
    )


def kernel(x, c, ctx, c_ctx, mod_w, mod_b, norm_w, ffn_w13, ffn_w2, fnet_w_o, fnet_b_o, rwkv_mu, rwkv_w_rkv,
           rwkv_w0, rwkv_w1, rwkv_w2, rwkv_a0, rwkv_a1, rwkv_a2, rwkv_g1, rwkv_g2, rwkv_k_k, rwkv_k_a, rwkv_r_k,
           rwkv_ln_w, rwkv_ln_b, rwkv_w_o, final_norm_w):
    bsz, seq, d = x.shape
    lc = ctx.shape[1]
    depth = mod_w.shape[0]
    assert depth == 2 and bsz == 2, "layer schedule below is written for the two-layer, batch-2 trunk"
    tm = 512
    assert seq % tm == 0 and seq % GRID_W == 0

    c8 = jnp.concatenate([c, c_ctx[None], jnp.zeros((8 - bsz - 1, d), F32)], axis=0)
    mods_all = _adaln(c8, mod_w, mod_b)
    f2 = ffn_w13.shape[-1]
    w13 = ffn_w13.astype(BF16).reshape(2 * depth, d, f2)
    w2 = ffn_w2.astype(BF16).reshape(2 * depth, f2 // 2, d)
    fw = final_norm_w.reshape(1, d)

    xl = x.reshape(bsz * seq, d)
    xc = ctx.reshape(bsz * lc, d)
    lat_row = lambda i: i // (seq // tm)
    tmf = tm
    lat_row_f = lat_row
    ctx_row = lambda i: 2

    mods = mods_all[0, :3].reshape(3 * N_MOD, 1, d)
    nw = norm_w[0].reshape(3, 1, d)
    xl, hl = _ffn(xl, mods, lat_row, 0, nw[0], w13, w2, 0, fw, tm=tm, next_nw=nw[1])
    xc, hc = _ffn(xc, mods, ctx_row, 0, nw[0], w13, w2, 0, fw, tm=lc, next_nw=nw[1])
    wo = fnet_w_o[0].astype(BF16)
    bo = fnet_b_o[0].reshape(1, d)
    xl = _fourier_latent(xl, hl, bsz, seq, mods, wo, bo)
    xc = _fourier_ctx(xc, hc, bsz, lc, mods, wo, bo)
    xl = _ffn(xl, mods, lat_row_f, 2, nw[2], w13, w2, 1, fw, tm=tmf)
    xc = _ffn(xc, mods, ctx_row, 2, nw[2], w13, w2, 1, fw, tm=lc)

    mods = mods_all[1, :3].reshape(3 * N_MOD, 1, d)
    nw = norm_w[1].reshape(3, 1, d)
    xl = _ffn(xl, mods, lat_row_f, 0, nw[0], w13, w2, 2, fw, tm=tmf)
    xc = _ffn(xc, mods, ctx_row, 0, nw[0], w13, w2, 2, fw, tm=lc)
    prm = _rwkv_params(0, d, rwkv_mu, rwkv_w_rkv, rwkv_w0, rwkv_w1, rwkv_w2, rwkv_a0, rwkv_a1, rwkv_a2, rwkv_g1,
                       rwkv_g2, rwkv_k_a, rwkv_r_k)
    r_c, k_c, v_c, _, lwf_c, lwb_c, alf_c, alb_c, _ = _rwkv_proj(
        xc, mods, ctx_row, nw[1], prm, tm=lc, grid_mode=False, tpb=1)
    r_l, k_l, v_l, g_l, lwf_l, lwb_l, alf_l, alb_l, bon_l = _rwkv_proj(
        xl, mods, lat_row, nw[1], prm, tm=tm, grid_mode=True, tpb=seq // tm)
    kkp = rwkv_k_k[0].reshape(1, d)
    kap = rwkv_k_a[0].reshape(1, d)
    wd = WKV_HEADS_PER_BLOCK * RWKV_HEAD
    s0 = jnp.zeros((bsz, d // wd, wd, wd), F32)
    _, s_f = _wkv(r_c, k_c, v_c, lwf_c, alf_c, kkp, kap, s0, bsz, rev=False, need_y=False)
    _, s_b = _wkv(r_c, k_c, v_c, lwb_c, alb_c, kkp, kap, s0, bsz, rev=True, need_y=False)
    yf, _ = _wkv(r_l, k_l, v_l, lwf_l, alf_l, kkp, kap, s_f, bsz, rev=False, need_y=True)
    yb, _ = _wkv(r_l, k_l, v_l, lwb_l, alb_l, kkp, kap, s_b, bsz, rev=True, need_y=True)
    xl = _rwkv_out(yf, yb, bon_l, g_l, xl, mods, lambda i: i // (seq // 256), rwkv_ln_w[0].reshape(1, d),
                   rwkv_ln_b[0].reshape(1, d), rwkv_w_o[0].astype(BF16), tm=256)
    xl = _ffn(xl, mods, lat_row_f, 2, nw[2], w13, w2, 3, fw, tm=tmf, final_norm=True)
    return xl.reshape(bsz, seq, d)
```

```python
import functools
import math

import numpy as np
import jax
import jax.numpy as jnp
from jax import lax
from jax.experimental import pallas as pl
from jax.experimental.pallas import tpu as pltpu

F32 = jnp.float32
BF16 = jnp.bfloat16

NORM_EPS = 1e-6
GN_EPS = 64e-5
KK_EPS = 1e-24
GRID_W = 64
FNET_GROUPS = 8
RWKV_HEAD = 64
N_MOD = 9
DECAY_SCALE = math.exp(-0.5)

LANE = 128
TOKEN_TILE = 512
MIX_TILE = 256
FFN_HIDDEN_TILE = 512
PROJ_COL_TILE = 256
WKV_CHUNK = 64
WKV_HEADS_PER_BLOCK = 2
WKV_GROUP = 16
VMEM_LIMIT = 56 * 1024 * 1024


def _cparams(sem):
    return pltpu.CompilerParams(dimension_semantics=sem, vmem_limit_bytes=VMEM_LIMIT)


def _sigmoid(x):
    return 1.0 / (1.0 + jnp.exp(-x))


def _modulate(x, nw, shift, scale):
    y = x * lax.rsqrt(jnp.mean(x * x, axis=-1, keepdims=True) + NORM_EPS)
    return (y * nw) * (1.0 + scale) + shift


def _dot(a, b):
    return jnp.dot(a, b, preferred_element_type=F32)


def _dot_nt(a, b):
    return lax.dot_general(a, b, (((1,), (1,)), ((), ())), preferred_element_type=F32)


def _split2(x):
    hi = x.astype(BF16)
    lo = (x - hi.astype(F32)).astype(BF16)
    return hi, lo


def _split3(x):
    hi = x.astype(BF16)
    r1 = x - hi.astype(F32)
    mid = r1.astype(BF16)
    lo = (r1 - mid.astype(F32)).astype(BF16)
    return hi, mid, lo


def _dot_x2(x, w_bf16):
    hi, lo = _split2(x)
    return _dot(hi, w_bf16) + _dot(lo, w_bf16)


def _adaln_kernel(c_ref, w_ref, b_ref, o_ref):
    c = c_ref[...]
    s = c * _sigmoid(c)
    o_ref[...] = _dot(s.astype(BF16), w_ref[...].astype(BF16)) + b_ref[...]


def _adaln(c8, mod_w, mod_b):
    depth, d, nd = mod_w.shape
    tn = 1024 if nd % 1024 == 0 else 512
    return pl.pallas_call(
        _adaln_kernel,
        out_shape=jax.ShapeDtypeStruct((depth, 8, nd), F32),
        grid=(depth, nd // tn),
        in_specs=[
            pl.BlockSpec((8, d), lambda l, j: (0, 0)),
            pl.BlockSpec((None, d, tn), lambda l, j: (l, 0, j)),
            pl.BlockSpec((None, 1, tn), lambda l, j: (l, 0, j)),
        ],
        out_specs=pl.BlockSpec((None, 8, tn), lambda l, j: (l, 0, j)),
        compiler_params=_cparams(("arbitrary", "arbitrary")),
        name="adaln_linear",
    )(c8, mod_w, mod_b.reshape(depth, 1, nd))


def _ffn_kernel(*refs, final_norm, emit_h):
    x_ref, sh_ref, sc_ref, gt_ref, nw_ref, w1g_ref, w1u_ref, w2_ref, fw_ref = refs[:9]
    if emit_h:
        nw2_ref, sh2_ref, sc2_ref, o_ref, h2_ref, h_ref = refs[9:]
    else:
        o_ref, h_ref = refs[9:]
    j = pl.program_id(1)

    @pl.when(j == 0)
    def _():
        h_ref[...] = _modulate(x_ref[...], nw_ref[...], sh_ref[...], sc_ref[...]).astype(BF16)
        o_ref[...] = jnp.zeros_like(o_ref)

    h = h_ref[...]
    g = _dot(h, w1g_ref[...])
    u = _dot(h, w1u_ref[...])
    a = (g * _sigmoid(g)) * u
    o_ref[...] += _dot(a.astype(BF16), w2_ref[...])

    @pl.when(j == pl.num_programs(1) - 1)
    def _():
        y = x_ref[...] + (0.5 * gt_ref[...]) * o_ref[...]
        if final_norm:
            y = y * lax.rsqrt(jnp.mean(y * y, axis=-1, keepdims=True) + NORM_EPS) * fw_ref[...]
        o_ref[...] = y
        if emit_h:
            h2_ref[...] = _modulate(y, nw2_ref[...], sh2_ref[...], sc2_ref[...])


def _ffn(x, mods, row_of_tile, sub, nw, w13, w2, wi, fw, *, tm, final_norm=False, next_nw=None):
    n, d = x.shape
    f = w2.shape[1]
    tf = FFN_HIDDEN_TILE
    nf = f // tf
    emit_h = next_nw is not None

    def mod_spec(k):
        return pl.BlockSpec((None, 1, d), lambda i, j: (row_of_tile(i) * N_MOD + 3 * sub + k, 0, 0))

    vec_spec = pl.BlockSpec((1, d), lambda i, j: (0, 0))
    tile_spec = pl.BlockSpec((tm, d), lambda i, j: (i, 0))
    out = jax.ShapeDtypeStruct((n, d), F32)
    args = [x, mods, mods, mods, nw, w13, w13, w2, fw]
    specs = [tile_spec, mod_spec(0), mod_spec(1), mod_spec(2), vec_spec,
             pl.BlockSpec((None, d, tf), lambda i, j: (wi, 0, j)),
             pl.BlockSpec((None, d, tf), lambda i, j: (wi, 0, nf + j)),
             pl.BlockSpec((None, tf, d), lambda i, j: (wi, j, 0)),
             vec_spec]
    if emit_h:
        args += [next_nw, mods, mods]
        specs += [vec_spec, mod_spec(3), mod_spec(4)]
    return pl.pallas_call(
        functools.partial(_ffn_kernel, final_norm=final_norm, emit_h=emit_h),
        out_shape=[out, out] if emit_h else out,
        grid=(n // tm, nf),
        in_specs=specs,
        out_specs=[tile_spec, tile_spec] if emit_h else tile_spec,
        scratch_shapes=[pltpu.VMEM((tm, d), BF16)],
        compiler_params=_cparams(("arbitrary", "arbitrary")),
        name="ffn_swiglu",
    )(*args)


def _dft_tables(n):
    k = np.arange(n, dtype=np.int64)
    ang = 2.0 * np.pi * ((k[:, None] * k[None, :]) % n).astype(np.float64) / n
    return np.cos(ang), np.sin(ang)


def _hi_lo(m):
    m = jnp.asarray(m, F32)
    hi = m.astype(BF16)
    return hi, (m - hi.astype(F32)).astype(BF16)


DFT_SUB = 8
DFT_COLS = 1024


def _dft_kernel(*refs, n_planes, mode, twiddle):
    it = iter(refs)
    x_ref = next(it)
    mats = [(next(it)[...], next(it)[...]) for _ in range(n_planes)]
    if twiddle:
        tc_ref, ts_ref = next(it), next(it)
    o_ref = next(it)
    if mode != "flat":
        stage_ref = next(it)

    def dense(p, strided):
        stage_ref[p] = strided
        return stage_ref[p]

    def transform(planes):
        y = None
        for x, (mh, ml) in zip(planes, mats):
            xh, xl = _split2(x)
            part = _dot(mh, xh) + (_dot(mh, xl) + _dot(ml, xh))
            y = part if y is None else y + part
        half = y.shape[0] // 2
        return y[:half], y[half:]

    if mode == "flat":
        yr, yi = transform([x_ref[...]])
        o_ref[0] = yr
        o_ref[1] = yi
    elif mode == "a":
        for j in range(DFT_SUB):
            yr, yi = transform([dense(0, x_ref[:, j, :])])
            if twiddle:
                reps = yr.shape[1] // LANE
                c = jnp.concatenate([tc_ref[j]] * reps, axis=1)
                s = jnp.concatenate([ts_ref[j]] * reps, axis=1)
                yr, yi = yr * c + yi * s, yi * c - yr * s
            o_ref[0, j] = yr
            o_ref[1, j] = yi
    else:
        for j in range(DFT_SUB):
            yr, yi = transform([dense(0, x_ref[0, :, j, :]), dense(1, x_ref[1, :, j, :])])
            o_ref[0, :, j, :] = yr
            o_ref[1, :, j, :] = yi


def _dft_call(mode, x, mats, out_shape, grid, x_spec, o_spec, tw=None, tw_spec=None):
    args, specs = [x], [x_spec]
    for m in mats:
        mh, ml = _hi_lo(m)
        args += [mh, ml]
        specs += [pl.BlockSpec(m.shape, lambda *_: (0, 0))] * 2
    if tw is not None:
        args += list(tw)
        specs += [tw_spec] * 2
    scratch = []
    if mode != "flat":
        scratch = [pltpu.VMEM((len(mats), mats[0].shape[1], x_spec.block_shape[-1]), F32)]
    return pl.pallas_call(
        functools.partial(_dft_kernel, n_planes=len(mats), mode=mode, twiddle=tw is not None),
        out_shape=jax.ShapeDtypeStruct(out_shape, F32),
        grid=grid,
        in_specs=specs,
        out_specs=o_spec,
        scratch_shapes=scratch,
        compiler_params=_cparams(("arbitrary",) * len(grid)),
        name="dft_" + mode,
    )(*args)


def _fmix_kernel(pr_ref, pi_ref, x_ref, gt_ref, ch_ref, cl_ref, wo_ref, bo_ref, o_ref, *, groups):
    pr, pi = pr_ref[...], pi_ref[...]
    gd = pr.shape[1] // groups
    ch, cl = ch_ref[...], cl_ref[...]
    outs = []
    for g in range(groups):
        z = jnp.concatenate([pr[:, g * gd:(g + 1) * gd], pi[:, g * gd:(g + 1) * gd]], axis=1)
        zh, zl = _split2(z)
        outs.append(_dot(zh, ch) + (_dot(zl, ch) + _dot(zh, cl)))
    f = jnp.concatenate(outs, axis=1)
    o = _dot(f.astype(BF16), wo_ref[...]) + bo_ref[...]
    o_ref[...] = x_ref[...] + gt_ref[...] * o


def _fmix(p, x, mods, row_of_tile, wo, bo, *, tm):
    n, d = x.shape
    b, _, l, _ = p.shape
    tpb = l // tm
    gd = d // FNET_GROUPS
    c, s = _dft_tables(gd)
    ch, cl = _hi_lo(np.concatenate([c, s], axis=0) / math.sqrt(gd))
    return pl.pallas_call(
        functools.partial(_fmix_kernel, groups=FNET_GROUPS),
        out_shape=jax.ShapeDtypeStruct((n, d), F32),
        grid=(n // tm,),
        in_specs=[
            pl.BlockSpec((None, None, tm, d), lambda i: (i // tpb, 0, i % tpb, 0)),
            pl.BlockSpec((None, None, tm, d), lambda i: (i // tpb, 1, i % tpb, 0)),
            pl.BlockSpec((tm, d), lambda i: (i, 0)),
            pl.BlockSpec((None, 1, d), lambda i: (row_of_tile(i) * N_MOD + 5, 0, 0)),
            pl.BlockSpec((2 * gd, gd), lambda i: (0, 0)),
            pl.BlockSpec((2 * gd, gd), lambda i: (0, 0)),
            pl.BlockSpec((d, d), lambda i: (0, 0)),
            pl.BlockSpec((1, d), lambda i: (0, 0)),
        ],
        out_specs=pl.BlockSpec((tm, d), lambda i: (i, 0)),
        compiler_params=_cparams(("arbitrary",)),
        name="fourier_out",
    )(p, p, x, mods, ch, cl, wo, bo)


def _fourier_latent(xl, hl, bsz, seq, mods, wo, bo):
    n, d = xl.shape
    la = lb = int(round(math.sqrt(seq)))
    assert la * lb == seq and lb % DFT_SUB == 0 and la % DFT_SUB == 0
    td = min(DFT_COLS, d)
    s8 = DFT_SUB
    ca, sa = _dft_tables(la)
    m_a = np.concatenate([ca, -sa], axis=0) / math.sqrt(la)
    cb, sb = _dft_tables(lb)
    m_br = np.concatenate([cb, -sb], axis=0) / math.sqrt(lb)
    m_bi = np.concatenate([sb, cb], axis=0) / math.sqrt(lb)
    n2 = np.arange(lb, dtype=np.int64)[:, None]
    k1 = np.arange(la, dtype=np.int64)[None, :]
    ang = 2.0 * np.pi * ((n2 * k1) % seq).astype(np.float64) / seq
    twc = jnp.asarray(np.broadcast_to(np.cos(ang)[:, :, None], (lb, la, LANE)), F32)
    tws = jnp.asarray(np.broadcast_to(np.sin(ang)[:, :, None], (lb, la, LANE)), F32)
    grid = (bsz, lb // s8, d // td)
    a = _dft_call(
        "a", hl.reshape(bsz, la, lb, d), [m_a], (bsz, 2, lb, la, d), grid,
        pl.BlockSpec((None, la, s8, td), lambda b, g, c: (b, 0, g, c)),
        pl.BlockSpec((None, 2, s8, la, td), lambda b, g, c: (b, 0, g, 0, c)),
        tw=(twc, tws), tw_spec=pl.BlockSpec((s8, la, LANE), lambda b, g, c: (g, 0, 0)))
    p = _dft_call(
        "b", a, [m_br, m_bi], (bsz, 2, lb, la, d), (bsz, la // s8, d // td),
        pl.BlockSpec((None, 2, lb, s8, td), lambda b, g, c: (b, 0, 0, g, c)),
        pl.BlockSpec((None, 2, lb, s8, td), lambda b, g, c: (b, 0, 0, g, c)))
    p = p.reshape(bsz, 2, seq, d)
    return _fmix(p, xl, mods, lambda i: i // (seq // MIX_TILE), wo, bo, tm=MIX_TILE)


def _fourier_ctx(xc, hc, bsz, lc, mods, wo, bo):
    n, d = xc.shape
    td = min(DFT_COLS, d)
    c, s = _dft_tables(lc)
    m = np.concatenate([c, -s], axis=0) / math.sqrt(lc)
    p = _dft_call(
        "flat", hc.reshape(bsz, lc, d), [m], (bsz, 2, lc, d), (bsz, d // td),
        pl.BlockSpec((None, lc, td), lambda b, c: (b, 0, c)),
        pl.BlockSpec((None, 2, lc, td), lambda b, c: (b, 0, 0, c)))
    return _fmix(p, xc, mods, lambda i: 2, wo, bo, tm=lc)


def _rwkv_proj_kernel(*refs, grid_mode, tpb):
    it = iter(refs)
    x_ref = next(it)
    if grid_mode:
        xp_ref, xn_ref = next(it), next(it)
    nw_ref, sh_ref, sc_ref, mu_ref = next(it), next(it), next(it), next(it)
    w1_ref, a1_ref, g1_ref = next(it), next(it), next(it)
    wr_ref, wk_ref, wv_ref = next(it), next(it), next(it)
    w2f_ref, w2b_ref, a2f_ref, a2b_ref, g2_ref = next(it), next(it), next(it), next(it), next(it)
    vec_ref, ones_ref = next(it), next(it)
    r_ref, k_ref, v_ref, g_ref, lwf_ref, lwb_ref, alf_ref, alb_ref, bon_ref = (next(it) for _ in range(9))
    hr_ref, hk_ref, hv_ref, tw_ref, ta_ref, tg_ref = (next(it) for _ in range(6))

    i = pl.program_id(0)
    j = pl.program_id(1)

    @pl.when(j == 0)
    def _():
        nw, sh, sc = nw_ref[...], sh_ref[...], sc_ref[...]
        h = _modulate(x_ref[...], nw, sh, sc)
        tm, d = h.shape
        q = d // 4
        row = lax.broadcasted_iota(jnp.int32, (tm, 1), 0)
        if grid_mode:
            tib = i % tpb
            up_ok = (tib != 0).astype(F32)
            dn_ok = (tib != tpb - 1).astype(F32)
            hp = _modulate(xp_ref[...], nw, sh, sc)[:, 2 * q:3 * q] * up_ok
            hn = _modulate(xn_ref[...], nw, sh, sc)[:, 3 * q:] * dn_ok
            col = row % GRID_W
            left = jnp.where(col != 0, pltpu.roll(h[:, :q], 1, 0), 0.0)
            right = jnp.where(col != GRID_W - 1, pltpu.roll(h[:, q:2 * q], tm - 1, 0), 0.0)
            up = jnp.concatenate([hp, h[:tm - GRID_W, 2 * q:3 * q]], axis=0)
            down = jnp.concatenate([h[GRID_W:, 3 * q:], hn], axis=0)
            hs = jnp.concatenate([left, right, up, down], axis=1)
        else:
            prev = jnp.where(row != 0, pltpu.roll(h, 1, 0), 0.0)
            nxt = jnp.where(row != tm - 1, pltpu.roll(h, tm - 1, 0), 0.0)
            hs = jnp.concatenate([prev[:, :q], nxt[:, q:2 * q], prev[:, 2 * q:3 * q], nxt[:, 3 * q:]], axis=1)
        dlt = hs - h
        mu = mu_ref[...]
        hr_ref[...] = (h + dlt * mu[0:1]).astype(BF16)
        hk_ref[...] = (h + dlt * mu[1:2]).astype(BF16)
        hv_ref[...] = (h + dlt * mu[2:3]).astype(BF16)
        tw_ref[...] = jnp.tanh(_dot((h + dlt * mu[3:4]).astype(BF16), w1_ref[...])).astype(BF16)
        ta_ref[...] = _dot((h + dlt * mu[4:5]).astype(BF16), a1_ref[...]).astype(BF16)
        tg_ref[...] = _sigmoid(_dot((h + dlt * mu[5:6]).astype(BF16), g1_ref[...])).astype(BF16)

    vec = vec_ref[...]
    w0f, w0b, a0f, a0b, ka, rk = (vec[n:n + 1] for n in range(6))
    r = _dot(hr_ref[...], wr_ref[...])
    k = _dot(hk_ref[...], wk_ref[...])
    v = _dot(hv_ref[...], wv_ref[...])
    tw = tw_ref[...]
    ta = ta_ref[...]
    lwf = -DECAY_SCALE * _sigmoid(w0f + _dot(tw, w2f_ref[...]))
    lwb = -DECAY_SCALE * _sigmoid(w0b + _dot(tw, w2b_ref[...]))
    alf = _sigmoid(a0f + _dot(ta, a2f_ref[...]))
    alb = _sigmoid(a0b + _dot(ta, a2b_ref[...]))
    kb = k * (1.0 + (0.5 * (alf + alb) - 1.0) * ka)
    r_ref[...] = r.astype(r_ref.dtype)
    k_ref[...] = k.astype(k_ref.dtype)
    v_ref[...] = v.astype(v_ref.dtype)
    g_ref[...] = _dot(tg_ref[...], g2_ref[...]).astype(g_ref.dtype)
    lwf_ref[...] = lwf
    lwb_ref[...] = lwb
    alf_ref[...] = alf.astype(alf_ref.dtype)
    alb_ref[...] = alb.astype(alb_ref.dtype)
    bon_ref[...] = (_dot_x2(r * kb * rk, ones_ref[...]) * v).astype(bon_ref.dtype)


def _rwkv_proj(x, mods, row_of_tile, nw, prm, *, tm, grid_mode, tpb):
    n, d = x.shape
    tn = PROJ_COL_TILE
    nj = d // tn
    hw = GRID_W
    lr = prm["w1c"].shape[1]

    def mod_spec(k):
        return pl.BlockSpec((None, 1, d), lambda i, j: (row_of_tile(i) * N_MOD + 3 + k, 0, 0))

    args, specs = [x], [pl.BlockSpec((tm, d), lambda i, j: (i, 0))]
    if grid_mode:
        nb = n // hw
        r = tm // hw
        args += [x, x]
        specs += [pl.BlockSpec((hw, d), lambda i, j: (jnp.maximum(i * r - 1, 0), 0)),
                  pl.BlockSpec((hw, d), lambda i, j: (jnp.minimum((i + 1) * r, nb - 1), 0))]
    args += [nw, mods, mods, prm["mu"], prm["w1c"], prm["a1c"], prm["g1"], prm["wr"], prm["wk"], prm["wv"],
             prm["w2f"], prm["w2b"], prm["a2f"], prm["a2b"], prm["g2"], prm["vec"], prm["ones_head"]]
    full = lambda shape: pl.BlockSpec(shape, lambda i, j: (0,) * len(shape))
    coltile = lambda rows: pl.BlockSpec((rows, tn), lambda i, j: (0, j))
    specs += [full((1, d)), mod_spec(0), mod_spec(1), full((8, d)), full((d, lr)), full((d, lr)), full((d, lr)),
              coltile(d), coltile(d), coltile(d), coltile(lr), coltile(lr), coltile(lr), coltile(lr), coltile(lr),
              coltile(8), full((tn, tn))]
    lo, hi = jax.ShapeDtypeStruct((n, d), BF16), jax.ShapeDtypeStruct((n, d), F32)
    return pl.pallas_call(
        functools.partial(_rwkv_proj_kernel, grid_mode=grid_mode, tpb=tpb),
        out_shape=[lo, lo, lo, lo, hi, hi, lo, lo, lo],
        grid=(n // tm, nj),
        in_specs=specs,
        out_specs=[pl.BlockSpec((tm, tn), lambda i, j: (i, j))] * 9,
        scratch_shapes=[pltpu.VMEM((tm, d), BF16)] * 3 + [pltpu.VMEM((tm, lr), BF16)] * 3,
        compiler_params=_cparams(("arbitrary", "arbitrary")),
        name="rwkv_proj",
    )(*args)


def _wkv_kernel(r_ref, k_ref, v_ref, lw_ref, al_ref, kk_ref, ka_ref, s0_ref, bm_ref, ms_ref, mi_ref, tri_ref,
                ones_ref, eye_ref, lvl_ref, y_ref, sf_ref, s_scr, *, rev, chunk, heads, nq, need_y, group):
    cc = pl.program_id(1)
    c = chunk
    wd = heads * RWKV_HEAD
    rn = heads * c

    @pl.when(cc == 0)
    def _():
        s_scr[...] = s0_ref[...]

    bm = bm_ref[...]
    bmf = bm.astype(F32)
    strict = ms_ref[...] > 0.0
    incl = mi_ref[...] > 0.0
    tri = tri_ref[...]
    ones = ones_ref[...]
    eye = eye_ref[...]

    def stack(x):
        return jnp.concatenate([x.astype(BF16)] * heads, axis=0) * bm

    if not need_y:
        y_ref[...] = jnp.zeros_like(y_ref)

    def block(q):
        sl = slice(q * wd, (q + 1) * wd)
        r, k, v = r_ref[:, sl].astype(F32), k_ref[:, sl].astype(F32), v_ref[:, sl].astype(F32)
        lw, al = lw_ref[:, sl], al_ref[:, sl].astype(F32)
        kk0 = k * kk_ref[:, sl]
        kk = kk0 * lax.rsqrt(jnp.maximum(_dot_x2(kk0 * kk0, ones), KK_EPS))
        b = kk * al
        kd = k * (1.0 + (al - 1.0) * ka_ref[:, sl])
        l3 = _split3(lw)
        lg = _dot(tri, l3[0]) + (_dot(tri, l3[1]) + _dot(tri, l3[2]))
        gc = lg[0:1] if rev else lg[c - 1:c]
        gin = jnp.exp(-lg)
        gout = jnp.exp(gc - lg)
        a_n = -kk * jnp.exp(lg - lw)
        r_n = r * jnp.exp(lg)
        a_s = stack(a_n)
        v_s = stack(v)
        bk = jnp.concatenate([stack(b * gin), stack(kd * gin)], axis=0)
        bo_n = (b * gout).astype(BF16)
        ko_n = (kd * gout).astype(BF16)
        yield

        lhs = jnp.concatenate([a_n, r_n], axis=0) if need_y else a_n
        gm = _dot_nt(lhs.astype(BF16), bk)
        gab = gm[:c, :rn]
        aak = jnp.where(strict, gm[:c, rn:], 0.0).astype(BF16)
        if need_y:
            arb = jnp.where(incl, gm[c:, :rn], 0.0).astype(BF16)
            ark = jnp.where(incl, gm[c:, rn:], 0.0).astype(BF16)
        yield
        wy = _dot(jnp.concatenate([aak, ark], axis=0) if need_y else aak, v_s)
        w0 = wy[:c]

        t = eye + jnp.where(lvl_ref[0] > 0.0, gab, 0.0)
        for lvl in range(1, int(math.log2(c))):
            n_l = jnp.where(lvl_ref[lvl] > 0.0, gab, 0.0)
            u = _dot(t.astype(BF16), stack(n_l))
            yield
            t = t + _dot(u.astype(BF16), stack(t))
            yield
        t = t.astype(BF16)
        if need_y:
            t = jnp.concatenate([t, _dot(arb, stack(t)).astype(BF16)], axis=0)
            yield

        tv = _dot(t, jnp.concatenate([a_s, stack(w0)], axis=1))
        av = tv[:c]
        yield
        s = s_scr[q]
        sb = s.astype(BF16)
        if need_y:
            ry = tv[c:]
            rh = r_n + ry[:, :wd]
            yk = wy[c:]
        pd = _dot(av.T.astype(BF16), bo_n)
        dt = (pd[wd:] + _dot(v.T.astype(BF16), ko_n)) * bmf
        pt = (pd[:wd] * bmf).astype(BF16)
        yield
        if need_y:
            y_ref[:, sl] = _dot_nt(rh.astype(BF16), sb) + ry[:, wd:] + yk
        s_scr[q] = s * jnp.exp(gc) + _dot(sb, pt) + dt

    for q0 in range(0, nq, group):
        live = [block(q) for q in range(q0, min(q0 + group, nq))]
        while live:
            nxt = []
            for gen in live:
                try:
                    next(gen)
                    nxt.append(gen)
                except StopIteration:
                    pass
            live = nxt

    @pl.when(cc == pl.num_programs(1) - 1)
    def _():
        sf_ref[...] = s_scr[...]


def _wkv_consts(rev):
    c, g = WKV_CHUNK, WKV_HEADS_PER_BLOCK
    wd, rn = g * RWKV_HEAD, g * c
    assert c == RWKV_HEAD, "one (rn, wd) mask serves both (head, time) and (head, channel) columns"
    hrow = np.arange(rn) // c
    hlane = np.arange(wd) // RWKV_HEAD
    bm = (hrow[:, None] == hlane[None, :]).astype(np.float32)
    t = np.arange(c)[:, None]
    src = (np.arange(rn) % c)[None, :]
    if rev:
        strict, incl = src > t, src >= t
        tri = np.triu(np.ones((c, c), np.float32))
    else:
        strict, incl = src < t, src <= t
        tri = np.tril(np.ones((c, c), np.float32))
    ones = (hlane[:, None] == hlane[None, :]).astype(np.float32)
    levels = [strict & ((t >> (l + 1)) == (src >> (l + 1))) & ((t >> l) != (src >> l))
              for l in range(int(math.log2(c)))]
    return (jnp.asarray(bm, BF16), jnp.asarray(strict, F32), jnp.asarray(incl, F32), jnp.asarray(tri, BF16),
            jnp.asarray(ones, BF16), jnp.asarray(src == t, F32), jnp.asarray(np.stack(levels), F32))


def _wkv(r, k, v, lw, al, kkp, kap, s0, bsz, *, rev, need_y):
    n, d = r.shape
    c, g = WKV_CHUNK, WKV_HEADS_PER_BLOCK
    wd, rn = g * RWKV_HEAD, g * c
    nq = d // wd
    nc = n // bsz // c
    if rev:
        tok = lambda b, cc: (b * nc + (nc - 1 - cc), 0)
    else:
        tok = lambda b, cc: (b * nc + cc, 0)
    tspec = pl.BlockSpec((c, d), tok)
    full = lambda shape: pl.BlockSpec(shape, lambda b, cc: (0,) * len(shape))
    sspec = pl.BlockSpec((None, nq, wd, wd), lambda b, cc: (b, 0, 0, 0))
    consts = _wkv_consts(rev)
    y, sf = pl.pallas_call(
        functools.partial(_wkv_kernel, rev=rev, chunk=c, heads=g, nq=nq, need_y=need_y, group=WKV_GROUP),
        out_shape=[jax.ShapeDtypeStruct((n, d), F32), jax.ShapeDtypeStruct((bsz, nq, wd, wd), F32)],
        grid=(bsz, nc),
        in_specs=[tspec] * 5 + [full((1, d)), full((1, d)), sspec,
                                full((rn, wd)), full((c, rn)), full((c, rn)), full((c, c)), full((wd, wd)),
                                full((c, rn)), full((int(math.log2(c)), c, rn))],
        out_specs=[tspec, sspec],
        scratch_shapes=[pltpu.VMEM((nq, wd, wd), F32)],
        compiler_params=_cparams(("arbitrary", "arbitrary")),
        name="wkv_scan",
    )(r, k, v, lw, al, kkp, kap, s0, *consts)
    return y, sf


def _rwkv_out_kernel(yf_ref, yb_ref, bon_ref, g_ref, x_ref, gt_ref, lnw_ref, lnb_ref, ones_ref, wo_ref, o_ref,
                     *, wd):
    y = yf_ref[...] + yb_ref[...]
    ones = ones_ref[...]
    d = y.shape[1]
    inv = 1.0 / RWKV_HEAD

    def headsum(z):
        return jnp.concatenate([_dot_x2(z[:, q * wd:(q + 1) * wd], ones) for q in range(d // wd)], axis=1)

    dev = y - headsum(y) * inv
    var = headsum(dev * dev) * inv
    yn = dev * lax.rsqrt(var + GN_EPS) * lnw_ref[...] + lnb_ref[...]
    o = (yn + bon_ref[...].astype(F32)) * g_ref[...].astype(F32)
    o_ref[...] = x_ref[...] + gt_ref[...] * _dot(o.astype(BF16), wo_ref[...])


def _rwkv_out(yf, yb, bon, g, x, mods, row_of_tile, lnw, lnb, wo, *, tm):
    n, d = x.shape
    wd = WKV_HEADS_PER_BLOCK * RWKV_HEAD
    hl = np.arange(wd) // RWKV_HEAD
    ones = jnp.asarray(hl[:, None] == hl[None, :], BF16)
    tspec = pl.BlockSpec((tm, d), lambda i: (i, 0))
    full = lambda shape: pl.BlockSpec(shape, lambda i: (0,) * len(shape))
    return pl.pallas_call(
        functools.partial(_rwkv_out_kernel, wd=wd),
        out_shape=jax.ShapeDtypeStruct((n, d), F32),
        grid=(n // tm,),
        in_specs=[tspec] * 5 + [pl.BlockSpec((None, 1, d), lambda i: (row_of_tile(i) * N_MOD + 5, 0, 0)),
                                full((1, d)), full((1, d)), full((wd, wd)), full((d, d))],
        out_specs=tspec,
        compiler_params=_cparams(("arbitrary",)),
        name="rwkv_out",
    )(yf, yb, bon, g, x, mods, lnw, lnb, ones, wo)


def _rwkv_params(j, d, rwkv_mu, rwkv_w_rkv, rwkv_w0, rwkv_w1, rwkv_w2, rwkv_a0, rwkv_a1, rwkv_a2, rwkv_g1,
                 rwkv_g2, rwkv_k_a, rwkv_r_k):
    lr = rwkv_g1.shape[2]
    rank = rwkv_w1.shape[3]
    assert 2 * rank <= lr

    def first(w):
        return jnp.pad(jnp.concatenate([w[0], w[1]], axis=1), ((0, 0), (0, lr - 2 * rank))).astype(BF16)

    def second(w, z):
        return jnp.pad(w, ((z * rank, lr - (z + 1) * rank), (0, 0))).astype(BF16)

    vec = jnp.stack([rwkv_w0[j, 0], rwkv_w0[j, 1], rwkv_a0[j, 0], rwkv_a0[j, 1], rwkv_k_a[j],
                     rwkv_r_k[j].reshape(d), jnp.zeros((d,), F32), jnp.zeros((d,), F32)])
    hl = np.arange(256) // RWKV_HEAD
    return dict(
        mu=jnp.pad(rwkv_mu[j], ((0, 2), (0, 0))),
        w1c=first(rwkv_w1[j]), a1c=first(rwkv_a1[j]), g1=rwkv_g1[j].astype(BF16),
        wr=rwkv_w_rkv[j, 0].astype(BF16), wk=rwkv_w_rkv[j, 1].astype(BF16), wv=rwkv_w_rkv[j, 2].astype(BF16),
        w2f=second(rwkv_w2[j, 0], 0), w2b=second(rwkv_w2[j, 1], 1),
        a2f=second(rwkv_a2[j, 0], 0), a2b=second(rwkv_a2[j, 1], 1),
        g2=rwkv_g2[j].astype(BF16), vec=vec,
        ones_head=jnp.asarray(hl[:, None] == hl[None, :], BF16),
    )


def kernel(x, c, ctx, c_ctx, mod_w, mod_b, norm_w, ffn_w13, ffn_w2, fnet_w_o, fnet_b_o, rwkv_mu, rwkv_w_rkv,
           rwkv_w0, rwkv_w1, rwkv_w2, rwkv_a0, rwkv_a1, rwkv_a2, rwkv_g1, rwkv_g2, rwkv_k_k, rwkv_k_a, rwkv_r_k,
           rwkv_ln_w, rwkv_ln_b, rwkv_w_o, final_norm_w):
    bsz, seq, d = x.shape
    lc = ctx.shape[1]
    depth = mod_w.shape[0]
    assert depth == 2 and bsz == 2, "layer schedule below is written for the two-layer, batch-2 trunk"
    tm = TOKEN_TILE
    assert seq % tm == 0 and seq % GRID_W == 0

    c8 = jnp.concatenate([c, c_ctx[None], jnp.zeros((8 - bsz - 1, d), F32)], axis=0)
    mods_all = _adaln(c8, mod_w, mod_b)
    f2 = ffn_w13.shape[-1]
    w13 = ffn_w13.astype(BF16).reshape(2 * depth, d, f2)
    w2 = ffn_w2.astype(BF16).reshape(2 * depth, f2 // 2, d)
    fw = final_norm_w.reshape(1, d)

    xl = x.reshape(bsz * seq, d)
    xc = ctx.reshape(bsz * lc, d)
    lat_row = lambda i: i // (seq // tm)
    ctx_row = lambda i: 2

    mods = mods_all[0, :3].reshape(3 * N_MOD, 1, d)
    nw = norm_w[0].reshape(3, 1, d)
    xl, hl = _ffn(xl, mods, lat_row, 0, nw[0], w13, w2, 0, fw, tm=tm, next_nw=nw[1])
    xc, hc = _ffn(xc, mods, ctx_row, 0, nw[0], w13, w2, 0, fw, tm=lc, next_nw=nw[1])
    wo = fnet_w_o[0].astype(BF16)
    bo = fnet_b_o[0].reshape(1, d)
    xl = _fourier_latent(xl, hl, bsz, seq, mods, wo, bo)
    xc = _fourier_ctx(xc, hc, bsz, lc, mods, wo, bo)
    xl = _ffn(xl, mods, lat_row, 2, nw[2], w13, w2, 1, fw, tm=tm)
    xc = _ffn(xc, mods, ctx_row, 2, nw[2], w13, w2, 1, fw, tm=lc)

    mods = mods_all[1, :3].reshape(3 * N_MOD, 1, d)
    nw = norm_w[1].reshape(3, 1, d)
    xl = _ffn(xl, mods, lat_row, 0, nw[0], w13, w2, 2, fw, tm=tm)
    xc = _ffn(xc, mods, ctx_row, 0, nw[0], w13, w2, 2, fw, tm=lc)
    prm = _rwkv_params(0, d, rwkv_mu, rwkv_w_rkv, rwkv_w0, rwkv_w1, rwkv_w2, rwkv_a0, rwkv_a1, rwkv_a2, rwkv_g1,
                       rwkv_g2, rwkv_k_a, rwkv_r_k)
    r_c, k_c, v_c, _, lwf_c, lwb_c, alf_c, alb_c, _ = _rwkv_proj(
        xc, mods, ctx_row, nw[1], prm, tm=lc, grid_mode=False, tpb=1)
    r_l, k_l, v_l, g_l, lwf_l, lwb_l, alf_l, alb_l, bon_l = _rwkv_proj(
        xl, mods, lat_row, nw[1], prm, tm=tm, grid_mode=True, tpb=seq // tm)
    kkp = rwkv_k_k[0].reshape(1, d)
    kap = rwkv_k_a[0].reshape(1, d)
    wd = WKV_HEADS_PER_BLOCK * RWKV_HEAD
    s0 = jnp.zeros((bsz, d // wd, wd, wd), F32)
    _, s_f = _wkv(r_c, k_c, v_c, lwf_c, alf_c, kkp, kap, s0, bsz, rev=False, need_y=False)
    _, s_b = _wkv(r_c, k_c, v_c, lwb_c, alb_c, kkp, kap, s0, bsz, rev=True, need_y=False)
    yf, _ = _wkv(r_l, k_l, v_l, lwf_l, alf_l, kkp, kap, s_f, bsz, rev=False, need_y=True)
    yb, _ = _wkv(r_l, k_l, v_l, lwb_l, alb_l, kkp, kap, s_b, bsz, rev=True, need_y=True)
    xl = _rwkv_out(yf, yb, bon_l, g_l, xl, mods, lambda i: i // (seq // MIX_TILE), rwkv_ln_w[0].reshape(1, d),
                   rwkv_ln_b[0].reshape(1, d), rwkv_w_o[0].astype(BF16), tm=MIX_TILE)
    xl = _ffn(xl, mods, lat_row, 2, nw[2], w13, w2, 3, fw, tm=tm, final_norm=True)
    return xl.reshape(bsz, seq, d)
```

```python
import functools
import math

import numpy as np
import jax
import jax.numpy as jnp
from jax import lax
from jax.experimental import pallas as pl
from jax.experimental.pallas import tpu as pltpu

F32 = jnp.float32
BF16 = jnp.bfloat16

NORM_EPS = 1e-6
GN_EPS = 64e-5
KK_EPS = 1e-24
GRID_W = 64
FNET_GROUPS = 8
RWKV_HEAD = 64
N_MOD = 9
DECAY_SCALE = math.exp(-0.5)

LANE = 128
TOKEN_TILE = 512
MIX_TILE = 256
FFN_HIDDEN_TILE = 512
PROJ_COL_TILE = 256
WKV_CHUNK = 64
WKV_HEADS_PER_BLOCK = 2
WKV_GROUP = 16
VMEM_LIMIT = 56 * 1024 * 1024


def _cparams(sem):
    return pltpu.CompilerParams(dimension_semantics=sem, vmem_limit_bytes=VMEM_LIMIT)


def _sigmoid(x):
    return 1.0 / (1.0 + jnp.exp(-x))


def _modulate(x, nw, shift, scale):
    y = x * lax.rsqrt(jnp.mean(x * x, axis=-1, keepdims=True) + NORM_EPS)
    return (y * nw) * (1.0 + scale) + shift


def _dot(a, b):
    return jnp.dot(a, b, preferred_element_type=F32)


def _dot_nt(a, b):
    return lax.dot_general(a, b, (((1,), (1,)), ((), ())), preferred_element_type=F32)


def _split2(x):
    hi = x.astype(BF16)
    lo = (x - hi.astype(F32)).astype(BF16)
    return hi, lo


def _dot_x2(x, w_bf16):
    hi, lo = _split2(x)
    return _dot(hi, w_bf16) + _dot(lo, w_bf16)


def _adaln_kernel(c_ref, w_ref, b_ref, o_ref):
    c = c_ref[...]
    s = c * _sigmoid(c)
    o_ref[...] = _dot(s.astype(BF16), w_ref[...].astype(BF16)) + b_ref[...]


def _adaln(c8, mod_w, mod_b):
    depth, d, nd = mod_w.shape
    tn = 1024 if nd % 1024 == 0 else 512
    return pl.pallas_call(
        _adaln_kernel,
        out_shape=jax.ShapeDtypeStruct((depth, 8, nd), F32),
        grid=(depth, nd // tn),
        in_specs=[
            pl.BlockSpec((8, d), lambda l, j: (0, 0)),
            pl.BlockSpec((None, d, tn), lambda l, j: (l, 0, j)),
            pl.BlockSpec((None, 1, tn), lambda l, j: (l, 0, j)),
        ],
        out_specs=pl.BlockSpec((None, 8, tn), lambda l, j: (l, 0, j)),
        compiler_params=_cparams(("arbitrary", "arbitrary")),
        name="adaln_linear",
    )(c8, mod_w, mod_b.reshape(depth, 1, nd))


def _ffn_kernel(*refs, final_norm, emit_h):
    x_ref, sh_ref, sc_ref, gt_ref, nw_ref, w1g_ref, w1u_ref, w2_ref, fw_ref = refs[:9]
    if emit_h:
        nw2_ref, sh2_ref, sc2_ref, o_ref, h2_ref, h_ref = refs[9:]
    else:
        o_ref, h_ref = refs[9:]
    j = pl.program_id(1)

    @pl.when(j == 0)
    def _():
        h_ref[...] = _modulate(x_ref[...], nw_ref[...], sh_ref[...], sc_ref[...]).astype(BF16)
        o_ref[...] = jnp.zeros_like(o_ref)

    h = h_ref[...]
    g = _dot(h, w1g_ref[...])
    u = _dot(h, w1u_ref[...])
    a = (g * _sigmoid(g)) * u
    o_ref[...] += _dot(a.astype(BF16), w2_ref[...])

    @pl.when(j == pl.num_programs(1) - 1)
    def _():
        y = x_ref[...] + (0.5 * gt_ref[...]) * o_ref[...]
        if final_norm:
            y = y * lax.rsqrt(jnp.mean(y * y, axis=-1, keepdims=True) + NORM_EPS) * fw_ref[...]
        o_ref[...] = y
        if emit_h:
            h2_ref[...] = _modulate(y, nw2_ref[...], sh2_ref[...], sc2_ref[...])


def _ffn(x, mods, row_of_tile, sub, nw, w13, w2, wi, fw, *, tm, final_norm=False, next_nw=None):
    n, d = x.shape
    f = w2.shape[1]
    tf = FFN_HIDDEN_TILE
    nf = f // tf
    emit_h = next_nw is not None

    def mod_spec(k):
        return pl.BlockSpec((None, 1, d), lambda i, j: (row_of_tile(i) * N_MOD + 3 * sub + k, 0, 0))

    vec_spec = pl.BlockSpec((1, d), lambda i, j: (0, 0))
    tile_spec = pl.BlockSpec((tm, d), lambda i, j: (i, 0))
    out = jax.ShapeDtypeStruct((n, d), F32)
    args = [x, mods, mods, mods, nw, w13, w13, w2, fw]
    specs = [tile_spec, mod_spec(0), mod_spec(1), mod_spec(2), vec_spec,
             pl.BlockSpec((None, d, tf), lambda i, j: (wi, 0, j)),
             pl.BlockSpec((None, d, tf), lambda i, j: (wi, 0, nf + j)),
             pl.BlockSpec((None, tf, d), lambda i, j: (wi, j, 0)),
             vec_spec]
    if emit_h:
        args += [next_nw, mods, mods]
        specs += [vec_spec, mod_spec(3), mod_spec(4)]
    return pl.pallas_call(
        functools.partial(_ffn_kernel, final_norm=final_norm, emit_h=emit_h),
        out_shape=[out, out] if emit_h else out,
        grid=(n // tm, nf),
        in_specs=specs,
        out_specs=[tile_spec, tile_spec] if emit_h else tile_spec,
        scratch_shapes=[pltpu.VMEM((tm, d), BF16)],
        compiler_params=_cparams(("arbitrary", "arbitrary")),
        name="ffn_swiglu",
    )(*args)


def _dft_tables(n):
    k = np.arange(n, dtype=np.int64)
    ang = 2.0 * np.pi * ((k[:, None] * k[None, :]) % n).astype(np.float64) / n
    return np.cos(ang), np.sin(ang)


def _hi_lo(m):
    m = jnp.asarray(m, F32)
    hi = m.astype(BF16)
    return hi, (m - hi.astype(F32)).astype(BF16)


DFT_SUB = 8
DFT_COLS = 1024
DFT_COLS_B = 512


def _dft_kernel(*refs, n_planes, mode, twiddle):
    it = iter(refs)
    x_ref = next(it)
    mats = [(next(it)[...], next(it)[...]) for _ in range(n_planes)]
    if twiddle:
        tc_ref, ts_ref = next(it), next(it)
    o_ref = next(it)
    if mode != "flat":
        stage_ref = next(it)

    def dense(p, strided):
        stage_ref[p] = strided
        return stage_ref[p]

    def transform(planes):
        y = None
        for x, (mh, ml) in zip(planes, mats):
            xh, xl = _split2(x)
            part = _dot(mh, xh) + (_dot(mh, xl) + _dot(ml, xh))
            y = part if y is None else y + part
        half = y.shape[0] // 2
        return y[:half], y[half:]

    if mode == "flat":
        yr, yi = transform([x_ref[...]])
        o_ref[0] = yr
        o_ref[1] = yi
    elif mode == "a":
        for j in range(DFT_SUB):
            yr, yi = transform([dense(0, x_ref[:, j, :])])
            if twiddle:
                reps = yr.shape[1] // LANE
                c = jnp.concatenate([tc_ref[j]] * reps, axis=1)
                s = jnp.concatenate([ts_ref[j]] * reps, axis=1)
                yr, yi = yr * c + yi * s, yi * c - yr * s
            o_ref[0, j] = yr
            o_ref[1, j] = yi
    else:
        for j in range(DFT_SUB):
            yr, yi = transform([dense(0, x_ref[0, :, j, :]), dense(1, x_ref[1, :, j, :])])
            o_ref[0, :, j, :] = yr
            o_ref[1, :, j, :] = yi


def _dft_call(mode, x, mats, out_shape, grid, x_spec, o_spec, tw=None, tw_spec=None):
    args, specs = [x], [x_spec]
    for m in mats:
        mh, ml = _hi_lo(m)
        args += [mh, ml]
        specs += [pl.BlockSpec(m.shape, lambda *_: (0, 0))] * 2
    if tw is not None:
        args += list(tw)
        specs += [tw_spec] * 2
    scratch = []
    if mode != "flat":
        scratch = [pltpu.VMEM((len(mats), mats[0].shape[1], x_spec.block_shape[-1]), F32)]
    return pl.pallas_call(
        functools.partial(_dft_kernel, n_planes=len(mats), mode=mode, twiddle=tw is not None),
        out_shape=jax.ShapeDtypeStruct(out_shape, F32),
        grid=grid,
        in_specs=specs,
        out_specs=o_spec,
        scratch_shapes=scratch,
        compiler_params=_cparams(("arbitrary",) * len(grid)),
        name="dft_" + mode,
    )(*args)


def _fmix_kernel(pr_ref, pi_ref, x_ref, gt_ref, ch_ref, cl_ref, wo_ref, bo_ref, o_ref, *, groups):
    pr, pi = pr_ref[...], pi_ref[...]
    gd = pr.shape[1] // groups
    ch, cl = ch_ref[...], cl_ref[...]
    outs = []
    for g in range(groups):
        z = jnp.concatenate([pr[:, g * gd:(g + 1) * gd], pi[:, g * gd:(g + 1) * gd]], axis=1)
        zh, zl = _split2(z)
        outs.append(_dot(zh, ch) + (_dot(zl, ch) + _dot(zh, cl)))
    f = jnp.concatenate(outs, axis=1)
    o = _dot(f.astype(BF16), wo_ref[...]) + bo_ref[...]
    o_ref[...] = x_ref[...] + gt_ref[...] * o


def _fmix(p, x, mods, row_of_tile, wo, bo, *, tm):
    n, d = x.shape
    b, _, l, _ = p.shape
    tpb = l // tm
    gd = d // FNET_GROUPS
    c, s = _dft_tables(gd)
    ch, cl = _hi_lo(np.concatenate([c, s], axis=0) / math.sqrt(gd))
    return pl.pallas_call(
        functools.partial(_fmix_kernel, groups=FNET_GROUPS),
        out_shape=jax.ShapeDtypeStruct((n, d), F32),
        grid=(n // tm,),
        in_specs=[
            pl.BlockSpec((None, None, tm, d), lambda i: (i // tpb, 0, i % tpb, 0)),
            pl.BlockSpec((None, None, tm, d), lambda i: (i // tpb, 1, i % tpb, 0)),
            pl.BlockSpec((tm, d), lambda i: (i, 0)),
            pl.BlockSpec((None, 1, d), lambda i: (row_of_tile(i) * N_MOD + 5, 0, 0)),
            pl.BlockSpec((2 * gd, gd), lambda i: (0, 0)),
            pl.BlockSpec((2 * gd, gd), lambda i: (0, 0)),
            pl.BlockSpec((d, d), lambda i: (0, 0)),
            pl.BlockSpec((1, d), lambda i: (0, 0)),
        ],
        out_specs=pl.BlockSpec((tm, d), lambda i: (i, 0)),
        compiler_params=_cparams(("arbitrary",)),
        name="fourier_out",
    )(p, p, x, mods, ch, cl, wo, bo)


def _fourier_latent(xl, hl, bsz, seq, mods, wo, bo):
    n, d = xl.shape
    la = lb = int(round(math.sqrt(seq)))
    assert la * lb == seq and lb % DFT_SUB == 0 and la % DFT_SUB == 0
    td = min(DFT_COLS, d)
    tdb = min(DFT_COLS_B, d)
    s8 = DFT_SUB
    ca, sa = _dft_tables(la)
    m_a = np.concatenate([ca, -sa], axis=0) / math.sqrt(la)
    cb, sb = _dft_tables(lb)
    m_br = np.concatenate([cb, -sb], axis=0) / math.sqrt(lb)
    m_bi = np.concatenate([sb, cb], axis=0) / math.sqrt(lb)
    n2 = np.arange(lb, dtype=np.int64)[:, None]
    k1 = np.arange(la, dtype=np.int64)[None, :]
    ang = 2.0 * np.pi * ((n2 * k1) % seq).astype(np.float64) / seq
    twc = jnp.asarray(np.broadcast_to(np.cos(ang)[:, :, None], (lb, la, LANE)), F32)
    tws = jnp.asarray(np.broadcast_to(np.sin(ang)[:, :, None], (lb, la, LANE)), F32)
    grid = (bsz, lb // s8, d // td)
    a = _dft_call(
        "a", hl.reshape(bsz, la, lb, d), [m_a], (bsz, 2, lb, la, d), grid,
        pl.BlockSpec((None, la, s8, td), lambda b, g, c: (b, 0, g, c)),
        pl.BlockSpec((None, 2, s8, la, td), lambda b, g, c: (b, 0, g, 0, c)),
        tw=(twc, tws), tw_spec=pl.BlockSpec((s8, la, LANE), lambda b, g, c: (g, 0, 0)))
    p = _dft_call(
        "b", a, [m_br, m_bi], (bsz, 2, lb, la, d), (bsz, la // s8, d // tdb),
        pl.BlockSpec((None, 2, lb, s8, tdb), lambda b, g, c: (b, 0, 0, g, c)),
        pl.BlockSpec((None, 2, lb, s8, tdb), lambda b, g, c: (b, 0, 0, g, c)))
    p = p.reshape(bsz, 2, seq, d)
    return _fmix(p, xl, mods, lambda i: i // (seq // MIX_TILE), wo, bo, tm=MIX_TILE)


def _fourier_ctx(xc, hc, bsz, lc, mods, wo, bo):
    n, d = xc.shape
    td = min(DFT_COLS, d)
    c, s = _dft_tables(lc)
    m = np.concatenate([c, -s], axis=0) / math.sqrt(lc)
    p = _dft_call(
        "flat", hc.reshape(bsz, lc, d), [m], (bsz, 2, lc, d), (bsz, d // td),
        pl.BlockSpec((None, lc, td), lambda b, c: (b, 0, c)),
        pl.BlockSpec((None, 2, lc, td), lambda b, c: (b, 0, 0, c)))
    return _fmix(p, xc, mods, lambda i: 2, wo, bo, tm=lc)


def _rwkv_proj_kernel(*refs, grid_mode, tpb):
    it = iter(refs)
    x_ref = next(it)
    if grid_mode:
        xp_ref, xn_ref = next(it), next(it)
    nw_ref, sh_ref, sc_ref, mu_ref = next(it), next(it), next(it), next(it)
    w1_ref, a1_ref, g1_ref = next(it), next(it), next(it)
    wr_ref, wk_ref, wv_ref = next(it), next(it), next(it)
    w2f_ref, w2b_ref, a2f_ref, a2b_ref, g2_ref = next(it), next(it), next(it), next(it), next(it)
    vec_ref, ones_ref = next(it), next(it)
    r_ref, k_ref, v_ref, g_ref, lwf_ref, lwb_ref, alf_ref, alb_ref, bon_ref = (next(it) for _ in range(9))
    hr_ref, hk_ref, hv_ref, tw_ref, ta_ref, tg_ref = (next(it) for _ in range(6))

    i = pl.program_id(0)
    j = pl.program_id(1)

    @pl.when(j == 0)
    def _():
        nw, sh, sc = nw_ref[...], sh_ref[...], sc_ref[...]
        h = _modulate(x_ref[...], nw, sh, sc)
        tm, d = h.shape
        q = d // 4
        row = lax.broadcasted_iota(jnp.int32, (tm, 1), 0)
        if grid_mode:
            tib = i % tpb
            up_ok = (tib != 0).astype(F32)
            dn_ok = (tib != tpb - 1).astype(F32)
            hp = _modulate(xp_ref[...], nw, sh, sc)[:, 2 * q:3 * q] * up_ok
            hn = _modulate(xn_ref[...], nw, sh, sc)[:, 3 * q:] * dn_ok
            col = row % GRID_W
            left = jnp.where(col != 0, pltpu.roll(h[:, :q], 1, 0), 0.0)
            right = jnp.where(col != GRID_W - 1, pltpu.roll(h[:, q:2 * q], tm - 1, 0), 0.0)
            up = jnp.concatenate([hp, h[:tm - GRID_W, 2 * q:3 * q]], axis=0)
            down = jnp.concatenate([h[GRID_W:, 3 * q:], hn], axis=0)
            hs = jnp.concatenate([left, right, up, down], axis=1)
        else:
            prev = jnp.where(row != 0, pltpu.roll(h, 1, 0), 0.0)
            nxt = jnp.where(row != tm - 1, pltpu.roll(h, tm - 1, 0), 0.0)
            hs = jnp.concatenate([prev[:, :q], nxt[:, q:2 * q], prev[:, 2 * q:3 * q], nxt[:, 3 * q:]], axis=1)
        dlt = hs - h
        mu = mu_ref[...]
        hr_ref[...] = (h + dlt * mu[0:1]).astype(BF16)
        hk_ref[...] = (h + dlt * mu[1:2]).astype(BF16)
        hv_ref[...] = (h + dlt * mu[2:3]).astype(BF16)
        tw_ref[...] = jnp.tanh(_dot((h + dlt * mu[3:4]).astype(BF16), w1_ref[...])).astype(BF16)
        ta_ref[...] = _dot((h + dlt * mu[4:5]).astype(BF16), a1_ref[...]).astype(BF16)
        tg_ref[...] = _sigmoid(_dot((h + dlt * mu[5:6]).astype(BF16), g1_ref[...])).astype(BF16)

    vec = vec_ref[...]
    w0f, w0b, a0f, a0b, ka, rk = (vec[n:n + 1] for n in range(6))
    r = _dot(hr_ref[...], wr_ref[...])
    k = _dot(hk_ref[...], wk_ref[...])
    v = _dot(hv_ref[...], wv_ref[...])
    tw = tw_ref[...]
    ta = ta_ref[...]
    lwf = -DECAY_SCALE * _sigmoid(w0f + _dot(tw, w2f_ref[...]))
    lwb = -DECAY_SCALE * _sigmoid(w0b + _dot(tw, w2b_ref[...]))
    alf = _sigmoid(a0f + _dot(ta, a2f_ref[...]))
    alb = _sigmoid(a0b + _dot(ta, a2b_ref[...]))
    kb = k * (1.0 + (0.5 * (alf + alb) - 1.0) * ka)
    r_ref[...] = r.astype(r_ref.dtype)
    k_ref[...] = k.astype(k_ref.dtype)
    v_ref[...] = v.astype(v_ref.dtype)
    g_ref[...] = _dot(tg_ref[...], g2_ref[...]).astype(g_ref.dtype)
    lwf_ref[...] = lwf
    lwb_ref[...] = lwb
    alf_ref[...] = alf.astype(alf_ref.dtype)
    alb_ref[...] = alb.astype(alb_ref.dtype)
    bon_ref[...] = (_dot_x2(r * kb * rk, ones_ref[...]) * v).astype(bon_ref.dtype)


def _rwkv_proj(x, mods, row_of_tile, nw, prm, *, tm, grid_mode, tpb):
    n, d = x.shape
    tn = PROJ_COL_TILE
    nj = d // tn
    hw = GRID_W
    lr = prm["w1c"].shape[1]

    def mod_spec(k):
        return pl.BlockSpec((None, 1, d), lambda i, j: (row_of_tile(i) * N_MOD + 3 + k, 0, 0))

    args, specs = [x], [pl.BlockSpec((tm, d), lambda i, j: (i, 0))]
    if grid_mode:
        nb = n // hw
        r = tm // hw
        args += [x, x]
        specs += [pl.BlockSpec((hw, d), lambda i, j: (jnp.maximum(i * r - 1, 0), 0)),
                  pl.BlockSpec((hw, d), lambda i, j: (jnp.minimum((i + 1) * r, nb - 1), 0))]
    args += [nw, mods, mods, prm["mu"], prm["w1c"], prm["a1c"], prm["g1"], prm["wr"], prm["wk"], prm["wv"],
             prm["w2f"], prm["w2b"], prm["a2f"], prm["a2b"], prm["g2"], prm["vec"], prm["ones_head"]]
    full = lambda shape: pl.BlockSpec(shape, lambda i, j: (0,) * len(shape))
    coltile = lambda rows: pl.BlockSpec((rows, tn), lambda i, j: (0, j))
    specs += [full((1, d)), mod_spec(0), mod_spec(1), full((8, d)), full((d, lr)), full((d, lr)), full((d, lr)),
              coltile(d), coltile(d), coltile(d), coltile(lr), coltile(lr), coltile(lr), coltile(lr), coltile(lr),
              coltile(8), full((tn, tn))]
    lo, hi = jax.ShapeDtypeStruct((n, d), BF16), jax.ShapeDtypeStruct((n, d), F32)
    return pl.pallas_call(
        functools.partial(_rwkv_proj_kernel, grid_mode=grid_mode, tpb=tpb),
        out_shape=[lo, lo, lo, lo, hi, hi, lo, lo, lo],
        grid=(n // tm, nj),
        in_specs=specs,
        out_specs=[pl.BlockSpec((tm, tn), lambda i, j: (i, j))] * 9,
        scratch_shapes=[pltpu.VMEM((tm, d), BF16)] * 3 + [pltpu.VMEM((tm, lr), BF16)] * 3,
        compiler_params=_cparams(("arbitrary", "arbitrary")),
        name="rwkv_proj",
    )(*args)


def _wkv_kernel(r_ref, k_ref, v_ref, lw_ref, al_ref, kk_ref, ka_ref, s0_ref, bm_ref, ms_ref, mi_ref, tri_ref,
                ones_ref, eye_ref, lvl_ref, y_ref, sf_ref, s_scr, *, rev, chunk, heads, nq, need_y, group):
    cc = pl.program_id(1)
    c = chunk
    wd = heads * RWKV_HEAD
    rn = heads * c

    @pl.when(cc == 0)
    def _():
        s_scr[...] = s0_ref[...]

    bm = bm_ref[...]
    bmf = bm.astype(F32)
    strict = ms_ref[...] > 0.0
    incl = mi_ref[...] > 0.0
    tri = tri_ref[...]
    ones = ones_ref[...]
    eye = eye_ref[...]

    def stack(x):
        return jnp.concatenate([x.astype(BF16)] * heads, axis=0) * bm

    if not need_y:
        y_ref[...] = jnp.zeros_like(y_ref)

    def block(q):
        sl = slice(q * wd, (q + 1) * wd)
        r, k, v = r_ref[:, sl].astype(F32), k_ref[:, sl].astype(F32), v_ref[:, sl].astype(F32)
        lw, al = lw_ref[:, sl], al_ref[:, sl].astype(F32)
        kk0 = k * kk_ref[:, sl]
        kk = kk0 * lax.rsqrt(jnp.maximum(_dot_x2(kk0 * kk0, ones), KK_EPS))
        b = kk * al
        kd = k * (1.0 + (al - 1.0) * ka_ref[:, sl])
        lw_hi, lw_lo = _split2(lw)
        lg = _dot(tri, lw_hi) + _dot(tri, lw_lo)
        gc = lg[0:1] if rev else lg[c - 1:c]
        gin = jnp.exp(-lg)
        gout = jnp.exp(gc - lg)
        a_n = -kk * jnp.exp(lg - lw)
        r_n = r * jnp.exp(lg)
        a_s = stack(a_n)
        v_s = stack(v)
        bk = jnp.concatenate([stack(b * gin), stack(kd * gin)], axis=0)
        bo_n = (b * gout).astype(BF16)
        ko_n = (kd * gout).astype(BF16)
        yield

        lhs = jnp.concatenate([a_n, r_n], axis=0) if need_y else a_n
        gm = _dot_nt(lhs.astype(BF16), bk)
        gab = gm[:c, :rn]
        aak = jnp.where(strict, gm[:c, rn:], 0.0).astype(BF16)
        if need_y:
            arb = jnp.where(incl, gm[c:, :rn], 0.0).astype(BF16)
            ark = jnp.where(incl, gm[c:, rn:], 0.0).astype(BF16)
        yield
        wy = _dot(jnp.concatenate([aak, ark], axis=0) if need_y else aak, v_s)
        w0 = wy[:c]

        t = eye + jnp.where(lvl_ref[0] > 0.0, gab, 0.0)
        for lvl in range(1, int(math.log2(c))):
            n_l = jnp.where(lvl_ref[lvl] > 0.0, gab, 0.0)
            u = _dot(t.astype(BF16), stack(n_l))
            yield
            t = t + _dot(u.astype(BF16), stack(t))
            yield
        t = t.astype(BF16)
        if need_y:
            t = jnp.concatenate([t, _dot(arb, stack(t)).astype(BF16)], axis=0)
            yield

        tv = _dot(t, jnp.concatenate([a_s, stack(w0)], axis=1))
        av = tv[:c]
        yield
        s = s_scr[q]
        sb = s.astype(BF16)
        if need_y:
            ry = tv[c:]
            rh = r_n + ry[:, :wd]
            yk = wy[c:]
        pd = _dot(av.T.astype(BF16), bo_n)
        dt = (pd[wd:] + _dot(v.T.astype(BF16), ko_n)) * bmf
        pt = (pd[:wd] * bmf).astype(BF16)
        yield
        if need_y:
            y_ref[:, sl] = _dot_nt(rh.astype(BF16), sb) + ry[:, wd:] + yk
        s_scr[q] = s * jnp.exp(gc) + _dot(sb, pt) + dt

    for q0 in range(0, nq, group):
        live = [block(q) for q in range(q0, min(q0 + group, nq))]
        while live:
            nxt = []
            for gen in live:
                try:
                    next(gen)
                    nxt.append(gen)
                except StopIteration:
                    pass
            live = nxt

    @pl.when(cc == pl.num_programs(1) - 1)
    def _():
        sf_ref[...] = s_scr[...]


def _wkv_consts(rev):
    c, g = WKV_CHUNK, WKV_HEADS_PER_BLOCK
    wd, rn = g * RWKV_HEAD, g * c
    assert c == RWKV_HEAD, "one (rn, wd) mask serves both (head, time) and (head, channel) columns"
    hrow = np.arange(rn) // c
    hlane = np.arange(wd) // RWKV_HEAD
    bm = (hrow[:, None] == hlane[None, :]).astype(np.float32)
    t = np.arange(c)[:, None]
    src = (np.arange(rn) % c)[None, :]
    if rev:
        strict, incl = src > t, src >= t
        tri = np.triu(np.ones((c, c), np.float32))
    else:
        strict, incl = src < t, src <= t
        tri = np.tril(np.ones((c, c), np.float32))
    ones = (hlane[:, None] == hlane[None, :]).astype(np.float32)
    levels = [strict & ((t >> (l + 1)) == (src >> (l + 1))) & ((t >> l) != (src >> l))
              for l in range(int(math.log2(c)))]
    return (jnp.asarray(bm, BF16), jnp.asarray(strict, F32), jnp.asarray(incl, F32), jnp.asarray(tri, BF16),
            jnp.asarray(ones, BF16), jnp.asarray(src == t, F32), jnp.asarray(np.stack(levels), F32))


def _wkv(r, k, v, lw, al, kkp, kap, s0, bsz, *, rev, need_y):
    n, d = r.shape
    c, g = WKV_CHUNK, WKV_HEADS_PER_BLOCK
    wd, rn = g * RWKV_HEAD, g * c
    nq = d // wd
    nc = n // bsz // c
    if rev:
        tok = lambda b, cc: (b * nc + (nc - 1 - cc), 0)
    else:
        tok = lambda b, cc: (b * nc + cc, 0)
    tspec = pl.BlockSpec((c, d), tok)
    full = lambda shape: pl.BlockSpec(shape, lambda b, cc: (0,) * len(shape))
    sspec = pl.BlockSpec((None, nq, wd, wd), lambda b, cc: (b, 0, 0, 0))
    consts = _wkv_consts(rev)
    y, sf = pl.pallas_call(
        functools.partial(_wkv_kernel, rev=rev, chunk=c, heads=g, nq=nq, need_y=need_y, group=WKV_GROUP),
        out_shape=[jax.ShapeDtypeStruct((n, d), F32), jax.ShapeDtypeStruct((bsz, nq, wd, wd), F32)],
        grid=(bsz, nc),
        in_specs=[tspec] * 5 + [full((1, d)), full((1, d)), sspec,
                                full((rn, wd)), full((c, rn)), full((c, rn)), full((c, c)), full((wd, wd)),
                                full((c, rn)), full((int(math.log2(c)), c, rn))],
        out_specs=[tspec, sspec],
        scratch_shapes=[pltpu.VMEM((nq, wd, wd), F32)],
        compiler_params=_cparams(("arbitrary", "arbitrary")),
        name="wkv_scan",
    )(r, k, v, lw, al, kkp, kap, s0, *consts)
    return y, sf


def _rwkv_out_kernel(yf_ref, yb_ref, bon_ref, g_ref, x_ref, gt_ref, lnw_ref, lnb_ref, ones_ref, wo_ref, o_ref,
                     *, wd):
    y = yf_ref[...] + yb_ref[...]
    ones = ones_ref[...]
    d = y.shape[1]
    inv = 1.0 / RWKV_HEAD

    def headsum(z):
        return jnp.concatenate([_dot_x2(z[:, q * wd:(q + 1) * wd], ones) for q in range(d // wd)], axis=1)

    dev = y - headsum(y) * inv
    var = headsum(dev * dev) * inv
    yn = dev * lax.rsqrt(var + GN_EPS) * lnw_ref[...] + lnb_ref[...]
    o = (yn + bon_ref[...].astype(F32)) * g_ref[...].astype(F32)
    o_ref[...] = x_ref[...] + gt_ref[...] * _dot(o.astype(BF16), wo_ref[...])


def _rwkv_out(yf, yb, bon, g, x, mods, row_of_tile, lnw, lnb, wo, *, tm):
    n, d = x.shape
    wd = WKV_HEADS_PER_BLOCK * RWKV_HEAD
    hl = np.arange(wd) // RWKV_HEAD
    ones = jnp.asarray(hl[:, None] == hl[None, :], BF16)
    tspec = pl.BlockSpec((tm, d), lambda i: (i, 0))
    full = lambda shape: pl.BlockSpec(shape, lambda i: (0,) * len(shape))
    return pl.pallas_call(
        functools.partial(_rwkv_out_kernel, wd=wd),
        out_shape=jax.ShapeDtypeStruct((n, d), F32),
        grid=(n // tm,),
        in_specs=[tspec] * 5 + [pl.BlockSpec((None, 1, d), lambda i: (row_of_tile(i) * N_MOD + 5, 0, 0)),
                                full((1, d)), full((1, d)), full((wd, wd)), full((d, d))],
        out_specs=tspec,
        compiler_params=_cparams(("arbitrary",)),
        name="rwkv_out",
    )(yf, yb, bon, g, x, mods, lnw, lnb, ones, wo)


def _rwkv_params(j, d, rwkv_mu, rwkv_w_rkv, rwkv_w0, rwkv_w1, rwkv_w2, rwkv_a0, rwkv_a1, rwkv_a2, rwkv_g1,
                 rwkv_g2, rwkv_k_a, rwkv_r_k):
    lr = rwkv_g1.shape[2]
    rank = rwkv_w1.shape[3]
    assert 2 * rank <= lr

    def first(w):
        return jnp.pad(jnp.concatenate([w[0], w[1]], axis=1), ((0, 0), (0, lr - 2 * rank))).astype(BF16)

    def second(w, z):
        return jnp.pad(w, ((z * rank, lr - (z + 1) * rank), (0, 0))).astype(BF16)

    vec = jnp.stack([rwkv_w0[j, 0], rwkv_w0[j, 1], rwkv_a0[j, 0], rwkv_a0[j, 1], rwkv_k_a[j],
                     rwkv_r_k[j].reshape(d), jnp.zeros((d,), F32), jnp.zeros((d,), F32)])
    hl = np.arange(256) // RWKV_HEAD
    return dict(
        mu=jnp.pad(rwkv_mu[j], ((0, 2), (0, 0))),
        w1c=first(rwkv_w1[j]), a1c=first(rwkv_a1[j]), g1=rwkv_g1[j].astype(BF16),
        wr=rwkv_w_rkv[j, 0].astype(BF16), wk=rwkv_w_rkv[j, 1].astype(BF16), wv=rwkv_w_rkv[j, 2].astype(BF16),
        w2f=second(rwkv_w2[j, 0], 0), w2b=second(rwkv_w2[j, 1], 1),
        a2f=second(rwkv_a2[j, 0], 0), a2b=second(rwkv_a2[j, 1], 1),
        g2=rwkv_g2[j].astype(BF16), vec=vec,
        ones_head=jnp.asarray(hl[:, None] == hl[None, :], BF16),
    )


def kernel(x, c, ctx, c_ctx, mod_w, mod_b, norm_w, ffn_w13, ffn_w2, fnet_w_o, fnet_b_o, rwkv_mu, rwkv_w_rkv,
           rwkv_w0, rwkv_w1, rwkv_w2, rwkv_a0, rwkv_a1, rwkv_a2, rwkv_g1, rwkv_g2, rwkv_k_k, rwkv_k_a, rwkv_r_k,
           rwkv_ln_w, rwkv_ln_b, rwkv_w_o, final_norm_w):
    bsz, seq, d = x.shape
    lc = ctx.shape[1]
    depth = mod_w.shape[0]
    assert depth == 2 and bsz == 2, "layer schedule below is written for the two-layer, batch-2 trunk"
    tm = TOKEN_TILE
    assert seq % tm == 0 and seq % GRID_W == 0

    c8 = jnp.concatenate([c, c_ctx[None], jnp.zeros((8 - bsz - 1, d), F32)], axis=0)
    mods_all = _adaln(c8, mod_w, mod_b)
    f2 = ffn_w13.shape[-1]
    w13 = ffn_w13.astype(BF16).reshape(2 * depth, d, f2)
    w2 = ffn_w2.astype(BF16).reshape(2 * depth, f2 // 2, d)
    fw = final_norm_w.reshape(1, d)

    xl = x.reshape(bsz * seq, d)
    xc = ctx.reshape(bsz * lc, d)
    lat_row = lambda i: i // (seq // tm)
    ctx_row = lambda i: 2

    mods = mods_all[0, :3].reshape(3 * N_MOD, 1, d)
    nw = norm_w[0].reshape(3, 1, d)
    xl, hl = _ffn(xl, mods, lat_row, 0, nw[0], w13, w2, 0, fw, tm=tm, next_nw=nw[1])
    xc, hc = _ffn(xc, mods, ctx_row, 0, nw[0], w13, w2, 0, fw, tm=lc, next_nw=nw[1])
    wo = fnet_w_o[0].astype(BF16)
    bo = fnet_b_o[0].reshape(1, d)
    xl = _fourier_latent(xl, hl, bsz, seq, mods, wo, bo)
    xc = _fourier_ctx(xc, hc, bsz, lc, mods, wo, bo)
    xl = _ffn(xl, mods, lat_row, 2, nw[2], w13, w2, 1, fw, tm=tm)
    xc = _ffn(xc, mods, ctx_row, 2, nw[2], w13, w2, 1, fw, tm=lc)

    mods = mods_all[1, :3].reshape(3 * N_MOD, 1, d)
    nw = norm_w[1].reshape(3, 1, d)
    xl = _ffn(xl, mods, lat_row, 0, nw[0], w13, w2, 2, fw, tm=tm)
    xc = _ffn(xc, mods, ctx_row, 0, nw[0], w13, w2, 2, fw, tm=lc)
    prm = _rwkv_params(0, d, rwkv_mu, rwkv_w_rkv, rwkv_w0, rwkv_w1, rwkv_w2, rwkv_a0, rwkv_a1, rwkv_a2, rwkv_g1,
                       rwkv_g2, rwkv_k_a, rwkv_r_k)
    r_c, k_c, v_c, _, lwf_c, lwb_c, alf_c, alb_c, _ = _rwkv_proj(
        xc, mods, ctx_row, nw[1], prm, tm=lc, grid_mode=False, tpb=1)
    r_l, k_l, v_l, g_l, lwf_l, lwb_l, alf_l, alb_l, bon_l = _rwkv_proj(
        xl, mods, lat_row, nw[1], prm, tm=tm, grid_mode=True, tpb=seq // tm)
    kkp = rwkv_k_k[0].reshape(1, d)
    kap = rwkv_k_a[0].reshape(1, d)
    wd = WKV_HEADS_PER_BLOCK * RWKV_HEAD
    s0 = jnp.zeros((bsz, d // wd, wd, wd), F32)
    _, s_f = _wkv(r_c, k_c, v_c, lwf_c, alf_c, kkp, kap, s0, bsz, rev=False, need_y=False)
    _, s_b = _wkv(r_c, k_c, v_c, lwb_c, alb_c, kkp, kap, s0, bsz, rev=True, need_y=False)
    yf, _ = _wkv(r_l, k_l, v_l, lwf_l, alf_l, kkp, kap, s_f, bsz, rev=False, need_y=True)
    yb, _ = _wkv(r_l, k_l, v_l, lwb_l, alb_l, kkp, kap, s_b, bsz, rev=True, need_y=True)
    xl = _rwkv_out(yf, yb, bon_l, g_l, xl, mods, lambda i: i // (seq // MIX_TILE), rwkv_ln_w[0].reshape(1, d),
                   rwkv_ln_b[0].reshape(1, d), rwkv_w_o[0].astype(BF16), tm=MIX_TILE)
    xl = _ffn(xl, mods, lat_row, 2, nw[2], w13, w2, 3, fw, tm=tm, final_norm=True)
    return xl.reshape(bsz, seq, d)
```

```python
import functools
import math

import numpy as np
import jax
import jax.numpy as jnp
from jax import lax
from jax.experimental import pallas as pl
from jax.experimental.pallas import tpu as pltpu

F32 = jnp.float32
BF16 = jnp.bfloat16

NORM_EPS = 1e-6
GN_EPS = 64e-5
KK_EPS = 1e-24
GRID_W = 64
FNET_GROUPS = 8
RWKV_HEAD = 64
N_MOD = 9
DECAY_SCALE = math.exp(-0.5)

LANE = 128
TOKEN_TILE = 512
MIX_TILE = 256
FFN_HIDDEN_TILE = 512
PROJ_COL_TILE = 256
ROW_CHUNK = 16
ROW_UNROLL = 8
WKV_CHUNK = 64
WKV_HEADS_PER_BLOCK = 2
WKV_GROUP = 16
VMEM_LIMIT = 56 * 1024 * 1024


def _cparams(sem):
    return pltpu.CompilerParams(dimension_semantics=sem, vmem_limit_bytes=VMEM_LIMIT)


def _sigmoid(x):
    return 1.0 / (1.0 + jnp.exp(-x))


def _modulate(x, nw, shift, scale):
    y = x * lax.rsqrt(jnp.mean(x * x, axis=-1, keepdims=True) + NORM_EPS)
    return (y * nw) * (1.0 + scale) + shift


def _dot(a, b):
    return jnp.dot(a, b, preferred_element_type=F32)


def _dot_nt(a, b):
    return lax.dot_general(a, b, (((1,), (1,)), ((), ())), preferred_element_type=F32)


def _split2(x):
    hi = x.astype(BF16)
    lo = (x - hi.astype(F32)).astype(BF16)
    return hi, lo


def _dot_x2(x, w_bf16):
    hi, lo = _split2(x)
    return _dot(hi, w_bf16) + _dot(lo, w_bf16)


def _adaln_kernel(c_ref, w_ref, b_ref, o_ref):
    c = c_ref[...]
    s = c * _sigmoid(c)
    o_ref[...] = _dot(s.astype(BF16), w_ref[...].astype(BF16)) + b_ref[...]


def _adaln(c8, mod_w, mod_b):
    depth, d, nd = mod_w.shape
    tn = 1024 if nd % 1024 == 0 else 512
    return pl.pallas_call(
        _adaln_kernel,
        out_shape=jax.ShapeDtypeStruct((depth, 8, nd), F32),
        grid=(depth, nd // tn),
        in_specs=[
            pl.BlockSpec((8, d), lambda l, j: (0, 0)),
            pl.BlockSpec((None, d, tn), lambda l, j: (l, 0, j)),
            pl.BlockSpec((None, 1, tn), lambda l, j: (l, 0, j)),
        ],
        out_specs=pl.BlockSpec((None, 8, tn), lambda l, j: (l, 0, j)),
        compiler_params=_cparams(("arbitrary", "arbitrary")),
        name="adaln_linear",
    )(c8, mod_w, mod_b.reshape(depth, 1, nd))


def _ffn_kernel(*refs, final_norm, emit_h):
    x_ref, sh_ref, sc_ref, gt_ref, nw_ref, w1g_ref, w1u_ref, w2_ref, fw_ref = refs[:9]
    if emit_h:
        nw2_ref, sh2_ref, sc2_ref, o_ref, h2_ref, h_ref = refs[9:]
    else:
        o_ref, h_ref = refs[9:]
    j = pl.program_id(1)
    n_chunks = x_ref.shape[0] // ROW_CHUNK

    def rows_of(c):
        return pl.ds(pl.multiple_of(c * ROW_CHUNK, ROW_CHUNK), ROW_CHUNK)

    @pl.when(j == 0)
    def _():
        nw, sh, sc = nw_ref[...], sh_ref[...], sc_ref[...]

        def chunk(c, carry):
            rows = rows_of(c)
            h_ref[rows, :] = _modulate(x_ref[rows, :], nw, sh, sc).astype(BF16)
            o_ref[rows, :] = jnp.zeros((ROW_CHUNK, o_ref.shape[1]), F32)
            return carry

        lax.fori_loop(0, n_chunks, chunk, 0, unroll=ROW_UNROLL)

    h = h_ref[...]
    g = _dot(h, w1g_ref[...])
    u = _dot(h, w1u_ref[...])
    a = (g * _sigmoid(g)) * u
    o_ref[...] += _dot(a.astype(BF16), w2_ref[...])

    @pl.when(j == pl.num_programs(1) - 1)
    def _():
        half_gate = 0.5 * gt_ref[...]

        def chunk(c, carry):
            rows = rows_of(c)
            y = x_ref[rows, :] + half_gate * o_ref[rows, :]
            if final_norm:
                y = y * lax.rsqrt(jnp.mean(y * y, axis=-1, keepdims=True) + NORM_EPS) * fw_ref[...]
            o_ref[rows, :] = y
            if emit_h:
                h2_ref[rows, :] = _modulate(y, nw2_ref[...], sh2_ref[...], sc2_ref[...])
            return carry

        lax.fori_loop(0, n_chunks, chunk, 0, unroll=ROW_UNROLL)


def _ffn(x, mods, row_of_tile, sub, nw, w13, w2, wi, fw, *, tm, final_norm=False, next_nw=None):
    n, d = x.shape
    f = w2.shape[1]
    tf = FFN_HIDDEN_TILE
    nf = f // tf
    emit_h = next_nw is not None

    def mod_spec(k):
        return pl.BlockSpec((None, 1, d), lambda i, j: (row_of_tile(i) * N_MOD + 3 * sub + k, 0, 0))

    vec_spec = pl.BlockSpec((1, d), lambda i, j: (0, 0))
    tile_spec = pl.BlockSpec((tm, d), lambda i, j: (i, 0))
    out = jax.ShapeDtypeStruct((n, d), F32)
    args = [x, mods, mods, mods, nw, w13, w13, w2, fw]
    specs = [tile_spec, mod_spec(0), mod_spec(1), mod_spec(2), vec_spec,
             pl.BlockSpec((None, d, tf), lambda i, j: (wi, 0, j)),
             pl.BlockSpec((None, d, tf), lambda i, j: (wi, 0, nf + j)),
             pl.BlockSpec((None, tf, d), lambda i, j: (wi, j, 0)),
             vec_spec]
    if emit_h:
        args += [next_nw, mods, mods]
        specs += [vec_spec, mod_spec(3), mod_spec(4)]
    return pl.pallas_call(
        functools.partial(_ffn_kernel, final_norm=final_norm, emit_h=emit_h),
        out_shape=[out, out] if emit_h else out,
        grid=(n // tm, nf),
        in_specs=specs,
        out_specs=[tile_spec, tile_spec] if emit_h else tile_spec,
        scratch_shapes=[pltpu.VMEM((tm, d), BF16)],
        compiler_params=_cparams(("arbitrary", "arbitrary")),
        name="ffn_swiglu",
    )(*args)


def _dft_tables(n):
    k = np.arange(n, dtype=np.int64)
    ang = 2.0 * np.pi * ((k[:, None] * k[None, :]) % n).astype(np.float64) / n
    return np.cos(ang), np.sin(ang)


def _hi_lo(m):
    m = jnp.asarray(m, F32)
    hi = m.astype(BF16)
    return hi, (m - hi.astype(F32)).astype(BF16)


DFT_SUB = 8
DFT_COLS = 1024
DFT_COLS_B = 512


def _dft_kernel(*refs, n_planes, mode, twiddle):
    it = iter(refs)
    x_ref = next(it)
    mats = [(next(it)[...], next(it)[...]) for _ in range(n_planes)]
    if twiddle:
        tc_ref, ts_ref = next(it), next(it)
    o_ref = next(it)
    if mode != "flat":
        stage_ref = next(it)

    def dense(p, strided):
        stage_ref[p] = strided
        return stage_ref[p]

    def transform(planes):
        y = None
        for x, (mh, ml) in zip(planes, mats):
            xh, xl = _split2(x)
            part = _dot(mh, xh) + (_dot(mh, xl) + _dot(ml, xh))
            y = part if y is None else y + part
        half = y.shape[0] // 2
        return y[:half], y[half:]

    if mode == "flat":
        yr, yi = transform([x_ref[...]])
        o_ref[0] = yr
        o_ref[1] = yi
    elif mode == "a":
        for j in range(DFT_SUB):
            yr, yi = transform([dense(0, x_ref[:, j, :])])
            if twiddle:
                reps = yr.shape[1] // LANE
                c = jnp.concatenate([tc_ref[j]] * reps, axis=1)
                s = jnp.concatenate([ts_ref[j]] * reps, axis=1)
                yr, yi = yr * c + yi * s, yi * c - yr * s
            o_ref[0, j] = yr
            o_ref[1, j] = yi
    else:
        for j in range(DFT_SUB):
            yr, yi = transform([dense(0, x_ref[0, :, j, :]), dense(1, x_ref[1, :, j, :])])
            o_ref[0, :, j, :] = yr
            o_ref[1, :, j, :] = yi


def _dft_call(mode, x, mats, out_shape, grid, x_spec, o_spec, tw=None, tw_spec=None):
    args, specs = [x], [x_spec]
    for m in mats:
        mh, ml = _hi_lo(m)
        args += [mh, ml]
        specs += [pl.BlockSpec(m.shape, lambda *_: (0, 0))] * 2
    if tw is not None:
        args += list(tw)
        specs += [tw_spec] * 2
    scratch = []
    if mode != "flat":
        scratch = [pltpu.VMEM((len(mats), mats[0].shape[1], x_spec.block_shape[-1]), F32)]
    return pl.pallas_call(
        functools.partial(_dft_kernel, n_planes=len(mats), mode=mode, twiddle=tw is not None),
        out_shape=jax.ShapeDtypeStruct(out_shape, F32),
        grid=grid,
        in_specs=specs,
        out_specs=o_spec,
        scratch_shapes=scratch,
        compiler_params=_cparams(("arbitrary",) * len(grid)),
        name="dft_" + mode,
    )(*args)


def _fmix_kernel(pr_ref, pi_ref, x_ref, gt_ref, ch_ref, cl_ref, wo_ref, bo_ref, o_ref, *, groups):
    pr, pi = pr_ref[...], pi_ref[...]
    gd = pr.shape[1] // groups
    ch, cl = ch_ref[...], cl_ref[...]
    outs = []
    for g in range(groups):
        z = jnp.concatenate([pr[:, g * gd:(g + 1) * gd], pi[:, g * gd:(g + 1) * gd]], axis=1)
        zh, zl = _split2(z)
        outs.append(_dot(zh, ch) + (_dot(zl, ch) + _dot(zh, cl)))
    f = jnp.concatenate(outs, axis=1)
    o = _dot(f.astype(BF16), wo_ref[...]) + bo_ref[...]
    o_ref[...] = x_ref[...] + gt_ref[...] * o


def _fmix(p, x, mods, row_of_tile, wo, bo, *, tm):
    n, d = x.shape
    b, _, l, _ = p.shape
    tpb = l // tm
    gd = d // FNET_GROUPS
    c, s = _dft_tables(gd)
    ch, cl = _hi_lo(np.concatenate([c, s], axis=0) / math.sqrt(gd))
    return pl.pallas_call(
        functools.partial(_fmix_kernel, groups=FNET_GROUPS),
        out_shape=jax.ShapeDtypeStruct((n, d), F32),
        grid=(n // tm,),
        in_specs=[
            pl.BlockSpec((None, None, tm, d), lambda i: (i // tpb, 0, i % tpb, 0)),
            pl.BlockSpec((None, None, tm, d), lambda i: (i // tpb, 1, i % tpb, 0)),
            pl.BlockSpec((tm, d), lambda i: (i, 0)),
            pl.BlockSpec((None, 1, d), lambda i: (row_of_tile(i) * N_MOD + 5, 0, 0)),
            pl.BlockSpec((2 * gd, gd), lambda i: (0, 0)),
            pl.BlockSpec((2 * gd, gd), lambda i: (0, 0)),
            pl.BlockSpec((d, d), lambda i: (0, 0)),
            pl.BlockSpec((1, d), lambda i: (0, 0)),
        ],
        out_specs=pl.BlockSpec((tm, d), lambda i: (i, 0)),
        compiler_params=_cparams(("arbitrary",)),
        name="fourier_out",
    )(p, p, x, mods, ch, cl, wo, bo)


def _fourier_latent(xl, hl, bsz, seq, mods, wo, bo):
    n, d = xl.shape
    la = lb = int(round(math.sqrt(seq)))
    assert la * lb == seq and lb % DFT_SUB == 0 and la % DFT_SUB == 0
    td = min(DFT_COLS, d)
    tdb = min(DFT_COLS_B, d)
    s8 = DFT_SUB
    ca, sa = _dft_tables(la)
    m_a = np.concatenate([ca, -sa], axis=0) / math.sqrt(la)
    cb, sb = _dft_tables(lb)
    m_br = np.concatenate([cb, -sb], axis=0) / math.sqrt(lb)
    m_bi = np.concatenate([sb, cb], axis=0) / math.sqrt(lb)
    n2 = np.arange(lb, dtype=np.int64)[:, None]
    k1 = np.arange(la, dtype=np.int64)[None, :]
    ang = 2.0 * np.pi * ((n2 * k1) % seq).astype(np.float64) / seq
    twc = jnp.asarray(np.broadcast_to(np.cos(ang)[:, :, None], (lb, la, LANE)), F32)
    tws = jnp.asarray(np.broadcast_to(np.sin(ang)[:, :, None], (lb, la, LANE)), F32)
    grid = (bsz, lb // s8, d // td)
    a = _dft_call(
        "a", hl.reshape(bsz, la, lb, d), [m_a], (bsz, 2, lb, la, d), grid,
        pl.BlockSpec((None, la, s8, td), lambda b, g, c: (b, 0, g, c)),
        pl.BlockSpec((None, 2, s8, la, td), lambda b, g, c: (b, 0, g, 0, c)),
        tw=(twc, tws), tw_spec=pl.BlockSpec((s8, la, LANE), lambda b, g, c: (g, 0, 0)))
    p = _dft_call(
        "b", a, [m_br, m_bi], (bsz, 2, lb, la, d), (bsz, la // s8, d // tdb),
        pl.BlockSpec((None, 2, lb, s8, tdb), lambda b, g, c: (b, 0, 0, g, c)),
        pl.BlockSpec((None, 2, lb, s8, tdb), lambda b, g, c: (b, 0, 0, g, c)))
    p = p.reshape(bsz, 2, seq, d)
    return _fmix(p, xl, mods, lambda i: i // (seq // MIX_TILE), wo, bo, tm=MIX_TILE)


def _fourier_ctx(xc, hc, bsz, lc, mods, wo, bo):
    n, d = xc.shape
    td = min(DFT_COLS, d)
    c, s = _dft_tables(lc)
    m = np.concatenate([c, -s], axis=0) / math.sqrt(lc)
    p = _dft_call(
        "flat", hc.reshape(bsz, lc, d), [m], (bsz, 2, lc, d), (bsz, d // td),
        pl.BlockSpec((None, lc, td), lambda b, c: (b, 0, c)),
        pl.BlockSpec((None, 2, lc, td), lambda b, c: (b, 0, 0, c)))
    return _fmix(p, xc, mods, lambda i: 2, wo, bo, tm=lc)


def _rwkv_proj_kernel(*refs, grid_mode, tpb):
    it = iter(refs)
    x_ref = next(it)
    if grid_mode:
        xp_ref, xn_ref = next(it), next(it)
    nw_ref, sh_ref, sc_ref, mu_ref = next(it), next(it), next(it), next(it)
    w1_ref, a1_ref, g1_ref = next(it), next(it), next(it)
    wr_ref, wk_ref, wv_ref = next(it), next(it), next(it)
    w2f_ref, w2b_ref, a2f_ref, a2b_ref, g2_ref = next(it), next(it), next(it), next(it), next(it)
    vec_ref, ones_ref = next(it), next(it)
    r_ref, k_ref, v_ref, g_ref, lwf_ref, lwb_ref, alf_ref, alb_ref, bon_ref = (next(it) for _ in range(9))
    hr_ref, hk_ref, hv_ref, tw_ref, ta_ref, tg_ref = (next(it) for _ in range(6))

    i = pl.program_id(0)
    j = pl.program_id(1)

    @pl.when(j == 0)
    def _():
        nw, sh, sc = nw_ref[...], sh_ref[...], sc_ref[...]
        h = _modulate(x_ref[...], nw, sh, sc)
        tm, d = h.shape
        q = d // 4
        row = lax.broadcasted_iota(jnp.int32, (tm, 1), 0)
        if grid_mode:
            tib = i % tpb
            up_ok = (tib != 0).astype(F32)
            dn_ok = (tib != tpb - 1).astype(F32)
            hp = _modulate(xp_ref[...], nw, sh, sc)[:, 2 * q:3 * q] * up_ok
            hn = _modulate(xn_ref[...], nw, sh, sc)[:, 3 * q:] * dn_ok
            col = row % GRID_W
            left = jnp.where(col != 0, pltpu.roll(h[:, :q], 1, 0), 0.0)
            right = jnp.where(col != GRID_W - 1, pltpu.roll(h[:, q:2 * q], tm - 1, 0), 0.0)
            up = jnp.concatenate([hp, h[:tm - GRID_W, 2 * q:3 * q]], axis=0)
            down = jnp.concatenate([h[GRID_W:, 3 * q:], hn], axis=0)
            hs = jnp.concatenate([left, right, up, down], axis=1)
        else:
            prev = jnp.where(row != 0, pltpu.roll(h, 1, 0), 0.0)
            nxt = jnp.where(row != tm - 1, pltpu.roll(h, tm - 1, 0), 0.0)
            hs = jnp.concatenate([prev[:, :q], nxt[:, q:2 * q], prev[:, 2 * q:3 * q], nxt[:, 3 * q:]], axis=1)
        dlt = hs - h
        mu = mu_ref[...]
        hr_ref[...] = (h + dlt * mu[0:1]).astype(BF16)
        hk_ref[...] = (h + dlt * mu[1:2]).astype(BF16)
        hv_ref[...] = (h + dlt * mu[2:3]).astype(BF16)
        tw_ref[...] = jnp.tanh(_dot((h + dlt * mu[3:4]).astype(BF16), w1_ref[...])).astype(BF16)
        ta_ref[...] = _dot((h + dlt * mu[4:5]).astype(BF16), a1_ref[...]).astype(BF16)
        tg_ref[...] = _sigmoid(_dot((h + dlt * mu[5:6]).astype(BF16), g1_ref[...])).astype(BF16)

    vec = vec_ref[...]
    w0f, w0b, a0f, a0b, ka, rk = (vec[n:n + 1] for n in range(6))
    r = _dot(hr_ref[...], wr_ref[...])
    k = _dot(hk_ref[...], wk_ref[...])
    v = _dot(hv_ref[...], wv_ref[...])
    tw = tw_ref[...]
    ta = ta_ref[...]
    lwf = -DECAY_SCALE * _sigmoid(w0f + _dot(tw, w2f_ref[...]))
    lwb = -DECAY_SCALE * _sigmoid(w0b + _dot(tw, w2b_ref[...]))
    alf = _sigmoid(a0f + _dot(ta, a2f_ref[...]))
    alb = _sigmoid(a0b + _dot(ta, a2b_ref[...]))
    kb = k * (1.0 + (0.5 * (alf + alb) - 1.0) * ka)
    r_ref[...] = r.astype(r_ref.dtype)
    k_ref[...] = k.astype(k_ref.dtype)
    v_ref[...] = v.astype(v_ref.dtype)
    g_ref[...] = _dot(tg_ref[...], g2_ref[...]).astype(g_ref.dtype)
    lwf_ref[...] = lwf
    lwb_ref[...] = lwb
    alf_ref[...] = alf.astype(alf_ref.dtype)
    alb_ref[...] = alb.astype(alb_ref.dtype)
    bon_ref[...] = (_dot_x2(r * kb * rk, ones_ref[...]) * v).astype(bon_ref.dtype)


def _rwkv_proj(x, mods, row_of_tile, nw, prm, *, tm, grid_mode, tpb):
    n, d = x.shape
    tn = PROJ_COL_TILE
    nj = d // tn
    hw = GRID_W
    lr = prm["w1c"].shape[1]

    def mod_spec(k):
        return pl.BlockSpec((None, 1, d), lambda i, j: (row_of_tile(i) * N_MOD + 3 + k, 0, 0))

    args, specs = [x], [pl.BlockSpec((tm, d), lambda i, j: (i, 0))]
    if grid_mode:
        nb = n // hw
        r = tm // hw
        args += [x, x]
        specs += [pl.BlockSpec((hw, d), lambda i, j: (jnp.maximum(i * r - 1, 0), 0)),
                  pl.BlockSpec((hw, d), lambda i, j: (jnp.minimum((i + 1) * r, nb - 1), 0))]
    args += [nw, mods, mods, prm["mu"], prm["w1c"], prm["a1c"], prm["g1"], prm["wr"], prm["wk"], prm["wv"],
             prm["w2f"], prm["w2b"], prm["a2f"], prm["a2b"], prm["g2"], prm["vec"], prm["ones_head"]]
    full = lambda shape: pl.BlockSpec(shape, lambda i, j: (0,) * len(shape))
    coltile = lambda rows: pl.BlockSpec((rows, tn), lambda i, j: (0, j))
    specs += [full((1, d)), mod_spec(0), mod_spec(1), full((8, d)), full((d, lr)), full((d, lr)), full((d, lr)),
              coltile(d), coltile(d), coltile(d), coltile(lr), coltile(lr), coltile(lr), coltile(lr), coltile(lr),
              coltile(8), full((tn, tn))]
    lo, hi = jax.ShapeDtypeStruct((n, d), BF16), jax.ShapeDtypeStruct((n, d), F32)
    return pl.pallas_call(
        functools.partial(_rwkv_proj_kernel, grid_mode=grid_mode, tpb=tpb),
        out_shape=[lo, lo, lo, lo, hi, hi, lo, lo, lo],
        grid=(n // tm, nj),
        in_specs=specs,
        out_specs=[pl.BlockSpec((tm, tn), lambda i, j: (i, j))] * 9,
        scratch_shapes=[pltpu.VMEM((tm, d), BF16)] * 3 + [pltpu.VMEM((tm, lr), BF16)] * 3,
        compiler_params=_cparams(("arbitrary", "arbitrary")),
        name="rwkv_proj",
    )(*args)


def _wkv_kernel(r_ref, k_ref, v_ref, lw_ref, al_ref, kk_ref, ka_ref, s0_ref, bm_ref, ms_ref, mi_ref, tri_ref,
                ones_ref, eye_ref, lvl_ref, y_ref, sf_ref, s_scr, *, rev, chunk, heads, nq, need_y, group):
    cc = pl.program_id(1)
    c = chunk
    wd = heads * RWKV_HEAD
    rn = heads * c

    @pl.when(cc == 0)
    def _():
        s_scr[...] = s0_ref[...]

    bm = bm_ref[...]
    bmf = bm.astype(F32)
    strict = ms_ref[...] > 0.0
    incl = mi_ref[...] > 0.0
    tri = tri_ref[...]
    ones = ones_ref[...]
    eye = eye_ref[...]

    def stack(x):
        return jnp.concatenate([x.astype(BF16)] * heads, axis=0) * bm

    if not need_y:
        y_ref[...] = jnp.zeros_like(y_ref)

    def block(q):
        sl = slice(q * wd, (q + 1) * wd)
        r, k, v = r_ref[:, sl].astype(F32), k_ref[:, sl].astype(F32), v_ref[:, sl].astype(F32)
        lw, al = lw_ref[:, sl], al_ref[:, sl].astype(F32)
        kk0 = k * kk_ref[:, sl]
        kk = kk0 * lax.rsqrt(jnp.maximum(_dot_x2(kk0 * kk0, ones), KK_EPS))
        b = kk * al
        kd = k * (1.0 + (al - 1.0) * ka_ref[:, sl])
        lw_hi, lw_lo = _split2(lw)
        lg = _dot(tri, lw_hi) + _dot(tri, lw_lo)
        gc = lg[0:1] if rev else lg[c - 1:c]
        gin = jnp.exp(-lg)
        gout = jnp.exp(gc - lg)
        a_n = -kk * jnp.exp(lg - lw)
        r_n = r * jnp.exp(lg)
        a_s = stack(a_n)
        v_s = stack(v)
        bk = jnp.concatenate([stack(b * gin), stack(kd * gin)], axis=0)
        bo_n = (b * gout).astype(BF16)
        ko_n = (kd * gout).astype(BF16)
        yield

        lhs = jnp.concatenate([a_n, r_n], axis=0) if need_y else a_n
        gm = _dot_nt(lhs.astype(BF16), bk)
        gab = gm[:c, :rn]
        aak = jnp.where(strict, gm[:c, rn:], 0.0).astype(BF16)
        if need_y:
            arb = jnp.where(incl, gm[c:, :rn], 0.0).astype(BF16)
            ark = jnp.where(incl, gm[c:, rn:], 0.0).astype(BF16)
        yield
        wy = _dot(jnp.concatenate([aak, ark], axis=0) if need_y else aak, v_s)
        w0 = wy[:c]

        t = eye + jnp.where(lvl_ref[0] > 0.0, gab, 0.0)
        for lvl in range(1, int(math.log2(c))):
            n_l = jnp.where(lvl_ref[lvl] > 0.0, gab, 0.0)
            u = _dot(t.astype(BF16), stack(n_l))
            yield
            t = t + _dot(u.astype(BF16), stack(t))
            yield
        t = t.astype(BF16)
        if need_y:
            t = jnp.concatenate([t, _dot(arb, stack(t)).astype(BF16)], axis=0)
            yield

        tv = _dot(t, jnp.concatenate([a_s, stack(w0)], axis=1))
        av = tv[:c]
        yield
        s = s_scr[q]
        sb = s.astype(BF16)
        if need_y:
            ry = tv[c:]
            rh = r_n + ry[:, :wd]
            yk = wy[c:]
        pd = _dot(av.T.astype(BF16), bo_n)
        dt = (pd[wd:] + _dot(v.T.astype(BF16), ko_n)) * bmf
        pt = (pd[:wd] * bmf).astype(BF16)
        yield
        if need_y:
            y_ref[:, sl] = _dot_nt(rh.astype(BF16), sb) + ry[:, wd:] + yk
        s_scr[q] = s * jnp.exp(gc) + _dot(sb, pt) + dt

    for q0 in range(0, nq, group):
        live = [block(q) for q in range(q0, min(q0 + group, nq))]
        while live:
            nxt = []
            for gen in live:
                try:
                    next(gen)
                    nxt.append(gen)
                except StopIteration:
                    pass
            live = nxt

    @pl.when(cc == pl.num_programs(1) - 1)
    def _():
        sf_ref[...] = s_scr[...]


def _wkv_consts(rev):
    c, g = WKV_CHUNK, WKV_HEADS_PER_BLOCK
    wd, rn = g * RWKV_HEAD, g * c
    assert c == RWKV_HEAD, "one (rn, wd) mask serves both (head, time) and (head, channel) columns"
    hrow = np.arange(rn) // c
    hlane = np.arange(wd) // RWKV_HEAD
    bm = (hrow[:, None] == hlane[None, :]).astype(np.float32)
    t = np.arange(c)[:, None]
    src = (np.arange(rn) % c)[None, :]
    if rev:
        strict, incl = src > t, src >= t
        tri = np.triu(np.ones((c, c), np.float32))
    else:
        strict, incl = src < t, src <= t
        tri = np.tril(np.ones((c, c), np.float32))
    ones = (hlane[:, None] == hlane[None, :]).astype(np.float32)
    levels = [strict & ((t >> (l + 1)) == (src >> (l + 1))) & ((t >> l) != (src >> l))
              for l in range(int(math.log2(c)))]
    return (jnp.asarray(bm, BF16), jnp.asarray(strict, F32), jnp.asarray(incl, F32), jnp.asarray(tri, BF16),
            jnp.asarray(ones, BF16), jnp.asarray(src == t, F32), jnp.asarray(np.stack(levels), F32))


def _wkv(r, k, v, lw, al, kkp, kap, s0, bsz, *, rev, need_y):
    n, d = r.shape
    c, g = WKV_CHUNK, WKV_HEADS_PER_BLOCK
    wd, rn = g * RWKV_HEAD, g * c
    nq = d // wd
    nc = n // bsz // c
    if rev:
        tok = lambda b, cc: (b * nc + (nc - 1 - cc), 0)
    else:
        tok = lambda b, cc: (b * nc + cc, 0)
    tspec = pl.BlockSpec((c, d), tok)
    full = lambda shape: pl.BlockSpec(shape, lambda b, cc: (0,) * len(shape))
    sspec = pl.BlockSpec((None, nq, wd, wd), lambda b, cc: (b, 0, 0, 0))
    consts = _wkv_consts(rev)
    y, sf = pl.pallas_call(
        functools.partial(_wkv_kernel, rev=rev, chunk=c, heads=g, nq=nq, need_y=need_y, group=WKV_GROUP),
        out_shape=[jax.ShapeDtypeStruct((n, d), F32), jax.ShapeDtypeStruct((bsz, nq, wd, wd), F32)],
        grid=(bsz, nc),
        in_specs=[tspec] * 5 + [full((1, d)), full((1, d)), sspec,
                                full((rn, wd)), full((c, rn)), full((c, rn)), full((c, c)), full((wd, wd)),
                                full((c, rn)), full((int(math.log2(c)), c, rn))],
        out_specs=[tspec, sspec],
        scratch_shapes=[pltpu.VMEM((nq, wd, wd), F32)],
        compiler_params=_cparams(("arbitrary", "arbitrary")),
        name="wkv_scan",
    )(r, k, v, lw, al, kkp, kap, s0, *consts)
    return y, sf


def _rwkv_out_kernel(yf_ref, yb_ref, bon_ref, g_ref, x_ref, gt_ref, lnw_ref, lnb_ref, ones_ref, wo_ref, o_ref,
                     *, wd):
    y = yf_ref[...] + yb_ref[...]
    ones = ones_ref[...]
    d = y.shape[1]
    inv = 1.0 / RWKV_HEAD

    def headsum(z):
        return jnp.concatenate([_dot_x2(z[:, q * wd:(q + 1) * wd], ones) for q in range(d // wd)], axis=1)

    dev = y - headsum(y) * inv
    var = headsum(dev * dev) * inv
    yn = dev * lax.rsqrt(var + GN_EPS) * lnw_ref[...] + lnb_ref[...]
    o = (yn + bon_ref[...].astype(F32)) * g_ref[...].astype(F32)
    o_ref[...] = x_ref[...] + gt_ref[...] * _dot(o.astype(BF16), wo_ref[...])


def _rwkv_out(yf, yb, bon, g, x, mods, row_of_tile, lnw, lnb, wo, *, tm):
    n, d = x.shape
    wd = WKV_HEADS_PER_BLOCK * RWKV_HEAD
    hl = np.arange(wd) // RWKV_HEAD
    ones = jnp.asarray(hl[:, None] == hl[None, :], BF16)
    tspec = pl.BlockSpec((tm, d), lambda i: (i, 0))
    full = lambda shape: pl.BlockSpec(shape, lambda i: (0,) * len(shape))
    return pl.pallas_call(
        functools.partial(_rwkv_out_kernel, wd=wd),
        out_shape=jax.ShapeDtypeStruct((n, d), F32),
        grid=(n // tm,),
        in_specs=[tspec] * 5 + [pl.BlockSpec((None, 1, d), lambda i: (row_of_tile(i) * N_MOD + 5, 0, 0)),
                                full((1, d)), full((1, d)), full((wd, wd)), full((d, d))],
        out_specs=tspec,
        compiler_params=_cparams(("arbitrary",)),
        name="rwkv_out",
    )(yf, yb, bon, g, x, mods, lnw, lnb, ones, wo)


def _rwkv_params(j, d, rwkv_mu, rwkv_w_rkv, rwkv_w0, rwkv_w1, rwkv_w2, rwkv_a0, rwkv_a1, rwkv_a2, rwkv_g1,
                 rwkv_g2, rwkv_k_a, rwkv_r_k):
    lr = rwkv_g1.shape[2]
    rank = rwkv_w1.shape[3]
    assert 2 * rank <= lr

    def first(w):
        return jnp.pad(jnp.concatenate([w[0], w[1]], axis=1), ((0, 0), (0, lr - 2 * rank))).astype(BF16)

    def second(w, z):
        return jnp.pad(w, ((z * rank, lr - (z + 1) * rank), (0, 0))).astype(BF16)

    vec = jnp.stack([rwkv_w0[j, 0], rwkv_w0[j, 1], rwkv_a0[j, 0], rwkv_a0[j, 1], rwkv_k_a[j],
                     rwkv_r_k[j].reshape(d), jnp.zeros((d,), F32), jnp.zeros((d,), F32)])
    hl = np.arange(256) // RWKV_HEAD
    return dict(
        mu=jnp.pad(rwkv_mu[j], ((0, 2), (0, 0))),
        w1c=first(rwkv_w1[j]), a1c=first(rwkv_a1[j]), g1=rwkv_g1[j].astype(BF16),
        wr=rwkv_w_rkv[j, 0].astype(BF16), wk=rwkv_w_rkv[j, 1].astype(BF16), wv=rwkv_w_rkv[j, 2].astype(BF16),
        w2f=second(rwkv_w2[j, 0], 0), w2b=second(rwkv_w2[j, 1], 1),
        a2f=second(rwkv_a2[j, 0], 0), a2b=second(rwkv_a2[j, 1], 1),
        g2=rwkv_g2[j].astype(BF16), vec=vec,
        ones_head=jnp.asarray(hl[:, None] == hl[None, :], BF16),
    )


def kernel(x, c, ctx, c_ctx, mod_w, mod_b, norm_w, ffn_w13, ffn_w2, fnet_w_o, fnet_b_o, rwkv_mu, rwkv_w_rkv,
           rwkv_w0, rwkv_w1, rwkv_w2, rwkv_a0, rwkv_a1, rwkv_a2, rwkv_g1, rwkv_g2, rwkv_k_k, rwkv_k_a, rwkv_r_k,
           rwkv_ln_w, rwkv_ln_b, rwkv_w_o, final_norm_w):
    bsz, seq, d = x.shape
    lc = ctx.shape[1]
    depth = mod_w.shape[0]
    assert depth == 2 and bsz == 2, "layer schedule below is written for the two-layer, batch-2 trunk"
    tm = TOKEN_TILE
    assert seq % tm == 0 and seq % GRID_W == 0

    c8 = jnp.concatenate([c, c_ctx[None], jnp.zeros((8 - bsz - 1, d), F32)], axis=0)
    mods_all = _adaln(c8, mod_w, mod_b)
    f2 = ffn_w13.shape[-1]
    w13 = ffn_w13.astype(BF16).reshape(2 * depth, d, f2)
    w2 = ffn_w2.astype(BF16).reshape(2 * depth, f2 // 2, d)
    fw = final_norm_w.reshape(1, d)

    xl = x.reshape(bsz * seq, d)
    xc = ctx.reshape(bsz * lc, d)
    lat_row = lambda i: i // (seq // tm)
    ctx_row = lambda i: 2

    mods = mods_all[0, :3].reshape(3 * N_MOD, 1, d)
    nw = norm_w[0].reshape(3, 1, d)
    xl, hl = _ffn(xl, mods, lat_row, 0, nw[0], w13, w2, 0, fw, tm=tm, next_nw=nw[1])
    xc, hc = _ffn(xc, mods, ctx_row, 0, nw[0], w13, w2, 0, fw, tm=lc, next_nw=nw[1])
    wo = fnet_w_o[0].astype(BF16)
    bo = fnet_b_o[0].reshape(1, d)
    xl = _fourier_latent(xl, hl, bsz, seq, mods, wo, bo)
    xc = _fourier_ctx(xc, hc, bsz, lc, mods, wo, bo)
    xl = _ffn(xl, mods, lat_row, 2, nw[2], w13, w2, 1, fw, tm=tm)
    xc = _ffn(xc, mods, ctx_row, 2, nw[2], w13, w2, 1, fw, tm=lc)

    mods = mods_all[1, :3].reshape(3 * N_MOD, 1, d)
    nw = norm_w[1].reshape(3, 1, d)
    xl = _ffn(xl, mods, lat_row, 0, nw[0], w13, w2, 2, fw, tm=tm)
    xc = _ffn(xc, mods, ctx_row, 0, nw[0], w13, w2, 2, fw, tm=lc)
    prm = _rwkv_params(0, d, rwkv_mu, rwkv_w_rkv, rwkv_w0, rwkv_w1, rwkv_w2, rwkv_a0, rwkv_a1, rwkv_a2, rwkv_g1,
                       rwkv_g2, rwkv_k_a, rwkv_r_k)
    r_c, k_c, v_c, _, lwf_c, lwb_c, alf_c, alb_c, _ = _rwkv_proj(
        xc, mods, ctx_row, nw[1], prm, tm=lc, grid_mode=False, tpb=1)
    r_l, k_l, v_l, g_l, lwf_l, lwb_l, alf_l, alb_l, bon_l = _rwkv_proj(
        xl, mods, lat_row, nw[1], prm, tm=tm, grid_mode=True, tpb=seq // tm)
    kkp = rwkv_k_k[0].reshape(1, d)
    kap = rwkv_k_a[0].reshape(1, d)
    wd = WKV_HEADS_PER_BLOCK * RWKV_HEAD
    s0 = jnp.zeros((bsz, d // wd, wd, wd), F32)
    _, s_f = _wkv(r_c, k_c, v_c, lwf_c, alf_c, kkp, kap, s0, bsz, rev=False, need_y=False)
    _, s_b = _wkv(r_c, k_c, v_c, lwb_c, alb_c, kkp, kap, s0, bsz, rev=True, need_y=False)
    yf, _ = _wkv(r_l, k_l, v_l, lwf_l, alf_l, kkp, kap, s_f, bsz, rev=False, need_y=True)
    yb, _ = _wkv(r_l, k_l, v_l, lwb_l, alb_l, kkp, kap, s_b, bsz, rev=True, need_y=True)
    xl = _rwkv_out(yf, yb, bon_l, g_l, xl, mods, lambda i: i // (seq // MIX_TILE), rwkv_ln_w[0].reshape(1, d),
                   rwkv_ln_b[0].reshape(1, d), rwkv_w_o[0].astype(BF16), tm=MIX_TILE)
    xl = _ffn(xl, mods, lat_row, 2, nw[2], w13, w2, 3, fw, tm=tm, final_norm=True)
    return xl.reshape(bsz, seq, d)
```

```python
import functools
import math

import numpy as np
import jax
import jax.numpy as jnp
from jax import lax
from jax.experimental import pallas as pl
from jax.experimental.pallas import tpu as pltpu

F32 = jnp.float32
BF16 = jnp.bfloat16

NORM_EPS = 1e-6
GN_EPS = 64e-5
KK_EPS = 1e-24
GRID_W = 64
FNET_GROUPS = 8
RWKV_HEAD = 64
N_MOD = 9
DECAY_SCALE = math.exp(-0.5)

LANE = 128
TOKEN_TILE = 512
MIX_TILE = 256
FFN_HIDDEN_TILE = 512
PROJ_COL_TILE = 256
ROW_CHUNK = 16
ROW_UNROLL = 8
WKV_CHUNK = 64
WKV_HEADS_PER_BLOCK = 2
WKV_GROUP = 16
VMEM_LIMIT = 56 * 1024 * 1024


def _cparams(sem):
    return pltpu.CompilerParams(dimension_semantics=sem, vmem_limit_bytes=VMEM_LIMIT)


def _sigmoid(x):
    return 1.0 / (1.0 + jnp.exp(-x))


def _modulate(x, nw, shift, scale):
    y = x * lax.rsqrt(jnp.mean(x * x, axis=-1, keepdims=True) + NORM_EPS)
    return (y * nw) * (1.0 + scale) + shift


def _dot(a, b):
    return jnp.dot(a, b, preferred_element_type=F32)


def _dot_nt(a, b):
    return lax.dot_general(a, b, (((1,), (1,)), ((), ())), preferred_element_type=F32)


def _split2(x):
    hi = x.astype(BF16)
    lo = (x - hi.astype(F32)).astype(BF16)
    return hi, lo


def _dot_x2(x, w_bf16):
    hi, lo = _split2(x)
    return _dot(hi, w_bf16) + _dot(lo, w_bf16)


def _adaln_kernel(c_ref, w_ref, b_ref, o_ref):
    c = c_ref[...]
    s = c * _sigmoid(c)
    o_ref[...] = _dot(s.astype(BF16), w_ref[...].astype(BF16)) + b_ref[...]


def _adaln(c8, mod_w, mod_b):
    depth, d, nd = mod_w.shape
    tn = 1024 if nd % 1024 == 0 else 512
    return pl.pallas_call(
        _adaln_kernel,
        out_shape=jax.ShapeDtypeStruct((depth, 8, nd), F32),
        grid=(depth, nd // tn),
        in_specs=[
            pl.BlockSpec((8, d), lambda l, j: (0, 0)),
            pl.BlockSpec((None, d, tn), lambda l, j: (l, 0, j)),
            pl.BlockSpec((None, 1, tn), lambda l, j: (l, 0, j)),
        ],
        out_specs=pl.BlockSpec((None, 8, tn), lambda l, j: (l, 0, j)),
        compiler_params=_cparams(("arbitrary", "arbitrary")),
        name="adaln_linear",
    )(c8, mod_w, mod_b.reshape(depth, 1, nd))


def _ffn_kernel(*refs, final_norm, emit_h):
    x_ref, sh_ref, sc_ref, gt_ref, nw_ref, w1g_ref, w1u_ref, w2_ref, fw_ref = refs[:9]
    if emit_h:
        nw2_ref, sh2_ref, sc2_ref, o_ref, h2_ref, h_ref = refs[9:]
    else:
        o_ref, h_ref = refs[9:]
    j = pl.program_id(1)
    n_chunks = x_ref.shape[0] // ROW_CHUNK

    def rows_of(c):
        return pl.ds(pl.multiple_of(c * ROW_CHUNK, ROW_CHUNK), ROW_CHUNK)

    @pl.when(j == 0)
    def _():
        nw, sh, sc = nw_ref[...], sh_ref[...], sc_ref[...]

        def chunk(c, carry):
            rows = rows_of(c)
            h_ref[rows, :] = _modulate(x_ref[rows, :], nw, sh, sc).astype(BF16)
            o_ref[rows, :] = jnp.zeros((ROW_CHUNK, o_ref.shape[1]), F32)
            return carry

        lax.fori_loop(0, n_chunks, chunk, 0, unroll=ROW_UNROLL)

    h = h_ref[...]
    g = _dot(h, w1g_ref[...])
    u = _dot(h, w1u_ref[...])
    a = (g * _sigmoid(g)) * u
    o_ref[...] += _dot(a.astype(BF16), w2_ref[...])

    @pl.when(j == pl.num_programs(1) - 1)
    def _():
        y = x_ref[...] + (0.5 * gt_ref[...]) * o_ref[...]
        if final_norm:
            y = y * lax.rsqrt(jnp.mean(y * y, axis=-1, keepdims=True) + NORM_EPS) * fw_ref[...]
        o_ref[...] = y
        if emit_h:
            h2_ref[...] = _modulate(y, nw2_ref[...], sh2_ref[...], sc2_ref[...])


def _ffn(x, mods, row_of_tile, sub, nw, w13, w2, wi, fw, *, tm, final_norm=False, next_nw=None):
    n, d = x.shape
    f = w2.shape[1]
    tf = FFN_HIDDEN_TILE
    nf = f // tf
    emit_h = next_nw is not None

    def mod_spec(k):
        return pl.BlockSpec((None, 1, d), lambda i, j: (row_of_tile(i) * N_MOD + 3 * sub + k, 0, 0))

    vec_spec = pl.BlockSpec((1, d), lambda i, j: (0, 0))
    tile_spec = pl.BlockSpec((tm, d), lambda i, j: (i, 0))
    out = jax.ShapeDtypeStruct((n, d), F32)
    args = [x, mods, mods, mods, nw, w13, w13, w2, fw]
    specs = [tile_spec, mod_spec(0), mod_spec(1), mod_spec(2), vec_spec,
             pl.BlockSpec((None, d, tf), lambda i, j: (wi, 0, j)),
             pl.BlockSpec((None, d, tf), lambda i, j: (wi, 0, nf + j)),
             pl.BlockSpec((None, tf, d), lambda i, j: (wi, j, 0)),
             vec_spec]
    if emit_h:
        args += [next_nw, mods, mods]
        specs += [vec_spec, mod_spec(3), mod_spec(4)]
    return pl.pallas_call(
        functools.partial(_ffn_kernel, final_norm=final_norm, emit_h=emit_h),
        out_shape=[out, out] if emit_h else out,
        grid=(n // tm, nf),
        in_specs=specs,
        out_specs=[tile_spec, tile_spec] if emit_h else tile_spec,
        scratch_shapes=[pltpu.VMEM((tm, d), BF16)],
        compiler_params=_cparams(("arbitrary", "arbitrary")),
        name="ffn_swiglu",
    )(*args)


def _dft_tables(n):
    k = np.arange(n, dtype=np.int64)
    ang = 2.0 * np.pi * ((k[:, None] * k[None, :]) % n).astype(np.float64) / n
    return np.cos(ang), np.sin(ang)


def _hi_lo(m):
    m = jnp.asarray(m, F32)
    hi = m.astype(BF16)
    return hi, (m - hi.astype(F32)).astype(BF16)


DFT_SUB = 8
DFT_COLS = 1024
DFT_COLS_B = 512


def _dft_kernel(*refs, n_planes, mode, twiddle):
    it = iter(refs)
    x_ref = next(it)
    mats = [(next(it)[...], next(it)[...]) for _ in range(n_planes)]
    if twiddle:
        tc_ref, ts_ref = next(it), next(it)
    o_ref = next(it)
    if mode != "flat":
        stage_ref = next(it)

    def dense(p, strided):
        stage_ref[p] = strided
        return stage_ref[p]

    def transform(planes):
        y = None
        for x, (mh, ml) in zip(planes, mats):
            xh, xl = _split2(x)
            part = _dot(mh, xh) + (_dot(mh, xl) + _dot(ml, xh))
            y = part if y is None else y + part
        half = y.shape[0] // 2
        return y[:half], y[half:]

    if mode == "flat":
        yr, yi = transform([x_ref[...]])
        o_ref[0] = yr
        o_ref[1] = yi
    elif mode == "a":
        for j in range(DFT_SUB):
            yr, yi = transform([dense(0, x_ref[:, j, :])])
            if twiddle:
                reps = yr.shape[1] // LANE
                c = jnp.concatenate([tc_ref[j]] * reps, axis=1)
                s = jnp.concatenate([ts_ref[j]] * reps, axis=1)
                yr, yi = yr * c + yi * s, yi * c - yr * s
            o_ref[0, j] = yr
            o_ref[1, j] = yi
    else:
        for j in range(DFT_SUB):
            yr, yi = transform([dense(0, x_ref[0, :, j, :]), dense(1, x_ref[1, :, j, :])])
            o_ref[0, :, j, :] = yr
            o_ref[1, :, j, :] = yi


def _dft_call(mode, x, mats, out_shape, grid, x_spec, o_spec, tw=None, tw_spec=None):
    args, specs = [x], [x_spec]
    for m in mats:
        mh, ml = _hi_lo(m)
        args += [mh, ml]
        specs += [pl.BlockSpec(m.shape, lambda *_: (0, 0))] * 2
    if tw is not None:
        args += list(tw)
        specs += [tw_spec] * 2
    scratch = []
    if mode != "flat":
        scratch = [pltpu.VMEM((len(mats), mats[0].shape[1], x_spec.block_shape[-1]), F32)]
    return pl.pallas_call(
        functools.partial(_dft_kernel, n_planes=len(mats), mode=mode, twiddle=tw is not None),
        out_shape=jax.ShapeDtypeStruct(out_shape, F32),
        grid=grid,
        in_specs=specs,
        out_specs=o_spec,
        scratch_shapes=scratch,
        compiler_params=_cparams(("arbitrary",) * len(grid)),
        name="dft_" + mode,
    )(*args)


def _fmix_kernel(pr_ref, pi_ref, x_ref, gt_ref, ch_ref, cl_ref, wo_ref, bo_ref, o_ref, *, groups):
    pr, pi = pr_ref[...], pi_ref[...]
    gd = pr.shape[1] // groups
    ch, cl = ch_ref[...], cl_ref[...]
    outs = []
    for g in range(groups):
        z = jnp.concatenate([pr[:, g * gd:(g + 1) * gd], pi[:, g * gd:(g + 1) * gd]], axis=1)
        zh, zl = _split2(z)
        outs.append(_dot(zh, ch) + (_dot(zl, ch) + _dot(zh, cl)))
    f = jnp.concatenate(outs, axis=1)
    o = _dot(f.astype(BF16), wo_ref[...]) + bo_ref[...]
    o_ref[...] = x_ref[...] + gt_ref[...] * o


def _fmix(p, x, mods, row_of_tile, wo, bo, *, tm):
    n, d = x.shape
    b, _, l, _ = p.shape
    tpb = l // tm
    gd = d // FNET_GROUPS
    c, s = _dft_tables(gd)
    ch, cl = _hi_lo(np.concatenate([c, s], axis=0) / math.sqrt(gd))
    return pl.pallas_call(
        functools.partial(_fmix_kernel, groups=FNET_GROUPS),
        out_shape=jax.ShapeDtypeStruct((n, d), F32),
        grid=(n // tm,),
        in_specs=[
            pl.BlockSpec((None, None, tm, d), lambda i: (i // tpb, 0, i % tpb, 0)),
            pl.BlockSpec((None, None, tm, d), lambda i: (i // tpb, 1, i % tpb, 0)),
            pl.BlockSpec((tm, d), lambda i: (i, 0)),
            pl.BlockSpec((None, 1, d), lambda i: (row_of_tile(i) * N_MOD + 5, 0, 0)),
            pl.BlockSpec((2 * gd, gd), lambda i: (0, 0)),
            pl.BlockSpec((2 * gd, gd), lambda i: (0, 0)),
            pl.BlockSpec((d, d), lambda i: (0, 0)),
            pl.BlockSpec((1, d), lambda i: (0, 0)),
        ],
        out_specs=pl.BlockSpec((tm, d), lambda i: (i, 0)),
        compiler_params=_cparams(("arbitrary",)),
        name="fourier_out",
    )(p, p, x, mods, ch, cl, wo, bo)


def _fourier_latent(xl, hl, bsz, seq, mods, wo, bo):
    n, d = xl.shape
    la = lb = int(round(math.sqrt(seq)))
    assert la * lb == seq and lb % DFT_SUB == 0 and la % DFT_SUB == 0
    td = min(DFT_COLS, d)
    tdb = min(DFT_COLS_B, d)
    s8 = DFT_SUB
    ca, sa = _dft_tables(la)
    m_a = np.concatenate([ca, -sa], axis=0) / math.sqrt(la)
    cb, sb = _dft_tables(lb)
    m_br = np.concatenate([cb, -sb], axis=0) / math.sqrt(lb)
    m_bi = np.concatenate([sb, cb], axis=0) / math.sqrt(lb)
    n2 = np.arange(lb, dtype=np.int64)[:, None]
    k1 = np.arange(la, dtype=np.int64)[None, :]
    ang = 2.0 * np.pi * ((n2 * k1) % seq).astype(np.float64) / seq
    twc = jnp.asarray(np.broadcast_to(np.cos(ang)[:, :, None], (lb, la, LANE)), F32)
    tws = jnp.asarray(np.broadcast_to(np.sin(ang)[:, :, None], (lb, la, LANE)), F32)
    grid = (bsz, lb // s8, d // td)
    a = _dft_call(
        "a", hl.reshape(bsz, la, lb, d), [m_a], (bsz, 2, lb, la, d), grid,
        pl.BlockSpec((None, la, s8, td), lambda b, g, c: (b, 0, g, c)),
        pl.BlockSpec((None, 2, s8, la, td), lambda b, g, c: (b, 0, g, 0, c)),
        tw=(twc, tws), tw_spec=pl.BlockSpec((s8, la, LANE), lambda b, g, c: (g, 0, 0)))
    p = _dft_call(
        "b", a, [m_br, m_bi], (bsz, 2, lb, la, d), (bsz, la // s8, d // tdb),
        pl.BlockSpec((None, 2, lb, s8, tdb), lambda b, g, c: (b, 0, 0, g, c)),
        pl.BlockSpec((None, 2, lb, s8, tdb), lambda b, g, c: (b, 0, 0, g, c)))
    p = p.reshape(bsz, 2, seq, d)
    return _fmix(p, xl, mods, lambda i: i // (seq // MIX_TILE), wo, bo, tm=MIX_TILE)


def _fourier_ctx(xc, hc, bsz, lc, mods, wo, bo):
    n, d = xc.shape
    td = min(DFT_COLS, d)
    c, s = _dft_tables(lc)
    m = np.concatenate([c, -s], axis=0) / math.sqrt(lc)
    p = _dft_call(
        "flat", hc.reshape(bsz, lc, d), [m], (bsz, 2, lc, d), (bsz, d // td),
        pl.BlockSpec((None, lc, td), lambda b, c: (b, 0, c)),
        pl.BlockSpec((None, 2, lc, td), lambda b, c: (b, 0, 0, c)))
    return _fmix(p, xc, mods, lambda i: 2, wo, bo, tm=lc)


def _rwkv_proj_kernel(*refs, grid_mode, tpb):
    it = iter(refs)
    x_ref = next(it)
    if grid_mode:
        xp_ref, xn_ref = next(it), next(it)
    nw_ref, sh_ref, sc_ref, mu_ref = next(it), next(it), next(it), next(it)
    w1_ref, a1_ref, g1_ref = next(it), next(it), next(it)
    wr_ref, wk_ref, wv_ref = next(it), next(it), next(it)
    w2f_ref, w2b_ref, a2f_ref, a2b_ref, g2_ref = next(it), next(it), next(it), next(it), next(it)
    vec_ref, ones_ref = next(it), next(it)
    r_ref, k_ref, v_ref, g_ref, lwf_ref, lwb_ref, alf_ref, alb_ref, bon_ref = (next(it) for _ in range(9))
    hr_ref, hk_ref, hv_ref, tw_ref, ta_ref, tg_ref = (next(it) for _ in range(6))

    i = pl.program_id(0)
    j = pl.program_id(1)

    @pl.when(j == 0)
    def _():
        nw, sh, sc = nw_ref[...], sh_ref[...], sc_ref[...]
        h = _modulate(x_ref[...], nw, sh, sc)
        tm, d = h.shape
        q = d // 4
        row = lax.broadcasted_iota(jnp.int32, (tm, 1), 0)
        if grid_mode:
            tib = i % tpb
            up_ok = (tib != 0).astype(F32)
            dn_ok = (tib != tpb - 1).astype(F32)
            hp = _modulate(xp_ref[...], nw, sh, sc)[:, 2 * q:3 * q] * up_ok
            hn = _modulate(xn_ref[...], nw, sh, sc)[:, 3 * q:] * dn_ok
            col = row % GRID_W
            left = jnp.where(col != 0, pltpu.roll(h[:, :q], 1, 0), 0.0)
            right = jnp.where(col != GRID_W - 1, pltpu.roll(h[:, q:2 * q], tm - 1, 0), 0.0)
            up = jnp.concatenate([hp, h[:tm - GRID_W, 2 * q:3 * q]], axis=0)
            down = jnp.concatenate([h[GRID_W:, 3 * q:], hn], axis=0)
            hs = jnp.concatenate([left, right, up, down], axis=1)
        else:
            prev = jnp.where(row != 0, pltpu.roll(h, 1, 0), 0.0)
            nxt = jnp.where(row != tm - 1, pltpu.roll(h, tm - 1, 0), 0.0)
            hs = jnp.concatenate([prev[:, :q], nxt[:, q:2 * q], prev[:, 2 * q:3 * q], nxt[:, 3 * q:]], axis=1)
        dlt = hs - h
        mu = mu_ref[...]
        hr_ref[...] = (h + dlt * mu[0:1]).astype(BF16)
        hk_ref[...] = (h + dlt * mu[1:2]).astype(BF16)
        hv_ref[...] = (h + dlt * mu[2:3]).astype(BF16)
        tw_ref[...] = jnp.tanh(_dot((h + dlt * mu[3:4]).astype(BF16), w1_ref[...])).astype(BF16)
        ta_ref[...] = _dot((h + dlt * mu[4:5]).astype(BF16), a1_ref[...]).astype(BF16)
        tg_ref[...] = _sigmoid(_dot((h + dlt * mu[5:6]).astype(BF16), g1_ref[...])).astype(BF16)

    vec = vec_ref[...]
    w0f, w0b, a0f, a0b, ka, rk = (vec[n:n + 1] for n in range(6))
    r = _dot(hr_ref[...], wr_ref[...])
    k = _dot(hk_ref[...], wk_ref[...])
    v = _dot(hv_ref[...], wv_ref[...])
    tw = tw_ref[...]
    ta = ta_ref[...]
    lwf = -DECAY_SCALE * _sigmoid(w0f + _dot(tw, w2f_ref[...]))
    lwb = -DECAY_SCALE * _sigmoid(w0b + _dot(tw, w2b_ref[...]))
    alf = _sigmoid(a0f + _dot(ta, a2f_ref[...]))
    alb = _sigmoid(a0b + _dot(ta, a2b_ref[...]))
    kb = k * (1.0 + (0.5 * (alf + alb) - 1.0) * ka)
    r_ref[...] = r.astype(r_ref.dtype)
    k_ref[...] = k.astype(k_ref.dtype)
    v_ref[...] = v.astype(v_ref.dtype)
    g_ref[...] = _dot(tg_ref[...], g2_ref[...]).astype(g_ref.dtype)
    lwf_ref[...] = lwf
    lwb_ref[...] = lwb
    alf_ref[...] = alf.astype(alf_ref.dtype)
    alb_ref[...] = alb.astype(alb_ref.dtype)
    bon_ref[...] = (_dot_x2(r * kb * rk, ones_ref[...]) * v).astype(bon_ref.dtype)


def _rwkv_proj(x, mods, row_of_tile, nw, prm, *, tm, grid_mode, tpb):
    n, d = x.shape
    tn = PROJ_COL_TILE
    nj = d // tn
    hw = GRID_W
    lr = prm["w1c"].shape[1]

    def mod_spec(k):
        return pl.BlockSpec((None, 1, d), lambda i, j: (row_of_tile(i) * N_MOD + 3 + k, 0, 0))

    args, specs = [x], [pl.BlockSpec((tm, d), lambda i, j: (i, 0))]
    if grid_mode:
        nb = n // hw
        r = tm // hw
        args += [x, x]
        specs += [pl.BlockSpec((hw, d), lambda i, j: (jnp.maximum(i * r - 1, 0), 0)),
                  pl.BlockSpec((hw, d), lambda i, j: (jnp.minimum((i + 1) * r, nb - 1), 0))]
    args += [nw, mods, mods, prm["mu"], prm["w1c"], prm["a1c"], prm["g1"], prm["wr"], prm["wk"], prm["wv"],
             prm["w2f"], prm["w2b"], prm["a2f"], prm["a2b"], prm["g2"], prm["vec"], prm["ones_head"]]
    full = lambda shape: pl.BlockSpec(shape, lambda i, j: (0,) * len(shape))
    coltile = lambda rows: pl.BlockSpec((rows, tn), lambda i, j: (0, j))
    specs += [full((1, d)), mod_spec(0), mod_spec(1), full((8, d)), full((d, lr)), full((d, lr)), full((d, lr)),
              coltile(d), coltile(d), coltile(d), coltile(lr), coltile(lr), coltile(lr), coltile(lr), coltile(lr),
              coltile(8), full((tn, tn))]
    lo, hi = jax.ShapeDtypeStruct((n, d), BF16), jax.ShapeDtypeStruct((n, d), F32)
    return pl.pallas_call(
        functools.partial(_rwkv_proj_kernel, grid_mode=grid_mode, tpb=tpb),
        out_shape=[lo, lo, lo, lo, hi, hi, lo, lo, lo],
        grid=(n // tm, nj),
        in_specs=specs,
        out_specs=[pl.BlockSpec((tm, tn), lambda i, j: (i, j))] * 9,
        scratch_shapes=[pltpu.VMEM((tm, d), BF16)] * 3 + [pltpu.VMEM((tm, lr), BF16)] * 3,
        compiler_params=_cparams(("arbitrary", "arbitrary")),
        name="rwkv_proj",
    )(*args)


def _wkv_kernel(r_ref, k_ref, v_ref, lw_ref, al_ref, kk_ref, ka_ref, s0_ref, bm_ref, ms_ref, mi_ref, tri_ref,
                ones_ref, eye_ref, lvl_ref, y_ref, sf_ref, s_scr, *, rev, chunk, heads, nq, need_y, group):
    cc = pl.program_id(1)
    c = chunk
    wd = heads * RWKV_HEAD
    rn = heads * c

    @pl.when(cc == 0)
    def _():
        s_scr[...] = s0_ref[...]

    bm = bm_ref[...]
    bmf = bm.astype(F32)
    strict = ms_ref[...] > 0.0
    incl = mi_ref[...] > 0.0
    tri = tri_ref[...]
    ones = ones_ref[...]
    eye = eye_ref[...]

    def stack(x):
        return jnp.concatenate([x.astype(BF16)] * heads, axis=0) * bm

    if not need_y:
        y_ref[...] = jnp.zeros_like(y_ref)

    def block(q):
        sl = slice(q * wd, (q + 1) * wd)
        r, k, v = r_ref[:, sl].astype(F32), k_ref[:, sl].astype(F32), v_ref[:, sl].astype(F32)
        lw, al = lw_ref[:, sl], al_ref[:, sl].astype(F32)
        kk0 = k * kk_ref[:, sl]
        kk = kk0 * lax.rsqrt(jnp.maximum(_dot_x2(kk0 * kk0, ones), KK_EPS))
        b = kk * al
        kd = k * (1.0 + (al - 1.0) * ka_ref[:, sl])
        lw_hi, lw_lo = _split2(lw)
        lg = _dot(tri, lw_hi) + _dot(tri, lw_lo)
        gc = lg[0:1] if rev else lg[c - 1:c]
        gin = jnp.exp(-lg)
        gout = jnp.exp(gc - lg)
        a_n = -kk * jnp.exp(lg - lw)
        r_n = r * jnp.exp(lg)
        a_s = stack(a_n)
        v_s = stack(v)
        bk = jnp.concatenate([stack(b * gin), stack(kd * gin)], axis=0)
        bo_n = (b * gout).astype(BF16)
        ko_n = (kd * gout).astype(BF16)
        yield

        lhs = jnp.concatenate([a_n, r_n], axis=0) if need_y else a_n
        gm = _dot_nt(lhs.astype(BF16), bk)
        gab = gm[:c, :rn]
        aak = jnp.where(strict, gm[:c, rn:], 0.0).astype(BF16)
        if need_y:
            arb = jnp.where(incl, gm[c:, :rn], 0.0).astype(BF16)
            ark = jnp.where(incl, gm[c:, rn:], 0.0).astype(BF16)
        yield
        wy = _dot(jnp.concatenate([aak, ark], axis=0) if need_y else aak, v_s)
        w0 = wy[:c]

        t = eye + jnp.where(lvl_ref[0] > 0.0, gab, 0.0)
        for lvl in range(1, int(math.log2(c))):
            n_l = jnp.where(lvl_ref[lvl] > 0.0, gab, 0.0)
            u = _dot(t.astype(BF16), stack(n_l))
            yield
            t = t + _dot(u.astype(BF16), stack(t))
            yield
        t = t.astype(BF16)
        if need_y:
            t = jnp.concatenate([t, _dot(arb, stack(t)).astype(BF16)], axis=0)
            yield

        tv = _dot(t, jnp.concatenate([a_s, stack(w0)], axis=1))
        av = tv[:c]
        yield
        s = s_scr[q]
        sb = s.astype(BF16)
        if need_y:
            ry = tv[c:]
            rh = r_n + ry[:, :wd]
            yk = wy[c:]
        pd = _dot(av.T.astype(BF16), bo_n)
        dt = (pd[wd:] + _dot(v.T.astype(BF16), ko_n)) * bmf
        pt = (pd[:wd] * bmf).astype(BF16)
        yield
        if need_y:
            y_ref[:, sl] = _dot_nt(rh.astype(BF16), sb) + ry[:, wd:] + yk
        s_scr[q] = s * jnp.exp(gc) + _dot(sb, pt) + dt

    for q0 in range(0, nq, group):
        live = [block(q) for q in range(q0, min(q0 + group, nq))]
        while live:
            nxt = []
            for gen in live:
                try:
                    next(gen)
                    nxt.append(gen)
                except StopIteration:
                    pass
            live = nxt

    @pl.when(cc == pl.num_programs(1) - 1)
    def _():
        sf_ref[...] = s_scr[...]


def _wkv_consts(rev):
    c, g = WKV_CHUNK, WKV_HEADS_PER_BLOCK
    wd, rn = g * RWKV_HEAD, g * c
    assert c == RWKV_HEAD, "one (rn, wd) mask serves both (head, time) and (head, channel) columns"
    hrow = np.arange(rn) // c
    hlane = np.arange(wd) // RWKV_HEAD
    bm = (hrow[:, None] == hlane[None, :]).astype(np.float32)
    t = np.arange(c)[:, None]
    src = (np.arange(rn) % c)[None, :]
    if rev:
        strict, incl = src > t, src >= t
        tri = np.triu(np.ones((c, c), np.float32))
    else:
        strict, incl = src < t, src <= t
        tri = np.tril(np.ones((c, c), np.float32))
    ones = (hlane[:, None] == hlane[None, :]).astype(np.float32)
    levels = [strict & ((t >> (l + 1)) == (src >> (l + 1))) & ((t >> l) != (src >> l))
              for l in range(int(math.log2(c)))]
    return (jnp.asarray(bm, BF16), jnp.asarray(strict, F32), jnp.asarray(incl, F32), jnp.asarray(tri, BF16),
            jnp.asarray(ones, BF16), jnp.asarray(src == t, F32), jnp.asarray(np.stack(levels), F32))


def _wkv(r, k, v, lw, al, kkp, kap, s0, bsz, *, rev, need_y):
    n, d = r.shape
    c, g = WKV_CHUNK, WKV_HEADS_PER_BLOCK
    wd, rn = g * RWKV_HEAD, g * c
    nq = d // wd
    nc = n // bsz // c
    if rev:
        tok = lambda b, cc: (b * nc + (nc - 1 - cc), 0)
    else:
        tok = lambda b, cc: (b * nc + cc, 0)
    tspec = pl.BlockSpec((c, d), tok)
    full = lambda shape: pl.BlockSpec(shape, lambda b, cc: (0,) * len(shape))
    sspec = pl.BlockSpec((None, nq, wd, wd), lambda b, cc: (b, 0, 0, 0))
    consts = _wkv_consts(rev)
    y, sf = pl.pallas_call(
        functools.partial(_wkv_kernel, rev=rev, chunk=c, heads=g, nq=nq, need_y=need_y, group=WKV_GROUP),
        out_shape=[jax.ShapeDtypeStruct((n, d), F32), jax.ShapeDtypeStruct((bsz, nq, wd, wd), F32)],
        grid=(bsz, nc),
        in_specs=[tspec] * 5 + [full((1, d)), full((1, d)), sspec,
                                full((rn, wd)), full((c, rn)), full((c, rn)), full((c, c)), full((wd, wd)),
                                full((c, rn)), full((int(math.log2(c)), c, rn))],
        out_specs=[tspec, sspec],
        scratch_shapes=[pltpu.VMEM((nq, wd, wd), F32)],
        compiler_params=_cparams(("arbitrary", "arbitrary")),
        name="wkv_scan",
    )(r, k, v, lw, al, kkp, kap, s0, *consts)
    return y, sf


def _rwkv_out_kernel(yf_ref, yb_ref, bon_ref, g_ref, x_ref, gt_ref, lnw_ref, lnb_ref, ones_ref, wo_ref, o_ref,
                     *, wd):
    y = yf_ref[...] + yb_ref[...]
    ones = ones_ref[...]
    d = y.shape[1]
    inv = 1.0 / RWKV_HEAD

    def headsum(z):
        return jnp.concatenate([_dot_x2(z[:, q * wd:(q + 1) * wd], ones) for q in range(d // wd)], axis=1)

    dev = y - headsum(y) * inv
    var = headsum(dev * dev) * inv
    yn = dev * lax.rsqrt(var + GN_EPS) * lnw_ref[...] + lnb_ref[...]
    o = (yn + bon_ref[...].astype(F32)) * g_ref[...].astype(F32)
    o_ref[...] = x_ref[...] + gt_ref[...] * _dot(o.astype(BF16), wo_ref[...])


def _rwkv_out(yf, yb, bon, g, x, mods, row_of_tile, lnw, lnb, wo, *, tm):
    n, d = x.shape
    wd = WKV_HEADS_PER_BLOCK * RWKV_HEAD
    hl = np.arange(wd) // RWKV_HEAD
    ones = jnp.asarray(hl[:, None] == hl[None, :], BF16)
    tspec = pl.BlockSpec((tm, d), lambda i: (i, 0))
    full = lambda shape: pl.BlockSpec(shape, lambda i: (0,) * len(shape))
    return pl.pallas_call(
        functools.partial(_rwkv_out_kernel, wd=wd),
        out_shape=jax.ShapeDtypeStruct((n, d), F32),
        grid=(n // tm,),
        in_specs=[tspec] * 5 + [pl.BlockSpec((None, 1, d), lambda i: (row_of_tile(i) * N_MOD + 5, 0, 0)),
                                full((1, d)), full((1, d)), full((wd, wd)), full((d, d))],
        out_specs=tspec,
        compiler_params=_cparams(("arbitrary",)),
        name="rwkv_out",
    )(yf, yb, bon, g, x, mods, lnw, lnb, ones, wo)


def _rwkv_params(j, d, rwkv_mu, rwkv_w_rkv, rwkv_w0, rwkv_w1, rwkv_w2, rwkv_a0, rwkv_a1, rwkv_a2, rwkv_g1,
                 rwkv_g2, rwkv_k_a, rwkv_r_k):
    lr = rwkv_g1.shape[2]
    rank = rwkv_w1.shape[3]
    assert 2 * rank <= lr

    def first(w):
        return jnp.pad(jnp.concatenate([w[0], w[1]], axis=1), ((0, 0), (0, lr - 2 * rank))).astype(BF16)

    def second(w, z):
        return jnp.pad(w, ((z * rank, lr - (z + 1) * rank), (0, 0))).astype(BF16)

    vec = jnp.stack([rwkv_w0[j, 0], rwkv_w0[j, 1], rwkv_a0[j, 0], rwkv_a0[j, 1], rwkv_k_a[j],
                     rwkv_r_k[j].reshape(d), jnp.zeros((d,), F32), jnp.zeros((d,), F32)])
    hl = np.arange(256) // RWKV_HEAD
    return dict(
        mu=jnp.pad(rwkv_mu[j], ((0, 2), (0, 0))),
        w1c=first(rwkv_w1[j]), a1c=first(rwkv_a1[j]), g1=rwkv_g1[j].astype(BF16),
        wr=rwkv_w_rkv[j, 0].astype(BF16), wk=rwkv_w_rkv[j, 1].astype(BF16), wv=rwkv_w_rkv[j, 2].astype(BF16),
        w2f=second(rwkv_w2[j, 0], 0), w2b=second(rwkv_w2[j, 1], 1),
        a2f=second(rwkv_a2[j, 0], 0), a2b=second(rwkv_a2[j, 1], 1),
        g2=rwkv_g2[j].astype(BF16), vec=vec,
        ones_head=jnp.asarray(hl[:, None] == hl[None, :], BF16),
    )


def kernel(x, c, ctx, c_ctx, mod_w, mod_b, norm_w, ffn_w13, ffn_w2, fnet_w_o, fnet_b_o, rwkv_mu, rwkv_w_rkv,
           rwkv_w0, rwkv_w1, rwkv_w2, rwkv_a0, rwkv_a1, rwkv_a2, rwkv_g1, rwkv_g2, rwkv_k_k, rwkv_k_a, rwkv_r_k,
           rwkv_ln_w, rwkv_ln_b, rwkv_w_o, final_norm_w):
    bsz, seq, d = x.shape
    lc = ctx.shape[1]
    depth = mod_w.shape[0]
    assert depth == 2 and bsz == 2, "layer schedule below is written for the two-layer, batch-2 trunk"
    tm = TOKEN_TILE
    assert seq % tm == 0 and seq % GRID_W == 0

    c8 = jnp.concatenate([c, c_ctx[None], jnp.zeros((8 - bsz - 1, d), F32)], axis=0)
    mods_all = _adaln(c8, mod_w, mod_b)
    f2 = ffn_w13.shape[-1]
    w13 = ffn_w13.astype(BF16).reshape(2 * depth, d, f2)
    w2 = ffn_w2.astype(BF16).reshape(2 * depth, f2 // 2, d)
    fw = final_norm_w.reshape(1, d)

    xl = x.reshape(bsz * seq, d)
    xc = ctx.reshape(bsz * lc, d)
    lat_row = lambda i: i // (seq // tm)
    ctx_row = lambda i: 2
    tmc = bsz * lc if bsz * lc <= tm else lc

    mods = mods_all[0, :3].reshape(3 * N_MOD, 1, d)
    nw = norm_w[0].reshape(3, 1, d)
    xl, hl = _ffn(xl, mods, lat_row, 0, nw[0], w13, w2, 0, fw, tm=tm, next_nw=nw[1])
    xc, hc = _ffn(xc, mods, ctx_row, 0, nw[0], w13, w2, 0, fw, tm=tmc, next_nw=nw[1])
    wo = fnet_w_o[0].astype(BF16)
    bo = fnet_b_o[0].reshape(1, d)
    xl = _fourier_latent(xl, hl, bsz, seq, mods, wo, bo)
    xc = _fourier_ctx(xc, hc, bsz, lc, mods, wo, bo)
    xl = _ffn(xl, mods, lat_row, 2, nw[2], w13, w2, 1, fw, tm=tm)
    xc = _ffn(xc, mods, ctx_row, 2, nw[2], w13, w2, 1, fw, tm=tmc)

    mods = mods_all[1, :3].reshape(3 * N_MOD, 1, d)
    nw = norm_w[1].reshape(3, 1, d)
    xl = _ffn(xl, mods, lat_row, 0, nw[0], w13, w2, 2, fw, tm=tm)
    xc = _ffn(xc, mods, ctx_row, 0, nw[0], w13, w2, 2, fw, tm=tmc)
    prm = _rwkv_params(0, d, rwkv_mu, rwkv_w_rkv, rwkv_w0, rwkv_w1, rwkv_w2, rwkv_a0, rwkv_a1, rwkv_a2, rwkv_g1,
                       rwkv_g2, rwkv_k_a, rwkv_r_k)
    r_c, k_c, v_c, _, lwf_c, lwb_c, alf_c, alb_c, _ = _rwkv_proj(
        xc, mods, ctx_row, nw[1], prm, tm=lc, grid_mode=False, tpb=1)
    r_l, k_l, v_l, g_l, lwf_l, lwb_l, alf_l, alb_l, bon_l = _rwkv_proj(
        xl, mods, lat_row, nw[1], prm, tm=tm, grid_mode=True, tpb=seq // tm)
    kkp = rwkv_k_k[0].reshape(1, d)
    kap = rwkv_k_a[0].reshape(1, d)
    wd = WKV_HEADS_PER_BLOCK * RWKV_HEAD
    s0 = jnp.zeros((bsz, d // wd, wd, wd), F32)
    _, s_f = _wkv(r_c, k_c, v_c, lwf_c, alf_c, kkp, kap, s0, bsz, rev=False, need_y=False)
    _, s_b = _wkv(r_c, k_c, v_c, lwb_c, alb_c, kkp, kap, s0, bsz, rev=True, need_y=False)
    yf, _ = _wkv(r_l, k_l, v_l, lwf_l, alf_l, kkp, kap, s_f, bsz, rev=False, need_y=True)
    yb, _ = _wkv(r_l, k_l, v_l, lwb_l, alb_l, kkp, kap, s_b, bsz, rev=True, need_y=True)
    xl = _rwkv_out(yf, yb, bon_l, g_l, xl, mods, lambda i: i // (seq // MIX_TILE), rwkv_ln_w[0].reshape(1, d),
                   rwkv_ln_b[0].reshape(1, d), rwkv_w_o[0].astype(BF16), tm=MIX_TILE)
    xl = _ffn(xl, mods, lat_row, 2, nw[2], w13, w2, 3, fw, tm=tm, final_norm=True)
    return xl.reshape(bsz, seq, d)
```

```python
import functools
import math

import numpy as np
import jax
import jax.numpy as jnp
from jax import lax
from jax.experimental import pallas as pl
from jax.experimental.pallas import tpu as pltpu

F32 = jnp.float32
BF16 = jnp.bfloat16

NORM_EPS = 1e-6
GN_EPS = 64e-5
KK_EPS = 1e-24
GRID_W = 64
FNET_GROUPS = 8
RWKV_HEAD = 64
N_MOD = 9
DECAY_SCALE = math.exp(-0.5)

LANE = 128
TOKEN_TILE = 512
MIX_TILE = 256
FFN_HIDDEN_TILE = 512
PROJ_COL_TILE = 256
HEADSUM_TILE = 256
ROW_CHUNK = 16
ROW_UNROLL = 8
WKV_CHUNK = 64
WKV_HEADS_PER_BLOCK = 2
WKV_GROUP = 16
VMEM_LIMIT = 56 * 1024 * 1024


def _cparams(sem):
    return pltpu.CompilerParams(dimension_semantics=sem, vmem_limit_bytes=VMEM_LIMIT)


def _sigmoid(x):
    return 1.0 / (1.0 + jnp.exp(-x))


def _modulate(x, nw, shift, scale):
    y = x * lax.rsqrt(jnp.mean(x * x, axis=-1, keepdims=True) + NORM_EPS)
    return (y * nw) * (1.0 + scale) + shift


def _dot(a, b):
    return jnp.dot(a, b, preferred_element_type=F32)


def _dot_nt(a, b):
    return lax.dot_general(a, b, (((1,), (1,)), ((), ())), preferred_element_type=F32)


def _split2(x):
    hi = x.astype(BF16)
    lo = (x - hi.astype(F32)).astype(BF16)
    return hi, lo


def _dot_x2(x, w_bf16):
    hi, lo = _split2(x)
    return _dot(hi, w_bf16) + _dot(lo, w_bf16)


def _adaln_kernel(c_ref, w_ref, b_ref, o_ref):
    c = c_ref[...]
    s = c * _sigmoid(c)
    o_ref[...] = _dot(s.astype(BF16), w_ref[...].astype(BF16)) + b_ref[...]


def _adaln(c8, mod_w, mod_b):
    depth, d, nd = mod_w.shape
    tn = 1024 if nd % 1024 == 0 else 512
    return pl.pallas_call(
        _adaln_kernel,
        out_shape=jax.ShapeDtypeStruct((depth, 8, nd), F32),
        grid=(depth, nd // tn),
        in_specs=[
            pl.BlockSpec((8, d), lambda l, j: (0, 0)),
            pl.BlockSpec((None, d, tn), lambda l, j: (l, 0, j)),
            pl.BlockSpec((None, 1, tn), lambda l, j: (l, 0, j)),
        ],
        out_specs=pl.BlockSpec((None, 8, tn), lambda l, j: (l, 0, j)),
        compiler_params=_cparams(("arbitrary", "arbitrary")),
        name="adaln_linear",
    )(c8, mod_w, mod_b.reshape(depth, 1, nd))


def _ffn_kernel(*refs, final_norm, emit_h):
    x_ref, sh_ref, sc_ref, gt_ref, nw_ref, w1g_ref, w1u_ref, w2_ref, fw_ref = refs[:9]
    if emit_h:
        nw2_ref, sh2_ref, sc2_ref, o_ref, h2_ref, h_ref = refs[9:]
    else:
        o_ref, h_ref = refs[9:]
    j = pl.program_id(1)
    n_chunks = x_ref.shape[0] // ROW_CHUNK

    def rows_of(c):
        return pl.ds(pl.multiple_of(c * ROW_CHUNK, ROW_CHUNK), ROW_CHUNK)

    @pl.when(j == 0)
    def _():
        nw, sh, sc = nw_ref[...], sh_ref[...], sc_ref[...]

        def chunk(c, carry):
            rows = rows_of(c)
            h_ref[rows, :] = _modulate(x_ref[rows, :], nw, sh, sc).astype(BF16)
            o_ref[rows, :] = jnp.zeros((ROW_CHUNK, o_ref.shape[1]), F32)
            return carry

        lax.fori_loop(0, n_chunks, chunk, 0, unroll=ROW_UNROLL)

    h = h_ref[...]
    g = _dot(h, w1g_ref[...])
    u = _dot(h, w1u_ref[...])
    a = (g * _sigmoid(g)) * u
    o_ref[...] += _dot(a.astype(BF16), w2_ref[...])

    @pl.when(j == pl.num_programs(1) - 1)
    def _():
        y = x_ref[...] + (0.5 * gt_ref[...]) * o_ref[...]
        if final_norm:
            y = y * lax.rsqrt(jnp.mean(y * y, axis=-1, keepdims=True) + NORM_EPS) * fw_ref[...]
        o_ref[...] = y
        if emit_h:
            h2_ref[...] = _modulate(y, nw2_ref[...], sh2_ref[...], sc2_ref[...])


def _ffn(x, mods, row_of_tile, sub, nw, w13, w2, wi, fw, *, tm, final_norm=False, next_nw=None):
    n, d = x.shape
    f = w2.shape[1]
    tf = FFN_HIDDEN_TILE
    nf = f // tf
    emit_h = next_nw is not None

    def mod_spec(k):
        return pl.BlockSpec((None, 1, d), lambda i, j: (row_of_tile(i) * N_MOD + 3 * sub + k, 0, 0))

    vec_spec = pl.BlockSpec((1, d), lambda i, j: (0, 0))
    tile_spec = pl.BlockSpec((tm, d), lambda i, j: (i, 0))
    out = jax.ShapeDtypeStruct((n, d), F32)
    args = [x, mods, mods, mods, nw, w13, w13, w2, fw]
    specs = [tile_spec, mod_spec(0), mod_spec(1), mod_spec(2), vec_spec,
             pl.BlockSpec((None, d, tf), lambda i, j: (wi, 0, j)),
             pl.BlockSpec((None, d, tf), lambda i, j: (wi, 0, nf + j)),
             pl.BlockSpec((None, tf, d), lambda i, j: (wi, j, 0)),
             vec_spec]
    if emit_h:
        args += [next_nw, mods, mods]
        specs += [vec_spec, mod_spec(3), mod_spec(4)]
    return pl.pallas_call(
        functools.partial(_ffn_kernel, final_norm=final_norm, emit_h=emit_h),
        out_shape=[out, out] if emit_h else out,
        grid=(n // tm, nf),
        in_specs=specs,
        out_specs=[tile_spec, tile_spec] if emit_h else tile_spec,
        scratch_shapes=[pltpu.VMEM((tm, d), BF16)],
        compiler_params=_cparams(("arbitrary", "arbitrary")),
        name="ffn_swiglu",
    )(*args)


def _dft_tables(n):
    k = np.arange(n, dtype=np.int64)
    ang = 2.0 * np.pi * ((k[:, None] * k[None, :]) % n).astype(np.float64) / n
    return np.cos(ang), np.sin(ang)


def _hi_lo(m):
    m = jnp.asarray(m, F32)
    hi = m.astype(BF16)
    return hi, (m - hi.astype(F32)).astype(BF16)


DFT_SUB = 8
DFT_COLS = 1024
DFT_COLS_B = 512


def _dft_kernel(*refs, n_planes, mode, twiddle):
    it = iter(refs)
    x_ref = next(it)
    mats = [(next(it)[...], next(it)[...]) for _ in range(n_planes)]
    if twiddle:
        tc_ref, ts_ref = next(it), next(it)
    o_ref = next(it)
    if mode != "flat":
        stage_ref = next(it)

    def dense(p, strided):
        stage_ref[p] = strided
        return stage_ref[p]

    def transform(planes):
        y = None
        for x, (mh, ml) in zip(planes, mats):
            xh, xl = _split2(x)
            part = _dot(mh, xh) + (_dot(mh, xl) + _dot(ml, xh))
            y = part if y is None else y + part
        half = y.shape[0] // 2
        return y[:half], y[half:]

    if mode == "flat":
        yr, yi = transform([x_ref[...]])
        o_ref[0] = yr
        o_ref[1] = yi
    elif mode == "a":
        for j in range(DFT_SUB):
            yr, yi = transform([dense(0, x_ref[:, j, :])])
            if twiddle:
                reps = yr.shape[1] // LANE
                c = jnp.concatenate([tc_ref[j]] * reps, axis=1)
                s = jnp.concatenate([ts_ref[j]] * reps, axis=1)
                yr, yi = yr * c + yi * s, yi * c - yr * s
            o_ref[0, j] = yr
            o_ref[1, j] = yi
    else:
        for j in range(DFT_SUB):
            yr, yi = transform([dense(0, x_ref[0, :, j, :]), dense(1, x_ref[1, :, j, :])])
            o_ref[0, :, j, :] = yr
            o_ref[1, :, j, :] = yi


def _dft_call(mode, x, mats, out_shape, grid, x_spec, o_spec, tw=None, tw_spec=None):
    args, specs = [x], [x_spec]
    for m in mats:
        mh, ml = _hi_lo(m)
        args += [mh, ml]
        specs += [pl.BlockSpec(m.shape, lambda *_: (0, 0))] * 2
    if tw is not None:
        args += list(tw)
        specs += [tw_spec] * 2
    scratch = []
    if mode != "flat":
        scratch = [pltpu.VMEM((len(mats), mats[0].shape[1], x_spec.block_shape[-1]), F32)]
    return pl.pallas_call(
        functools.partial(_dft_kernel, n_planes=len(mats), mode=mode, twiddle=tw is not None),
        out_shape=jax.ShapeDtypeStruct(out_shape, F32),
        grid=grid,
        in_specs=specs,
        out_specs=o_spec,
        scratch_shapes=scratch,
        compiler_params=_cparams(("arbitrary",) * len(grid)),
        name="dft_" + mode,
    )(*args)


def _fmix_kernel(pr_ref, pi_ref, x_ref, gt_ref, ch_ref, cl_ref, wo_ref, bo_ref, o_ref, *, groups):
    pr, pi = pr_ref[...], pi_ref[...]
    gd = pr.shape[1] // groups
    ch, cl = ch_ref[...], cl_ref[...]
    outs = []
    for g in range(groups):
        z = jnp.concatenate([pr[:, g * gd:(g + 1) * gd], pi[:, g * gd:(g + 1) * gd]], axis=1)
        zh, zl = _split2(z)
        outs.append(_dot(zh, ch) + (_dot(zl, ch) + _dot(zh, cl)))
    f = jnp.concatenate(outs, axis=1)
    o = _dot(f.astype(BF16), wo_ref[...]) + bo_ref[...]
    o_ref[...] = x_ref[...] + gt_ref[...] * o


def _fmix(p, x, mods, row_of_tile, wo, bo, *, tm):
    n, d = x.shape
    b, _, l, _ = p.shape
    tpb = l // tm
    gd = d // FNET_GROUPS
    c, s = _dft_tables(gd)
    ch, cl = _hi_lo(np.concatenate([c, s], axis=0) / math.sqrt(gd))
    return pl.pallas_call(
        functools.partial(_fmix_kernel, groups=FNET_GROUPS),
        out_shape=jax.ShapeDtypeStruct((n, d), F32),
        grid=(n // tm,),
        in_specs=[
            pl.BlockSpec((None, None, tm, d), lambda i: (i // tpb, 0, i % tpb, 0)),
            pl.BlockSpec((None, None, tm, d), lambda i: (i // tpb, 1, i % tpb, 0)),
            pl.BlockSpec((tm, d), lambda i: (i, 0)),
            pl.BlockSpec((None, 1, d), lambda i: (row_of_tile(i) * N_MOD + 5, 0, 0)),
            pl.BlockSpec((2 * gd, gd), lambda i: (0, 0)),
            pl.BlockSpec((2 * gd, gd), lambda i: (0, 0)),
            pl.BlockSpec((d, d), lambda i: (0, 0)),
            pl.BlockSpec((1, d), lambda i: (0, 0)),
        ],
        out_specs=pl.BlockSpec((tm, d), lambda i: (i, 0)),
        compiler_params=_cparams(("arbitrary",)),
        name="fourier_out",
    )(p, p, x, mods, ch, cl, wo, bo)


def _fourier_latent(xl, hl, bsz, seq, mods, wo, bo):
    n, d = xl.shape
    la = lb = int(round(math.sqrt(seq)))
    assert la * lb == seq and lb % DFT_SUB == 0 and la % DFT_SUB == 0
    td = min(DFT_COLS, d)
    tdb = min(DFT_COLS_B, d)
    s8 = DFT_SUB
    ca, sa = _dft_tables(la)
    m_a = np.concatenate([ca, -sa], axis=0) / math.sqrt(la)
    cb, sb = _dft_tables(lb)
    m_br = np.concatenate([cb, -sb], axis=0) / math.sqrt(lb)
    m_bi = np.concatenate([sb, cb], axis=0) / math.sqrt(lb)
    n2 = np.arange(lb, dtype=np.int64)[:, None]
    k1 = np.arange(la, dtype=np.int64)[None, :]
    ang = 2.0 * np.pi * ((n2 * k1) % seq).astype(np.float64) / seq
    twc = jnp.asarray(np.broadcast_to(np.cos(ang)[:, :, None], (lb, la, LANE)), F32)
    tws = jnp.asarray(np.broadcast_to(np.sin(ang)[:, :, None], (lb, la, LANE)), F32)
    grid = (bsz, lb // s8, d // td)
    a = _dft_call(
        "a", hl.reshape(bsz, la, lb, d), [m_a], (bsz, 2, lb, la, d), grid,
        pl.BlockSpec((None, la, s8, td), lambda b, g, c: (b, 0, g, c)),
        pl.BlockSpec((None, 2, s8, la, td), lambda b, g, c: (b, 0, g, 0, c)),
        tw=(twc, tws), tw_spec=pl.BlockSpec((s8, la, LANE), lambda b, g, c: (g, 0, 0)))
    p = _dft_call(
        "b", a, [m_br, m_bi], (bsz, 2, lb, la, d), (bsz, la // s8, d // tdb),
        pl.BlockSpec((None, 2, lb, s8, tdb), lambda b, g, c: (b, 0, 0, g, c)),
        pl.BlockSpec((None, 2, lb, s8, tdb), lambda b, g, c: (b, 0, 0, g, c)))
    p = p.reshape(bsz, 2, seq, d)
    return _fmix(p, xl, mods, lambda i: i // (seq // MIX_TILE), wo, bo, tm=MIX_TILE)


def _fourier_ctx(xc, hc, bsz, lc, mods, wo, bo):
    n, d = xc.shape
    td = min(DFT_COLS, d)
    c, s = _dft_tables(lc)
    m = np.concatenate([c, -s], axis=0) / math.sqrt(lc)
    p = _dft_call(
        "flat", hc.reshape(bsz, lc, d), [m], (bsz, 2, lc, d), (bsz, d // td),
        pl.BlockSpec((None, lc, td), lambda b, c: (b, 0, c)),
        pl.BlockSpec((None, 2, lc, td), lambda b, c: (b, 0, 0, c)))
    return _fmix(p, xc, mods, lambda i: 2, wo, bo, tm=lc)


def _rwkv_proj_kernel(*refs, grid_mode, tpb):
    it = iter(refs)
    x_ref = next(it)
    if grid_mode:
        xp_ref, xn_ref = next(it), next(it)
    nw_ref, sh_ref, sc_ref, mu_ref = next(it), next(it), next(it), next(it)
    w1_ref, a1_ref, g1_ref = next(it), next(it), next(it)
    wr_ref, wk_ref, wv_ref = next(it), next(it), next(it)
    w2f_ref, w2b_ref, a2f_ref, a2b_ref, g2_ref = next(it), next(it), next(it), next(it), next(it)
    vec_ref, ones_ref = next(it), next(it)
    r_ref, k_ref, v_ref, g_ref, lwf_ref, lwb_ref, alf_ref, alb_ref, bon_ref = (next(it) for _ in range(9))
    hr_ref, hk_ref, hv_ref, tw_ref, ta_ref, tg_ref = (next(it) for _ in range(6))

    i = pl.program_id(0)
    j = pl.program_id(1)

    @pl.when(j == 0)
    def _():
        nw, sh, sc = nw_ref[...], sh_ref[...], sc_ref[...]
        h = _modulate(x_ref[...], nw, sh, sc)
        tm, d = h.shape
        q = d // 4
        row = lax.broadcasted_iota(jnp.int32, (tm, 1), 0)
        if grid_mode:
            tib = i % tpb
            up_ok = (tib != 0).astype(F32)
            dn_ok = (tib != tpb - 1).astype(F32)
            hp = _modulate(xp_ref[...], nw, sh, sc)[:, 2 * q:3 * q] * up_ok
            hn = _modulate(xn_ref[...], nw, sh, sc)[:, 3 * q:] * dn_ok
            col = row % GRID_W
            left = jnp.where(col != 0, pltpu.roll(h[:, :q], 1, 0), 0.0)
            right = jnp.where(col != GRID_W - 1, pltpu.roll(h[:, q:2 * q], tm - 1, 0), 0.0)
            up = jnp.concatenate([hp, h[:tm - GRID_W, 2 * q:3 * q]], axis=0)
            down = jnp.concatenate([h[GRID_W:, 3 * q:], hn], axis=0)
            hs = jnp.concatenate([left, right, up, down], axis=1)
        else:
            prev = jnp.where(row != 0, pltpu.roll(h, 1, 0), 0.0)
            nxt = jnp.where(row != tm - 1, pltpu.roll(h, tm - 1, 0), 0.0)
            hs = jnp.concatenate([prev[:, :q], nxt[:, q:2 * q], prev[:, 2 * q:3 * q], nxt[:, 3 * q:]], axis=1)
        dlt = hs - h
        mu = mu_ref[...]
        hr_ref[...] = (h + dlt * mu[0:1]).astype(BF16)
        hk_ref[...] = (h + dlt * mu[1:2]).astype(BF16)
        hv_ref[...] = (h + dlt * mu[2:3]).astype(BF16)
        tw_ref[...] = jnp.tanh(_dot((h + dlt * mu[3:4]).astype(BF16), w1_ref[...])).astype(BF16)
        ta_ref[...] = _dot((h + dlt * mu[4:5]).astype(BF16), a1_ref[...]).astype(BF16)
        tg_ref[...] = _sigmoid(_dot((h + dlt * mu[5:6]).astype(BF16), g1_ref[...])).astype(BF16)

    vec = vec_ref[...]
    w0f, w0b, a0f, a0b, ka, rk = (vec[n:n + 1] for n in range(6))
    r = _dot(hr_ref[...], wr_ref[...])
    k = _dot(hk_ref[...], wk_ref[...])
    v = _dot(hv_ref[...], wv_ref[...])
    tw = tw_ref[...]
    ta = ta_ref[...]
    lwf = -DECAY_SCALE * _sigmoid(w0f + _dot(tw, w2f_ref[...]))
    lwb = -DECAY_SCALE * _sigmoid(w0b + _dot(tw, w2b_ref[...]))
    alf = _sigmoid(a0f + _dot(ta, a2f_ref[...]))
    alb = _sigmoid(a0b + _dot(ta, a2b_ref[...]))
    kb = k * (1.0 + (0.5 * (alf + alb) - 1.0) * ka)
    r_ref[...] = r.astype(r_ref.dtype)
    k_ref[...] = k.astype(k_ref.dtype)
    v_ref[...] = v.astype(v_ref.dtype)
    g_ref[...] = _dot(tg_ref[...], g2_ref[...]).astype(g_ref.dtype)
    lwf_ref[...] = lwf
    lwb_ref[...] = lwb
    alf_ref[...] = alf.astype(alf_ref.dtype)
    alb_ref[...] = alb.astype(alb_ref.dtype)
    bon_ref[...] = (_dot_x2(r * kb * rk, ones_ref[...]) * v).astype(bon_ref.dtype)


def _rwkv_proj(x, mods, row_of_tile, nw, prm, *, tm, grid_mode, tpb):
    n, d = x.shape
    tn = PROJ_COL_TILE
    nj = d // tn
    hw = GRID_W
    lr = prm["w1c"].shape[1]

    def mod_spec(k):
        return pl.BlockSpec((None, 1, d), lambda i, j: (row_of_tile(i) * N_MOD + 3 + k, 0, 0))

    args, specs = [x], [pl.BlockSpec((tm, d), lambda i, j: (i, 0))]
    if grid_mode:
        nb = n // hw
        r = tm // hw
        args += [x, x]
        specs += [pl.BlockSpec((hw, d), lambda i, j: (jnp.maximum(i * r - 1, 0), 0)),
                  pl.BlockSpec((hw, d), lambda i, j: (jnp.minimum((i + 1) * r, nb - 1), 0))]
    args += [nw, mods, mods, prm["mu"], prm["w1c"], prm["a1c"], prm["g1"], prm["wr"], prm["wk"], prm["wv"],
             prm["w2f"], prm["w2b"], prm["a2f"], prm["a2b"], prm["g2"], prm["vec"], prm["ones_head"]]
    full = lambda shape: pl.BlockSpec(shape, lambda i, j: (0,) * len(shape))
    coltile = lambda rows: pl.BlockSpec((rows, tn), lambda i, j: (0, j))
    specs += [full((1, d)), mod_spec(0), mod_spec(1), full((8, d)), full((d, lr)), full((d, lr)), full((d, lr)),
              coltile(d), coltile(d), coltile(d), coltile(lr), coltile(lr), coltile(lr), coltile(lr), coltile(lr),
              coltile(8), full((tn, tn))]
    lo, hi = jax.ShapeDtypeStruct((n, d), BF16), jax.ShapeDtypeStruct((n, d), F32)
    return pl.pallas_call(
        functools.partial(_rwkv_proj_kernel, grid_mode=grid_mode, tpb=tpb),
        out_shape=[lo, lo, lo, lo, hi, hi, lo, lo, lo],
        grid=(n // tm, nj),
        in_specs=specs,
        out_specs=[pl.BlockSpec((tm, tn), lambda i, j: (i, j))] * 9,
        scratch_shapes=[pltpu.VMEM((tm, d), BF16)] * 3 + [pltpu.VMEM((tm, lr), BF16)] * 3,
        compiler_params=_cparams(("arbitrary", "arbitrary")),
        name="rwkv_proj",
    )(*args)


def _wkv_kernel(r_ref, k_ref, v_ref, lw_ref, al_ref, kk_ref, ka_ref, s0_ref, bm_ref, ms_ref, mi_ref, tri_ref,
                ones_ref, eye_ref, lvl_ref, y_ref, sf_ref, s_scr, *, rev, chunk, heads, nq, need_y, group):
    cc = pl.program_id(1)
    c = chunk
    wd = heads * RWKV_HEAD
    rn = heads * c

    @pl.when(cc == 0)
    def _():
        s_scr[...] = s0_ref[...]

    bm = bm_ref[...]
    bmf = bm.astype(F32)
    strict = ms_ref[...] > 0.0
    incl = mi_ref[...] > 0.0
    tri = tri_ref[...]
    ones = ones_ref[...]
    eye = eye_ref[...]

    def stack(x):
        return jnp.concatenate([x.astype(BF16)] * heads, axis=0) * bm

    if not need_y:
        y_ref[...] = jnp.zeros_like(y_ref)

    def block(q):
        sl = slice(q * wd, (q + 1) * wd)
        r, k, v = r_ref[:, sl].astype(F32), k_ref[:, sl].astype(F32), v_ref[:, sl].astype(F32)
        lw, al = lw_ref[:, sl], al_ref[:, sl].astype(F32)
        kk0 = k * kk_ref[:, sl]
        kk = kk0 * lax.rsqrt(jnp.maximum(_dot_x2(kk0 * kk0, ones), KK_EPS))
        b = kk * al
        kd = k * (1.0 + (al - 1.0) * ka_ref[:, sl])
        lw_hi, lw_lo = _split2(lw)
        lg = _dot(tri, lw_hi) + _dot(tri, lw_lo)
        gc = lg[0:1] if rev else lg[c - 1:c]
        gin = jnp.exp(-lg)
        gout = jnp.exp(gc - lg)
        a_n = -kk * jnp.exp(lg - lw)
        r_n = r * jnp.exp(lg)
        a_s = stack(a_n)
        v_s = stack(v)
        bk = jnp.concatenate([stack(b * gin), stack(kd * gin)], axis=0)
        bo_n = (b * gout).astype(BF16)
        ko_n = (kd * gout).astype(BF16)
        yield

        lhs = jnp.concatenate([a_n, r_n], axis=0) if need_y else a_n
        gm = _dot_nt(lhs.astype(BF16), bk)
        gab = gm[:c, :rn]
        aak = jnp.where(strict, gm[:c, rn:], 0.0).astype(BF16)
        if need_y:
            arb = jnp.where(incl, gm[c:, :rn], 0.0).astype(BF16)
            ark = jnp.where(incl, gm[c:, rn:], 0.0).astype(BF16)
        yield
        wy = _dot(jnp.concatenate([aak, ark], axis=0) if need_y else aak, v_s)
        w0 = wy[:c]

        t = eye + jnp.where(lvl_ref[0] > 0.0, gab, 0.0)
        for lvl in range(1, int(math.log2(c))):
            n_l = jnp.where(lvl_ref[lvl] > 0.0, gab, 0.0)
            u = _dot(t.astype(BF16), stack(n_l))
            yield
            t = t + _dot(u.astype(BF16), stack(t))
            yield
        t = t.astype(BF16)
        av = _dot(t, jnp.concatenate([a_s, stack(w0)], axis=1))
        yield
        s = s_scr[q]
        sb = s.astype(BF16)
        if need_y:
            ry = _dot(arb, jnp.concatenate([stack(av[:, :wd]), stack(av[:, wd:])], axis=1))
            rh = r_n + ry[:, :wd]
            yk = wy[c:]
        pd = _dot(av.T.astype(BF16), bo_n)
        dt = (pd[wd:] + _dot(v.T.astype(BF16), ko_n)) * bmf
        pt = (pd[:wd] * bmf).astype(BF16)
        yield
        if need_y:
            y_ref[:, sl] = _dot_nt(rh.astype(BF16), sb) + ry[:, wd:] + yk
        s_scr[q] = s * jnp.exp(gc) + _dot(sb, pt) + dt

    for q0 in range(0, nq, group):
        live = [block(q) for q in range(q0, min(q0 + group, nq))]
        while live:
            nxt = []
            for gen in live:
                try:
                    next(gen)
                    nxt.append(gen)
                except StopIteration:
                    pass
            live = nxt

    @pl.when(cc == pl.num_programs(1) - 1)
    def _():
        sf_ref[...] = s_scr[...]


def _wkv_consts(rev):
    c, g = WKV_CHUNK, WKV_HEADS_PER_BLOCK
    wd, rn = g * RWKV_HEAD, g * c
    assert c == RWKV_HEAD, "one (rn, wd) mask serves both (head, time) and (head, channel) columns"
    hrow = np.arange(rn) // c
    hlane = np.arange(wd) // RWKV_HEAD
    bm = (hrow[:, None] == hlane[None, :]).astype(np.float32)
    t = np.arange(c)[:, None]
    src = (np.arange(rn) % c)[None, :]
    if rev:
        strict, incl = src > t, src >= t
        tri = np.triu(np.ones((c, c), np.float32))
    else:
        strict, incl = src < t, src <= t
        tri = np.tril(np.ones((c, c), np.float32))
    ones = (hlane[:, None] == hlane[None, :]).astype(np.float32)
    levels = [strict & ((t >> (l + 1)) == (src >> (l + 1))) & ((t >> l) != (src >> l))
              for l in range(int(math.log2(c)))]
    return (jnp.asarray(bm, BF16), jnp.asarray(strict, F32), jnp.asarray(incl, F32), jnp.asarray(tri, BF16),
            jnp.asarray(ones, BF16), jnp.asarray(src == t, F32), jnp.asarray(np.stack(levels), F32))


def _wkv(r, k, v, lw, al, kkp, kap, s0, bsz, *, rev, need_y):
    n, d = r.shape
    c, g = WKV_CHUNK, WKV_HEADS_PER_BLOCK
    wd, rn = g * RWKV_HEAD, g * c
    nq = d // wd
    nc = n // bsz // c
    if rev:
        tok = lambda b, cc: (b * nc + (nc - 1 - cc), 0)
    else:
        tok = lambda b, cc: (b * nc + cc, 0)
    tspec = pl.BlockSpec((c, d), tok)
    full = lambda shape: pl.BlockSpec(shape, lambda b, cc: (0,) * len(shape))
    sspec = pl.BlockSpec((None, nq, wd, wd), lambda b, cc: (b, 0, 0, 0))
    consts = _wkv_consts(rev)
    y, sf = pl.pallas_call(
        functools.partial(_wkv_kernel, rev=rev, chunk=c, heads=g, nq=nq, need_y=need_y, group=WKV_GROUP),
        out_shape=[jax.ShapeDtypeStruct((n, d), F32), jax.ShapeDtypeStruct((bsz, nq, wd, wd), F32)],
        grid=(bsz, nc),
        in_specs=[tspec] * 5 + [full((1, d)), full((1, d)), sspec,
                                full((rn, wd)), full((c, rn)), full((c, rn)), full((c, c)), full((wd, wd)),
                                full((c, rn)), full((int(math.log2(c)), c, rn))],
        out_specs=[tspec, sspec],
        scratch_shapes=[pltpu.VMEM((nq, wd, wd), F32)],
        compiler_params=_cparams(("arbitrary", "arbitrary")),
        name="wkv_scan",
    )(r, k, v, lw, al, kkp, kap, s0, *consts)
    return y, sf


def _rwkv_out_kernel(yf_ref, yb_ref, bon_ref, g_ref, x_ref, gt_ref, lnw_ref, lnb_ref, ones_ref, wo_ref, o_ref,
                     *, wd):
    y = yf_ref[...] + yb_ref[...]
    ones = ones_ref[...]
    d = y.shape[1]
    inv = 1.0 / RWKV_HEAD

    def headsum(z, exact):
        parts = [z[:, q * wd:(q + 1) * wd] for q in range(d // wd)]
        return jnp.concatenate([_dot_x2(p, ones) if exact else _dot(p.astype(BF16), ones) for p in parts], axis=1)

    dev = y - headsum(y, True) * inv
    var = headsum(dev * dev, False) * inv
    yn = dev * lax.rsqrt(var + GN_EPS) * lnw_ref[...] + lnb_ref[...]
    o = (yn + bon_ref[...].astype(F32)) * g_ref[...].astype(F32)
    o_ref[...] = x_ref[...] + gt_ref[...] * _dot(o.astype(BF16), wo_ref[...])


def _rwkv_out(yf, yb, bon, g, x, mods, row_of_tile, lnw, lnb, wo, *, tm):
    n, d = x.shape
    wd = HEADSUM_TILE
    hl = np.arange(wd) // RWKV_HEAD
    ones = jnp.asarray(hl[:, None] == hl[None, :], BF16)
    tspec = pl.BlockSpec((tm, d), lambda i: (i, 0))
    full = lambda shape: pl.BlockSpec(shape, lambda i: (0,) * len(shape))
    return pl.pallas_call(
        functools.partial(_rwkv_out_kernel, wd=wd),
        out_shape=jax.ShapeDtypeStruct((n, d), F32),
        grid=(n // tm,),
        in_specs=[tspec] * 5 + [pl.BlockSpec((None, 1, d), lambda i: (row_of_tile(i) * N_MOD + 5, 0, 0)),
                                full((1, d)), full((1, d)), full((wd, wd)), full((d, d))],
        out_specs=tspec,
        compiler_params=_cparams(("arbitrary",)),
        name="rwkv_out",
    )(yf, yb, bon, g, x, mods, lnw, lnb, ones, wo)


def _rwkv_params(j, d, rwkv_mu, rwkv_w_rkv, rwkv_w0, rwkv_w1, rwkv_w2, rwkv_a0, rwkv_a1, rwkv_a2, rwkv_g1,
                 rwkv_g2, rwkv_k_a, rwkv_r_k):
    lr = rwkv_g1.shape[2]
    rank = rwkv_w1.shape[3]
    assert 2 * rank <= lr

    def first(w):
        return jnp.pad(jnp.concatenate([w[0], w[1]], axis=1), ((0, 0), (0, lr - 2 * rank))).astype(BF16)

    def second(w, z):
        return jnp.pad(w, ((z * rank, lr - (z + 1) * rank), (0, 0))).astype(BF16)

    vec = jnp.stack([rwkv_w0[j, 0], rwkv_w0[j, 1], rwkv_a0[j, 0], rwkv_a0[j, 1], rwkv_k_a[j],
                     rwkv_r_k[j].reshape(d), jnp.zeros((d,), F32), jnp.zeros((d,), F32)])
    hl = np.arange(256) // RWKV_HEAD
    return dict(
        mu=jnp.pad(rwkv_mu[j], ((0, 2), (0, 0))),
        w1c=first(rwkv_w1[j]), a1c=first(rwkv_a1[j]), g1=rwkv_g1[j].astype(BF16),
        wr=rwkv_w_rkv[j, 0].astype(BF16), wk=rwkv_w_rkv[j, 1].astype(BF16), wv=rwkv_w_rkv[j, 2].astype(BF16),
        w2f=second(rwkv_w2[j, 0], 0), w2b=second(rwkv_w2[j, 1], 1),
        a2f=second(rwkv_a2[j, 0], 0), a2b=second(rwkv_a2[j, 1], 1),
        g2=rwkv_g2[j].astype(BF16), vec=vec,
        ones_head=jnp.asarray(hl[:, None] == hl[None, :], BF16),
    )


def kernel(x, c, ctx, c_ctx, mod_w, mod_b, norm_w, ffn_w13, ffn_w2, fnet_w_o, fnet_b_o, rwkv_mu, rwkv_w_rkv,
           rwkv_w0, rwkv_w1, rwkv_w2, rwkv_a0, rwkv_a1, rwkv_a2, rwkv_g1, rwkv_g2, rwkv_k_k, rwkv_k_a, rwkv_r_k,
           rwkv_ln_w, rwkv_ln_b, rwkv_w_o, final_norm_w):
    bsz, seq, d = x.shape
    lc = ctx.shape[1]
    depth = mod_w.shape[0]
    assert depth == 2 and bsz == 2, "layer schedule below is written for the two-layer, batch-2 trunk"
    tm = TOKEN_TILE
    assert seq % tm == 0 and seq % GRID_W == 0

    c8 = jnp.concatenate([c, c_ctx[None], jnp.zeros((8 - bsz - 1, d), F32)], axis=0)
    mods_all = _adaln(c8, mod_w, mod_b)
    f2 = ffn_w13.shape[-1]
    w13 = ffn_w13.astype(BF16).reshape(2 * depth, d, f2)
    w2 = ffn_w2.astype(BF16).reshape(2 * depth, f2 // 2, d)
    fw = final_norm_w.reshape(1, d)

    xl = x.reshape(bsz * seq, d)
    xc = ctx.reshape(bsz * lc, d)
    lat_row = lambda i: i // (seq // tm)
    ctx_row = lambda i: 2
    tmc = bsz * lc if bsz * lc <= tm else lc

    mods = mods_all[0, :3].reshape(3 * N_MOD, 1, d)
    nw = norm_w[0].reshape(3, 1, d)
    xl, hl = _ffn(xl, mods, lat_row, 0, nw[0], w13, w2, 0, fw, tm=tm, next_nw=nw[1])
    xc, hc = _ffn(xc, mods, ctx_row, 0, nw[0], w13, w2, 0, fw, tm=tmc, next_nw=nw[1])
    wo = fnet_w_o[0].astype(BF16)
    bo = fnet_b_o[0].reshape(1, d)
    xl = _fourier_latent(xl, hl, bsz, seq, mods, wo, bo)
    xc = _fourier_ctx(xc, hc, bsz, lc, mods, wo, bo)
    xl = _ffn(xl, mods, lat_row, 2, nw[2], w13, w2, 1, fw, tm=tm)
    xc = _ffn(xc, mods, ctx_row, 2, nw[2], w13, w2, 1, fw, tm=tmc)

    mods = mods_all[1, :3].reshape(3 * N_MOD, 1, d)
    nw = norm_w[1].reshape(3, 1, d)
    xl = _ffn(xl, mods, lat_row, 0, nw[0], w13, w2, 2, fw, tm=tm)
    xc = _ffn(xc, mods, ctx_row, 0, nw[0], w13, w2, 2, fw, tm=tmc)
    prm = _rwkv_params(0, d, rwkv_mu, rwkv_w_rkv, rwkv_w0, rwkv_w1, rwkv_w2, rwkv_a0, rwkv_a1, rwkv_a2, rwkv_g1,
                       rwkv_g2, rwkv_k_a, rwkv_r_k)
    r_c, k_c, v_c, _, lwf_c, lwb_c, alf_c, alb_c, _ = _rwkv_proj(
        xc, mods, ctx_row, nw[1], prm, tm=lc, grid_mode=False, tpb=1)
    r_l, k_l, v_l, g_l, lwf_l, lwb_l, alf_l, alb_l, bon_l = _rwkv_proj(
        xl, mods, lat_row, nw[1], prm, tm=tm, grid_mode=True, tpb=seq // tm)
    kkp = rwkv_k_k[0].reshape(1, d)
    kap = rwkv_k_a[0].reshape(1, d)
    wd = WKV_HEADS_PER_BLOCK * RWKV_HEAD
    s0 = jnp.zeros((bsz, d // wd, wd, wd), F32)
    _, s_f = _wkv(r_c, k_c, v_c, lwf_c, alf_c, kkp, kap, s0, bsz, rev=False, need_y=False)
    _, s_b = _wkv(r_c, k_c, v_c, lwb_c, alb_c, kkp, kap, s0, bsz, rev=True, need_y=False)
    yf, _ = _wkv(r_l, k_l, v_l, lwf_l, alf_l, kkp, kap, s_f, bsz, rev=False, need_y=True)
    yb, _ = _wkv(r_l, k_l, v_l, lwb_l, alb_l, kkp, kap, s_b, bsz, rev=True, need_y=True)
    xl = _rwkv_out(yf, yb, bon_l, g_l, xl, mods, lambda i: i // (seq // MIX_TILE), rwkv_ln_w[0].reshape(1, d),
                   rwkv_ln_b[0].reshape(1, d), rwkv_w_o[0].astype(BF16), tm=MIX_TILE)
    xl = _ffn(xl, mods, lat_row, 2, nw[2], w13, w2, 3, fw, tm=tm, final_norm=True)
    return xl.reshape(bsz, seq, d)
```

```python
import functools
import math

import numpy as np
import jax
import jax.numpy as jnp
from jax import lax
from jax.experimental import pallas as pl
from jax.experimental.pallas import tpu as pltpu

F32 = jnp.float32
BF16 = jnp.bfloat16

NORM_EPS = 1e-6
GN_EPS = 64e-5
KK_EPS = 1e-24
GRID_W = 64
FNET_GROUPS = 8
RWKV_HEAD = 64
N_MOD = 9
DECAY_SCALE = math.exp(-0.5)

LANE = 128
TOKEN_TILE = 512
MIX_TILE = 256
FFN_HIDDEN_TILE = 512
PROJ_COL_TILE = 256
HEADSUM_TILE = 256
ROW_CHUNK = 16
ROW_UNROLL = 8
WKV_CHUNK = 64
WKV_HEADS_PER_BLOCK = 2
WKV_GROUP = 16
VMEM_LIMIT = 56 * 1024 * 1024


def _cparams(sem):
    return pltpu.CompilerParams(dimension_semantics=sem, vmem_limit_bytes=VMEM_LIMIT)


def _sigmoid(x):
    return 1.0 / (1.0 + jnp.exp(-x))


def _modulate(x, nw, shift, scale):
    y = x * lax.rsqrt(jnp.mean(x * x, axis=-1, keepdims=True) + NORM_EPS)
    return (y * nw) * (1.0 + scale) + shift


def _dot(a, b):
    return jnp.dot(a, b, preferred_element_type=F32)


def _dot_nt(a, b):
    return lax.dot_general(a, b, (((1,), (1,)), ((), ())), preferred_element_type=F32)


def _split2(x):
    hi = x.astype(BF16)
    lo = (x - hi.astype(F32)).astype(BF16)
    return hi, lo


def _dot_x2(x, w_bf16):
    hi, lo = _split2(x)
    return _dot(hi, w_bf16) + _dot(lo, w_bf16)


def _adaln_kernel(c_ref, w_ref, b_ref, o_ref):
    c = c_ref[...]
    s = c * _sigmoid(c)
    o_ref[...] = _dot(s.astype(BF16), w_ref[...].astype(BF16)) + b_ref[...]


def _adaln(c8, mod_w, mod_b):
    depth, d, nd = mod_w.shape
    tn = 1024 if nd % 1024 == 0 else 512
    return pl.pallas_call(
        _adaln_kernel,
        out_shape=jax.ShapeDtypeStruct((depth, 8, nd), F32),
        grid=(depth, nd // tn),
        in_specs=[
            pl.BlockSpec((8, d), lambda l, j: (0, 0)),
            pl.BlockSpec((None, d, tn), lambda l, j: (l, 0, j)),
            pl.BlockSpec((None, 1, tn), lambda l, j: (l, 0, j)),
        ],
        out_specs=pl.BlockSpec((None, 8, tn), lambda l, j: (l, 0, j)),
        compiler_params=_cparams(("arbitrary", "arbitrary")),
        name="adaln_linear",
    )(c8, mod_w, mod_b.reshape(depth, 1, nd))


def _ffn_kernel(*refs, final_norm, emit_h):
    x_ref, sh_ref, sc_ref, gt_ref, nw_ref, w1g_ref, w1u_ref, w2_ref, fw_ref = refs[:9]
    if emit_h:
        nw2_ref, sh2_ref, sc2_ref, o_ref, h2_ref, h_ref = refs[9:]
    else:
        o_ref, h_ref = refs[9:]
    j = pl.program_id(1)
    n_chunks = x_ref.shape[0] // ROW_CHUNK

    def rows_of(c):
        return pl.ds(pl.multiple_of(c * ROW_CHUNK, ROW_CHUNK), ROW_CHUNK)

    @pl.when(j == 0)
    def _():
        nw, sh, sc = nw_ref[...], sh_ref[...], sc_ref[...]

        def chunk(c, carry):
            rows = rows_of(c)
            h_ref[rows, :] = _modulate(x_ref[rows, :], nw, sh, sc).astype(BF16)
            o_ref[rows, :] = jnp.zeros((ROW_CHUNK, o_ref.shape[1]), F32)
            return carry

        lax.fori_loop(0, n_chunks, chunk, 0, unroll=ROW_UNROLL)

    h = h_ref[...]
    g = _dot(h, w1g_ref[...])
    u = _dot(h, w1u_ref[...])
    a = (g * _sigmoid(g)) * u
    o_ref[...] += _dot(a.astype(BF16), w2_ref[...])

    @pl.when(j == pl.num_programs(1) - 1)
    def _():
        y = x_ref[...] + (0.5 * gt_ref[...]) * o_ref[...]
        if final_norm:
            y = y * lax.rsqrt(jnp.mean(y * y, axis=-1, keepdims=True) + NORM_EPS) * fw_ref[...]
        o_ref[...] = y
        if emit_h:
            h2_ref[...] = _modulate(y, nw2_ref[...], sh2_ref[...], sc2_ref[...])


def _ffn(x, mods, row_of_tile, sub, nw, w13, w2, wi, fw, *, tm, final_norm=False, next_nw=None):
    n, d = x.shape
    f = w2.shape[1]
    tf = FFN_HIDDEN_TILE
    nf = f // tf
    emit_h = next_nw is not None

    def mod_spec(k):
        return pl.BlockSpec((None, 1, d), lambda i, j: (row_of_tile(i) * N_MOD + 3 * sub + k, 0, 0))

    vec_spec = pl.BlockSpec((1, d), lambda i, j: (0, 0))
    tile_spec = pl.BlockSpec((tm, d), lambda i, j: (i, 0))
    out = jax.ShapeDtypeStruct((n, d), F32)
    args = [x, mods, mods, mods, nw, w13, w13, w2, fw]
    specs = [tile_spec, mod_spec(0), mod_spec(1), mod_spec(2), vec_spec,
             pl.BlockSpec((None, d, tf), lambda i, j: (wi, 0, j)),
             pl.BlockSpec((None, d, tf), lambda i, j: (wi, 0, nf + j)),
             pl.BlockSpec((None, tf, d), lambda i, j: (wi, j, 0)),
             vec_spec]
    if emit_h:
        args += [next_nw, mods, mods]
        specs += [vec_spec, mod_spec(3), mod_spec(4)]
    return pl.pallas_call(
        functools.partial(_ffn_kernel, final_norm=final_norm, emit_h=emit_h),
        out_shape=[out, out] if emit_h else out,
        grid=(n // tm, nf),
        in_specs=specs,
        out_specs=[tile_spec, tile_spec] if emit_h else tile_spec,
        scratch_shapes=[pltpu.VMEM((tm, d), BF16)],
        compiler_params=_cparams(("arbitrary", "arbitrary")),
        name="ffn_swiglu",
    )(*args)


def _dft_tables(n):
    k = np.arange(n, dtype=np.int64)
    ang = 2.0 * np.pi * ((k[:, None] * k[None, :]) % n).astype(np.float64) / n
    return np.cos(ang), np.sin(ang)


def _hi_lo(m):
    m = jnp.asarray(m, F32)
    hi = m.astype(BF16)
    return hi, (m - hi.astype(F32)).astype(BF16)


DFT_SUB = 8
DFT_COLS = 1024
DFT_COLS_B = 512


def _dft_kernel(*refs, n_planes, mode, twiddle):
    it = iter(refs)
    x_ref = next(it)
    mats = [(next(it)[...], next(it)[...]) for _ in range(n_planes)]
    if twiddle:
        tc_ref, ts_ref = next(it), next(it)
    o_ref = next(it)
    if mode != "flat":
        stage_ref = next(it)

    def dense(p, strided):
        stage_ref[p] = strided
        return stage_ref[p]

    def transform(planes):
        y = None
        for x, (mh, ml) in zip(planes, mats):
            xh, xl = _split2(x)
            part = _dot(mh, xh) + (_dot(mh, xl) + _dot(ml, xh))
            y = part if y is None else y + part
        half = y.shape[0] // 2
        return y[:half], y[half:]

    if mode == "flat":
        yr, yi = transform([x_ref[...]])
        o_ref[0] = yr
        o_ref[1] = yi
    elif mode == "a":
        for j in range(DFT_SUB):
            yr, yi = transform([dense(0, x_ref[:, j, :])])
            if twiddle:
                reps = yr.shape[1] // LANE
                c = jnp.concatenate([tc_ref[j]] * reps, axis=1)
                s = jnp.concatenate([ts_ref[j]] * reps, axis=1)
                yr, yi = yr * c + yi * s, yi * c - yr * s
            o_ref[0, j] = yr
            o_ref[1, j] = yi
    else:
        for j in range(DFT_SUB):
            yr, yi = transform([dense(0, x_ref[0, :, j, :]), dense(1, x_ref[1, :, j, :])])
            o_ref[0, :, j, :] = yr
            o_ref[1, :, j, :] = yi


def _dft_call(mode, x, mats, out_shape, grid, x_spec, o_spec, tw=None, tw_spec=None):
    args, specs = [x], [x_spec]
    for m in mats:
        mh, ml = _hi_lo(m)
        args += [mh, ml]
        specs += [pl.BlockSpec(m.shape, lambda *_: (0, 0))] * 2
    if tw is not None:
        args += list(tw)
        specs += [tw_spec] * 2
    scratch = []
    if mode != "flat":
        scratch = [pltpu.VMEM((len(mats), mats[0].shape[1], x_spec.block_shape[-1]), F32)]
    return pl.pallas_call(
        functools.partial(_dft_kernel, n_planes=len(mats), mode=mode, twiddle=tw is not None),
        out_shape=jax.ShapeDtypeStruct(out_shape, F32),
        grid=grid,
        in_specs=specs,
        out_specs=o_spec,
        scratch_shapes=scratch,
        compiler_params=_cparams(("arbitrary",) * len(grid)),
        name="dft_" + mode,
    )(*args)


def _fmix_kernel(pr_ref, pi_ref, x_ref, gt_ref, ch_ref, cl_ref, wo_ref, bo_ref, o_ref, *, groups):
    pr, pi = pr_ref[...], pi_ref[...]
    gd = pr.shape[1] // groups
    ch, cl = ch_ref[...], cl_ref[...]
    outs = []
    for g in range(groups):
        z = jnp.concatenate([pr[:, g * gd:(g + 1) * gd], pi[:, g * gd:(g + 1) * gd]], axis=1)
        zh, zl = _split2(z)
        outs.append(_dot(zh, ch) + (_dot(zl, ch) + _dot(zh, cl)))
    f = jnp.concatenate(outs, axis=1)
    o = _dot(f.astype(BF16), wo_ref[...]) + bo_ref[...]
    o_ref[...] = x_ref[...] + gt_ref[...] * o


def _fmix(p, x, mods, row_of_tile, wo, bo, *, tm):
    n, d = x.shape
    b, _, l, _ = p.shape
    tpb = l // tm
    gd = d // FNET_GROUPS
    c, s = _dft_tables(gd)
    ch, cl = _hi_lo(np.concatenate([c, s], axis=0) / math.sqrt(gd))
    return pl.pallas_call(
        functools.partial(_fmix_kernel, groups=FNET_GROUPS),
        out_shape=jax.ShapeDtypeStruct((n, d), F32),
        grid=(n // tm,),
        in_specs=[
            pl.BlockSpec((None, None, tm, d), lambda i: (i // tpb, 0, i % tpb, 0)),
            pl.BlockSpec((None, None, tm, d), lambda i: (i // tpb, 1, i % tpb, 0)),
            pl.BlockSpec((tm, d), lambda i: (i, 0)),
            pl.BlockSpec((None, 1, d), lambda i: (row_of_tile(i) * N_MOD + 5, 0, 0)),
            pl.BlockSpec((2 * gd, gd), lambda i: (0, 0)),
            pl.BlockSpec((2 * gd, gd), lambda i: (0, 0)),
            pl.BlockSpec((d, d), lambda i: (0, 0)),
            pl.BlockSpec((1, d), lambda i: (0, 0)),
        ],
        out_specs=pl.BlockSpec((tm, d), lambda i: (i, 0)),
        compiler_params=_cparams(("arbitrary",)),
        name="fourier_out",
    )(p, p, x, mods, ch, cl, wo, bo)


def _fourier_latent(xl, hl, bsz, seq, mods, wo, bo):
    n, d = xl.shape
    la = lb = int(round(math.sqrt(seq)))
    assert la * lb == seq and lb % DFT_SUB == 0 and la % DFT_SUB == 0
    td = min(DFT_COLS, d)
    tdb = min(DFT_COLS_B, d)
    s8 = DFT_SUB
    ca, sa = _dft_tables(la)
    m_a = np.concatenate([ca, -sa], axis=0) / math.sqrt(la)
    cb, sb = _dft_tables(lb)
    m_br = np.concatenate([cb, -sb], axis=0) / math.sqrt(lb)
    m_bi = np.concatenate([sb, cb], axis=0) / math.sqrt(lb)
    n2 = np.arange(lb, dtype=np.int64)[:, None]
    k1 = np.arange(la, dtype=np.int64)[None, :]
    ang = 2.0 * np.pi * ((n2 * k1) % seq).astype(np.float64) / seq
    twc = jnp.asarray(np.broadcast_to(np.cos(ang)[:, :, None], (lb, la, LANE)), F32)
    tws = jnp.asarray(np.broadcast_to(np.sin(ang)[:, :, None], (lb, la, LANE)), F32)
    grid = (bsz, lb // s8, d // td)
    a = _dft_call(
        "a", hl.reshape(bsz, la, lb, d), [m_a], (bsz, 2, lb, la, d), grid,
        pl.BlockSpec((None, la, s8, td), lambda b, g, c: (b, 0, g, c)),
        pl.BlockSpec((None, 2, s8, la, td), lambda b, g, c: (b, 0, g, 0, c)),
        tw=(twc, tws), tw_spec=pl.BlockSpec((s8, la, LANE), lambda b, g, c: (g, 0, 0)))
    p = _dft_call(
        "b", a, [m_br, m_bi], (bsz, 2, lb, la, d), (bsz, la // s8, d // tdb),
        pl.BlockSpec((None, 2, lb, s8, tdb), lambda b, g, c: (b, 0, 0, g, c)),
        pl.BlockSpec((None, 2, lb, s8, tdb), lambda b, g, c: (b, 0, 0, g, c)))
    p = p.reshape(bsz, 2, seq, d)
    return _fmix(p, xl, mods, lambda i: i // (seq // MIX_TILE), wo, bo, tm=MIX_TILE)


def _fourier_ctx(xc, hc, bsz, lc, mods, wo, bo):
    n, d = xc.shape
    td = min(DFT_COLS, d)
    c, s = _dft_tables(lc)
    m = np.concatenate([c, -s], axis=0) / math.sqrt(lc)
    p = _dft_call(
        "flat", hc.reshape(bsz, lc, d), [m], (bsz, 2, lc, d), (bsz, d // td),
        pl.BlockSpec((None, lc, td), lambda b, c: (b, 0, c)),
        pl.BlockSpec((None, 2, lc, td), lambda b, c: (b, 0, 0, c)))
    return _fmix(p, xc, mods, lambda i: 2, wo, bo, tm=lc)


def _rwkv_proj_kernel(*refs, grid_mode, tpb):
    it = iter(refs)
    x_ref = next(it)
    if grid_mode:
        xp_ref, xn_ref = next(it), next(it)
    nw_ref, sh_ref, sc_ref, mu_ref = next(it), next(it), next(it), next(it)
    w1_ref, a1_ref, g1_ref = next(it), next(it), next(it)
    wr_ref, wk_ref, wv_ref = next(it), next(it), next(it)
    w2f_ref, w2b_ref, a2f_ref, a2b_ref, g2_ref = next(it), next(it), next(it), next(it), next(it)
    vec_ref, ones_ref = next(it), next(it)
    r_ref, k_ref, v_ref, g_ref, lwf_ref, lwb_ref, alf_ref, alb_ref, bon_ref = (next(it) for _ in range(9))
    hr_ref, hk_ref, hv_ref, tw_ref, ta_ref, tg_ref = (next(it) for _ in range(6))

    i = pl.program_id(0)
    j = pl.program_id(1)

    @pl.when(j == 0)
    def _():
        nw, sh, sc = nw_ref[...], sh_ref[...], sc_ref[...]
        h = _modulate(x_ref[...], nw, sh, sc)
        tm, d = h.shape
        q = d // 4
        row = lax.broadcasted_iota(jnp.int32, (tm, 1), 0)
        if grid_mode:
            tib = i % tpb
            up_ok = (tib != 0).astype(F32)
            dn_ok = (tib != tpb - 1).astype(F32)
            hp = _modulate(xp_ref[...], nw, sh, sc)[:, 2 * q:3 * q] * up_ok
            hn = _modulate(xn_ref[...], nw, sh, sc)[:, 3 * q:] * dn_ok
            col = row % GRID_W
            left = jnp.where(col != 0, pltpu.roll(h[:, :q], 1, 0), 0.0)
            right = jnp.where(col != GRID_W - 1, pltpu.roll(h[:, q:2 * q], tm - 1, 0), 0.0)
            up = jnp.concatenate([hp, h[:tm - GRID_W, 2 * q:3 * q]], axis=0)
            down = jnp.concatenate([h[GRID_W:, 3 * q:], hn], axis=0)
            hs = jnp.concatenate([left, right, up, down], axis=1)
        else:
            prev = jnp.where(row != 0, pltpu.roll(h, 1, 0), 0.0)
            nxt = jnp.where(row != tm - 1, pltpu.roll(h, tm - 1, 0), 0.0)
            hs = jnp.concatenate([prev[:, :q], nxt[:, q:2 * q], prev[:, 2 * q:3 * q], nxt[:, 3 * q:]], axis=1)
        dlt = hs - h
        mu = mu_ref[...]
        hr_ref[...] = (h + dlt * mu[0:1]).astype(BF16)
        hk_ref[...] = (h + dlt * mu[1:2]).astype(BF16)
        hv_ref[...] = (h + dlt * mu[2:3]).astype(BF16)
        tw_ref[...] = jnp.tanh(_dot((h + dlt * mu[3:4]).astype(BF16), w1_ref[...])).astype(BF16)
        ta_ref[...] = _dot((h + dlt * mu[4:5]).astype(BF16), a1_ref[...]).astype(BF16)
        tg_ref[...] = _sigmoid(_dot((h + dlt * mu[5:6]).astype(BF16), g1_ref[...])).astype(BF16)

    vec = vec_ref[...]
    w0f, w0b, a0f, a0b, ka, rk = (vec[n:n + 1] for n in range(6))
    r = _dot(hr_ref[...], wr_ref[...])
    k = _dot(hk_ref[...], wk_ref[...])
    v = _dot(hv_ref[...], wv_ref[...])
    tw = tw_ref[...]
    ta = ta_ref[...]
    lwf = -DECAY_SCALE * _sigmoid(w0f + _dot(tw, w2f_ref[...]))
    lwb = -DECAY_SCALE * _sigmoid(w0b + _dot(tw, w2b_ref[...]))
    alf = _sigmoid(a0f + _dot(ta, a2f_ref[...]))
    alb = _sigmoid(a0b + _dot(ta, a2b_ref[...]))
    kb = k * (1.0 + (0.5 * (alf + alb) - 1.0) * ka)
    r_ref[...] = r.astype(r_ref.dtype)
    k_ref[...] = k.astype(k_ref.dtype)
    v_ref[...] = v.astype(v_ref.dtype)
    g_ref[...] = _dot(tg_ref[...], g2_ref[...]).astype(g_ref.dtype)
    lwf_ref[...] = lwf
    lwb_ref[...] = lwb
    alf_ref[...] = alf.astype(alf_ref.dtype)
    alb_ref[...] = alb.astype(alb_ref.dtype)
    bon_ref[...] = (_dot_x2(r * kb * rk, ones_ref[...]) * v).astype(bon_ref.dtype)


def _rwkv_proj(x, mods, row_of_tile, nw, prm, *, tm, grid_mode, tpb):
    n, d = x.shape
    tn = PROJ_COL_TILE
    nj = d // tn
    hw = GRID_W
    lr = prm["w1c"].shape[1]

    def mod_spec(k):
        return pl.BlockSpec((None, 1, d), lambda i, j: (row_of_tile(i) * N_MOD + 3 + k, 0, 0))

    args, specs = [x], [pl.BlockSpec((tm, d), lambda i, j: (i, 0))]
    if grid_mode:
        nb = n // hw
        r = tm // hw
        args += [x, x]
        specs += [pl.BlockSpec((hw, d), lambda i, j: (jnp.maximum(i * r - 1, 0), 0)),
                  pl.BlockSpec((hw, d), lambda i, j: (jnp.minimum((i + 1) * r, nb - 1), 0))]
    args += [nw, mods, mods, prm["mu"], prm["w1c"], prm["a1c"], prm["g1"], prm["wr"], prm["wk"], prm["wv"],
             prm["w2f"], prm["w2b"], prm["a2f"], prm["a2b"], prm["g2"], prm["vec"], prm["ones_head"]]
    full = lambda shape: pl.BlockSpec(shape, lambda i, j: (0,) * len(shape))
    coltile = lambda rows: pl.BlockSpec((rows, tn), lambda i, j: (0, j))
    specs += [full((1, d)), mod_spec(0), mod_spec(1), full((8, d)), full((d, lr)), full((d, lr)), full((d, lr)),
              coltile(d), coltile(d), coltile(d), coltile(lr), coltile(lr), coltile(lr), coltile(lr), coltile(lr),
              coltile(8), full((tn, tn))]
    lo, hi = jax.ShapeDtypeStruct((n, d), BF16), jax.ShapeDtypeStruct((n, d), F32)
    return pl.pallas_call(
        functools.partial(_rwkv_proj_kernel, grid_mode=grid_mode, tpb=tpb),
        out_shape=[lo, lo, lo, lo, hi, hi, lo, lo, lo],
        grid=(n // tm, nj),
        in_specs=specs,
        out_specs=[pl.BlockSpec((tm, tn), lambda i, j: (i, j))] * 9,
        scratch_shapes=[pltpu.VMEM((tm, d), BF16)] * 3 + [pltpu.VMEM((tm, lr), BF16)] * 3,
        compiler_params=_cparams(("arbitrary", "arbitrary")),
        name="rwkv_proj",
    )(*args)


def _wkv_kernel(*refs, rev, chunk, heads, nq, need_y, group, add_y):
    if add_y:
        ya_ref, refs = refs[0], refs[1:]
    (r_ref, k_ref, v_ref, lw_ref, al_ref, kk_ref, ka_ref, s0_ref, bm_ref, ms_ref, mi_ref, tri_ref,
     ones_ref, eye_ref, lvl_ref, y_ref, sf_ref, s_scr) = refs
    cc = pl.program_id(1)
    c = chunk
    wd = heads * RWKV_HEAD
    rn = heads * c

    @pl.when(cc == 0)
    def _():
        s_scr[...] = s0_ref[...]

    bm = bm_ref[...]
    bmf = bm.astype(F32)
    strict = ms_ref[...] > 0.0
    incl = mi_ref[...] > 0.0
    tri = tri_ref[...]
    ones = ones_ref[...]
    eye = eye_ref[...]

    def stack(x):
        return jnp.concatenate([x.astype(BF16)] * heads, axis=0) * bm

    if not need_y:
        y_ref[...] = jnp.zeros_like(y_ref)

    def block(q):
        sl = slice(q * wd, (q + 1) * wd)
        r, k, v = r_ref[:, sl].astype(F32), k_ref[:, sl].astype(F32), v_ref[:, sl].astype(F32)
        lw, al = lw_ref[:, sl], al_ref[:, sl].astype(F32)
        kk0 = k * kk_ref[:, sl]
        kk = kk0 * lax.rsqrt(jnp.maximum(_dot_x2(kk0 * kk0, ones), KK_EPS))
        b = kk * al
        kd = k * (1.0 + (al - 1.0) * ka_ref[:, sl])
        lw_hi, lw_lo = _split2(lw)
        lg = _dot(tri, lw_hi) + _dot(tri, lw_lo)
        gc = lg[0:1] if rev else lg[c - 1:c]
        gin = jnp.exp(-lg)
        gout = jnp.exp(gc - lg)
        a_n = -kk * jnp.exp(lg - lw)
        r_n = r * jnp.exp(lg)
        a_s = stack(a_n)
        v_s = stack(v)
        bk = jnp.concatenate([stack(b * gin), stack(kd * gin)], axis=0)
        bo_n = (b * gout).astype(BF16)
        ko_n = (kd * gout).astype(BF16)
        yield

        lhs = jnp.concatenate([a_n, r_n], axis=0) if need_y else a_n
        gm = _dot_nt(lhs.astype(BF16), bk)
        gab = gm[:c, :rn]
        aak = jnp.where(strict, gm[:c, rn:], 0.0).astype(BF16)
        if need_y:
            arb = jnp.where(incl, gm[c:, :rn], 0.0).astype(BF16)
            ark = jnp.where(incl, gm[c:, rn:], 0.0).astype(BF16)
        yield
        wy = _dot(jnp.concatenate([aak, ark], axis=0) if need_y else aak, v_s)
        w0 = wy[:c]

        t = eye + jnp.where(lvl_ref[0] > 0.0, gab, 0.0)
        for lvl in range(1, int(math.log2(c))):
            n_l = jnp.where(lvl_ref[lvl] > 0.0, gab, 0.0)
            u = _dot(t.astype(BF16), stack(n_l))
            yield
            t = t + _dot(u.astype(BF16), stack(t))
            yield
        t = t.astype(BF16)
        av = _dot(t, jnp.concatenate([a_s, stack(w0)], axis=1))
        yield
        s = s_scr[q]
        sb = s.astype(BF16)
        if need_y:
            ry = _dot(arb, jnp.concatenate([stack(av[:, :wd]), stack(av[:, wd:])], axis=1))
            rh = r_n + ry[:, :wd]
            yk = wy[c:]
        pd = _dot(av.T.astype(BF16), bo_n)
        dt = (pd[wd:] + _dot(v.T.astype(BF16), ko_n)) * bmf
        pt = (pd[:wd] * bmf).astype(BF16)
        yield
        if need_y:
            y = _dot_nt(rh.astype(BF16), sb) + ry[:, wd:] + yk
            y_ref[:, sl] = y + ya_ref[:, sl] if add_y else y
        s_scr[q] = s * jnp.exp(gc) + _dot(sb, pt) + dt

    for q0 in range(0, nq, group):
        live = [block(q) for q in range(q0, min(q0 + group, nq))]
        while live:
            nxt = []
            for gen in live:
                try:
                    next(gen)
                    nxt.append(gen)
                except StopIteration:
                    pass
            live = nxt

    @pl.when(cc == pl.num_programs(1) - 1)
    def _():
        sf_ref[...] = s_scr[...]


def _wkv_consts(rev):
    c, g = WKV_CHUNK, WKV_HEADS_PER_BLOCK
    wd, rn = g * RWKV_HEAD, g * c
    assert c == RWKV_HEAD, "one (rn, wd) mask serves both (head, time) and (head, channel) columns"
    hrow = np.arange(rn) // c
    hlane = np.arange(wd) // RWKV_HEAD
    bm = (hrow[:, None] == hlane[None, :]).astype(np.float32)
    t = np.arange(c)[:, None]
    src = (np.arange(rn) % c)[None, :]
    if rev:
        strict, incl = src > t, src >= t
        tri = np.triu(np.ones((c, c), np.float32))
    else:
        strict, incl = src < t, src <= t
        tri = np.tril(np.ones((c, c), np.float32))
    ones = (hlane[:, None] == hlane[None, :]).astype(np.float32)
    levels = [strict & ((t >> (l + 1)) == (src >> (l + 1))) & ((t >> l) != (src >> l))
              for l in range(int(math.log2(c)))]
    return (jnp.asarray(bm, BF16), jnp.asarray(strict, F32), jnp.asarray(incl, F32), jnp.asarray(tri, BF16),
            jnp.asarray(ones, BF16), jnp.asarray(src == t, F32), jnp.asarray(np.stack(levels), F32))


def _wkv(r, k, v, lw, al, kkp, kap, s0, bsz, *, rev, need_y, y_add=None):
    n, d = r.shape
    c, g = WKV_CHUNK, WKV_HEADS_PER_BLOCK
    wd, rn = g * RWKV_HEAD, g * c
    nq = d // wd
    nc = n // bsz // c
    if rev:
        tok = lambda b, cc: (b * nc + (nc - 1 - cc), 0)
    else:
        tok = lambda b, cc: (b * nc + cc, 0)
    tspec = pl.BlockSpec((c, d), tok)
    full = lambda shape: pl.BlockSpec(shape, lambda b, cc: (0,) * len(shape))
    sspec = pl.BlockSpec((None, nq, wd, wd), lambda b, cc: (b, 0, 0, 0))
    consts = _wkv_consts(rev)
    y, sf = pl.pallas_call(
        functools.partial(_wkv_kernel, rev=rev, chunk=c, heads=g, nq=nq, need_y=need_y, group=WKV_GROUP,
                          add_y=y_add is not None),
        out_shape=[jax.ShapeDtypeStruct((n, d), F32), jax.ShapeDtypeStruct((bsz, nq, wd, wd), F32)],
        grid=(bsz, nc),
        in_specs=[tspec] * (5 + (y_add is not None)) + [full((1, d)), full((1, d)), sspec,
                                full((rn, wd)), full((c, rn)), full((c, rn)), full((c, c)), full((wd, wd)),
                                full((c, rn)), full((int(math.log2(c)), c, rn))],
        out_specs=[tspec, sspec],
        scratch_shapes=[pltpu.VMEM((nq, wd, wd), F32)],
        compiler_params=_cparams(("arbitrary", "arbitrary")),
        name="wkv_scan",
    )(*([] if y_add is None else [y_add]), r, k, v, lw, al, kkp, kap, s0, *consts)
    return y, sf


def _rwkv_out_kernel(y_ref, bon_ref, g_ref, x_ref, gt_ref, lnw_ref, lnb_ref, ones_ref, wo_ref, o_ref, *, wd):
    y = y_ref[...]
    ones = ones_ref[...]
    d = y.shape[1]
    inv = 1.0 / RWKV_HEAD

    def headsum(z, exact):
        parts = [z[:, q * wd:(q + 1) * wd] for q in range(d // wd)]
        return jnp.concatenate([_dot_x2(p, ones) if exact else _dot(p.astype(BF16), ones) for p in parts], axis=1)

    dev = y - headsum(y, True) * inv
    var = headsum(dev * dev, False) * inv
    yn = dev * lax.rsqrt(var + GN_EPS) * lnw_ref[...] + lnb_ref[...]
    o = (yn + bon_ref[...].astype(F32)) * g_ref[...].astype(F32)
    o_ref[...] = x_ref[...] + gt_ref[...] * _dot(o.astype(BF16), wo_ref[...])


def _rwkv_out(y, bon, g, x, mods, row_of_tile, lnw, lnb, wo, *, tm):
    n, d = x.shape
    wd = HEADSUM_TILE
    hl = np.arange(wd) // RWKV_HEAD
    ones = jnp.asarray(hl[:, None] == hl[None, :], BF16)
    tspec = pl.BlockSpec((tm, d), lambda i: (i, 0))
    full = lambda shape: pl.BlockSpec(shape, lambda i: (0,) * len(shape))
    return pl.pallas_call(
        functools.partial(_rwkv_out_kernel, wd=wd),
        out_shape=jax.ShapeDtypeStruct((n, d), F32),
        grid=(n // tm,),
        in_specs=[tspec] * 4 + [pl.BlockSpec((None, 1, d), lambda i: (row_of_tile(i) * N_MOD + 5, 0, 0)),
                                full((1, d)), full((1, d)), full((wd, wd)), full((d, d))],
        out_specs=tspec,
        compiler_params=_cparams(("arbitrary",)),
        name="rwkv_out",
    )(y, bon, g, x, mods, lnw, lnb, ones, wo)


def _rwkv_params(j, d, rwkv_mu, rwkv_w_rkv, rwkv_w0, rwkv_w1, rwkv_w2, rwkv_a0, rwkv_a1, rwkv_a2, rwkv_g1,
                 rwkv_g2, rwkv_k_a, rwkv_r_k):
    lr = rwkv_g1.shape[2]
    rank = rwkv_w1.shape[3]
    assert 2 * rank <= lr

    def first(w):
        return jnp.pad(jnp.concatenate([w[0], w[1]], axis=1), ((0, 0), (0, lr - 2 * rank))).astype(BF16)

    def second(w, z):
        return jnp.pad(w, ((z * rank, lr - (z + 1) * rank), (0, 0))).astype(BF16)

    vec = jnp.stack([rwkv_w0[j, 0], rwkv_w0[j, 1], rwkv_a0[j, 0], rwkv_a0[j, 1], rwkv_k_a[j],
                     rwkv_r_k[j].reshape(d), jnp.zeros((d,), F32), jnp.zeros((d,), F32)])
    hl = np.arange(256) // RWKV_HEAD
    return dict(
        mu=jnp.pad(rwkv_mu[j], ((0, 2), (0, 0))),
        w1c=first(rwkv_w1[j]), a1c=first(rwkv_a1[j]), g1=rwkv_g1[j].astype(BF16),
        wr=rwkv_w_rkv[j, 0].astype(BF16), wk=rwkv_w_rkv[j, 1].astype(BF16), wv=rwkv_w_rkv[j, 2].astype(BF16),
        w2f=second(rwkv_w2[j, 0], 0), w2b=second(rwkv_w2[j, 1], 1),
        a2f=second(rwkv_a2[j, 0], 0), a2b=second(rwkv_a2[j, 1], 1),
        g2=rwkv_g2[j].astype(BF16), vec=vec,
        ones_head=jnp.asarray(hl[:, None] == hl[None, :], BF16),
    )


def kernel(x, c, ctx, c_ctx, mod_w, mod_b, norm_w, ffn_w13, ffn_w2, fnet_w_o, fnet_b_o, rwkv_mu, rwkv_w_rkv,
           rwkv_w0, rwkv_w1, rwkv_w2, rwkv_a0, rwkv_a1, rwkv_a2, rwkv_g1, rwkv_g2, rwkv_k_k, rwkv_k_a, rwkv_r_k,
           rwkv_ln_w, rwkv_ln_b, rwkv_w_o, final_norm_w):
    bsz, seq, d = x.shape
    lc = ctx.shape[1]
    depth = mod_w.shape[0]
    assert depth == 2 and bsz == 2, "layer schedule below is written for the two-layer, batch-2 trunk"
    tm = TOKEN_TILE
    assert seq % tm == 0 and seq % GRID_W == 0

    c8 = jnp.concatenate([c, c_ctx[None], jnp.zeros((8 - bsz - 1, d), F32)], axis=0)
    mods_all = _adaln(c8, mod_w, mod_b)
    f2 = ffn_w13.shape[-1]
    w13 = ffn_w13.astype(BF16).reshape(2 * depth, d, f2)
    w2 = ffn_w2.astype(BF16).reshape(2 * depth, f2 // 2, d)
    fw = final_norm_w.reshape(1, d)

    xl = x.reshape(bsz * seq, d)
    xc = ctx.reshape(bsz * lc, d)
    lat_row = lambda i: i // (seq // tm)
    ctx_row = lambda i: 2
    tmc = bsz * lc if bsz * lc <= tm else lc

    mods = mods_all[0, :3].reshape(3 * N_MOD, 1, d)
    nw = norm_w[0].reshape(3, 1, d)
    xl, hl = _ffn(xl, mods, lat_row, 0, nw[0], w13, w2, 0, fw, tm=tm, next_nw=nw[1])
    xc, hc = _ffn(xc, mods, ctx_row, 0, nw[0], w13, w2, 0, fw, tm=tmc, next_nw=nw[1])
    wo = fnet_w_o[0].astype(BF16)
    bo = fnet_b_o[0].reshape(1, d)
    xl = _fourier_latent(xl, hl, bsz, seq, mods, wo, bo)
    xc = _fourier_ctx(xc, hc, bsz, lc, mods, wo, bo)
    xl = _ffn(xl, mods, lat_row, 2, nw[2], w13, w2, 1, fw, tm=tm)
    xc = _ffn(xc, mods, ctx_row, 2, nw[2], w13, w2, 1, fw, tm=tmc)

    mods = mods_all[1, :3].reshape(3 * N_MOD, 1, d)
    nw = norm_w[1].reshape(3, 1, d)
    xl = _ffn(xl, mods, lat_row, 0, nw[0], w13, w2, 2, fw, tm=tm)
    xc = _ffn(xc, mods, ctx_row, 0, nw[0], w13, w2, 2, fw, tm=tmc)
    prm = _rwkv_params(0, d, rwkv_mu, rwkv_w_rkv, rwkv_w0, rwkv_w1, rwkv_w2, rwkv_a0, rwkv_a1, rwkv_a2, rwkv_g1,
                       rwkv_g2, rwkv_k_a, rwkv_r_k)
    r_c, k_c, v_c, _, lwf_c, lwb_c, alf_c, alb_c, _ = _rwkv_proj(
        xc, mods, ctx_row, nw[1], prm, tm=lc, grid_mode=False, tpb=1)
    r_l, k_l, v_l, g_l, lwf_l, lwb_l, alf_l, alb_l, bon_l = _rwkv_proj(
        xl, mods, lat_row, nw[1], prm, tm=tm, grid_mode=True, tpb=seq // tm)
    kkp = rwkv_k_k[0].reshape(1, d)
    kap = rwkv_k_a[0].reshape(1, d)
    wd = WKV_HEADS_PER_BLOCK * RWKV_HEAD
    s0 = jnp.zeros((bsz, d // wd, wd, wd), F32)
    _, s_f = _wkv(r_c, k_c, v_c, lwf_c, alf_c, kkp, kap, s0, bsz, rev=False, need_y=False)
    _, s_b = _wkv(r_c, k_c, v_c, lwb_c, alb_c, kkp, kap, s0, bsz, rev=True, need_y=False)
    yf, _ = _wkv(r_l, k_l, v_l, lwf_l, alf_l, kkp, kap, s_f, bsz, rev=False, need_y=True)
    y, _ = _wkv(r_l, k_l, v_l, lwb_l, alb_l, kkp, kap, s_b, bsz, rev=True, need_y=True, y_add=yf)
    xl = _rwkv_out(y, bon_l, g_l, xl, mods, lambda i: i // (seq // MIX_TILE), rwkv_ln_w[0].reshape(1, d),
                   rwkv_ln_b[0].reshape(1, d), rwkv_w_o[0].astype(BF16), tm=MIX_TILE)
    xl = _ffn(xl, mods, lat_row, 2, nw[2], w13, w2, 3, fw, tm=tm, final_norm=True)
    return xl.reshape(bsz, seq, d)
```
